```python
import math
import jax
import jax.numpy as jnp
from jax import lax
import numpy as np

D_MODEL = 1024
BATCH = 4
SEQ = 4096
DEPTH = 4
DEC_BATCH = 32
DEC_SEQ = 8
PAST_LEN = 8192
PAGE_SIZE = 128

N_EVEN = (DEPTH + 1) // 2
N_ODD = DEPTH // 2
GROUP_W = D_MODEL // 2
MIX_W = 2 * GROUP_W
H_A = 4
DK_A = 128
DV_A = GROUP_W // H_A
HGRN_CHUNK = 16
F_FLOOR = 1e-30
H_B = 4
DH_B = 64
DV_B = 2 * DH_B
H_C = 4
G_C = 2
HPG_C = H_C // G_C
DK_C = 128
DV_C = GROUP_W // H_C
CMP_BLK = 32
SEL_BLK = 64
N_SEL = 16
WINDOW = 512
SEL_QBLK = 32
H_D = 4
DK_D = 128
DV_D = GROUP_W // H_D
MLSTM_CHUNK = 64
CONV_W = 4
QK_CH_D = 2 * H_D * DK_D
MEM_HEADS = 4
MEM_DH = 128
MEM_W = MEM_HEADS * MEM_DH
MEM_LEN = 256
D_FF = 2816
ROPE_THETA = 10000.0
QBLK = 128
EPS = 1e-6
NEG = -1e30
FORCE_BONUS = 1e4

EVEN_SPLITS = (H_A * DK_A, H_A * DK_A, H_A * DV_A, H_A * DV_A, H_B * DV_B, H_B * DV_B, H_B * DV_B)
ODD_SPLITS = (H_C * DK_C,) + (G_C * DK_C,) * 6 + (3 * H_C, QK_CH_D, H_D * DV_D, H_D, H_D, H_D * DV_D)
IN_EVEN = sum(EVEN_SPLITS)
IN_ODD = sum(ODD_SPLITS)

kernel_name = 'hybrid_hgrn2_diffattn_nsa_mlstm_decode_step'


def _rmsnorm(x, g):
    xf = x.astype(jnp.float32)
    y = xf * lax.rsqrt(jnp.mean(xf * xf, axis=-1, keepdims=True) + EPS)
    return (y * g.astype(jnp.float32)).astype(x.dtype)


def _swiglu(h, wg, wu, wd):
    return (jax.nn.silu(h @ wg) * (h @ wu)) @ wd


def _split(z, sizes):
    cuts = [int(c) for c in np.cumsum(sizes)[:-1]]
    return jnp.split(z, cuts, axis=-1)


def _rope(x, pos):
    half = x.shape[-1] // 2
    inv = ROPE_THETA ** (-jnp.arange(half, dtype=jnp.float32) / half)
    ang = pos.astype(jnp.float32)[:, None] * inv[None, :]
    cos = jnp.cos(ang)[None, :, None, :]
    sin = jnp.sin(ang)[None, :, None, :]
    xf = x.astype(jnp.float32)
    x1, x2 = xf[..., :half], xf[..., half:]
    return jnp.concatenate([x1 * cos - x2 * sin, x2 * cos + x1 * sin], axis=-1).astype(x.dtype)


def _qblock(T, pref):
    return pref if T % pref == 0 else T


def _blocked(fn, qb, *xs):
    B, T = xs[0].shape[:2]
    nb = T // qb
    def split(a):
        return jnp.moveaxis(a.reshape((B, nb, qb) + a.shape[2:]), 1, 0)
    out = lax.map(lambda args: fn(args[0], *args[1:]), (jnp.arange(nb),) + tuple(split(a) for a in xs))
    out = jnp.moveaxis(out, 0, 1)
    return out.reshape((B, T) + out.shape[3:])


def _gather_pages(pool, page_table):
    g = pool[page_table]
    return g.reshape((g.shape[0], g.shape[1] * g.shape[2]) + g.shape[3:])


def _chunked(a, c, fill):
    B, T = a.shape[:2]
    Tp = -(-T // c) * c
    a = jnp.pad(a, [(0, 0), (0, Tp - T)] + [(0, 0)] * (a.ndim - 2), constant_values=fill)
    a = a.reshape((B, Tp // c, c) + a.shape[2:])
    return jnp.transpose(a, (1, 0, 3, 2) + tuple(range(4, a.ndim)))


def _unchunk(o, T):
    nc, B, H, c, D = o.shape
    return jnp.transpose(o, (1, 0, 3, 2, 4)).reshape(B, nc * c, H, D)[:, :T]


def _hgrn_lower_bounds(lb_raw):
    sm = jax.nn.softmax(lb_raw.astype(jnp.float32), axis=0)
    return jnp.cumsum(sm, axis=0) - sm[0:1]


def _gla_scan(q, k, v, log_f, S0):
    B, T = q.shape[:2]
    c = HGRN_CHUNK if T >= HGRN_CHUNK else T
    tri = jnp.tril(jnp.ones((c, c), dtype=bool))[:, :, None]
    def step(S, xs):
        qc, kc, vc, gc = xs
        b = jnp.cumsum(gc, axis=2)
        o = jnp.einsum('bhtd,bhde->bhte', qc * jnp.exp(b), S)
        diff = b[:, :, :, None, :] - b[:, :, None, :, :]
        decay = jnp.where(tri, jnp.exp(jnp.where(tri, diff, 0.0)), 0.0)
        att = jnp.einsum('bhtd,bhtsd,bhsd->bhts', qc, decay, kc)
        o = o + jnp.einsum('bhts,bhse->bhte', att, vc)
        bl = b[:, :, -1:, :]
        S = jnp.exp(bl[:, :, 0, :])[..., None] * S + jnp.einsum('bhsd,bhse->bhde', kc * jnp.exp(bl - b), vc)
        return S, o
    xs = tuple(_chunked(a, c, 0.0) for a in (q, k, v, log_f))
    S, o = lax.scan(step, S0, xs)
    return _unchunk(o, T), S


def _hgrn2_mixer(zq, zf, zi, zg, lb, norm_g, S0):
    B, T, _ = zq.shape
    f32 = jnp.float32
    q = zq.reshape(B, T, H_A, DK_A).astype(f32)
    a = zf.reshape(B, T, H_A, DK_A).astype(f32)
    lbh = lb.reshape(H_A, DK_A)
    f = lbh + (1.0 - lbh) * jax.nn.sigmoid(a)
    log_f = jnp.log(jnp.maximum(f, F_FLOOR))
    k = (1.0 - lbh) * jax.nn.sigmoid(-a)
    v = zi.reshape(B, T, H_A, DV_A).astype(f32)
    o, S = _gla_scan(q, k, v, log_f, S0.astype(f32))
    o = _rmsnorm(o, norm_g) * jax.nn.silu(zg.reshape(B, T, H_A, DV_A).astype(f32))
    return o.reshape(B, T, H_A * DV_A).astype(zq.dtype), S


def _diff_mixer(zq, zk, zv, pos, past_kv, lam_vecs, norm_g, layer_idx):
    B, T, _ = zq.shape
    f32 = jnp.float32
    def rot(a):
        a = a.reshape(B, T, H_B, DV_B)
        return jnp.concatenate([_rope(a[..., :DH_B], pos), _rope(a[..., DH_B:], pos)], axis=-1)
    q, k = rot(zq), rot(zk)
    v = zv.reshape(B, T, H_B, DV_B)
    new_kv = jnp.stack([k, v], axis=2)
    if past_kv is None:
        k_all, v_all, n_prev = k, v, 0
    else:
        k_all = jnp.concatenate([past_kv[:, :, 0], k], axis=1)
        v_all = jnp.concatenate([past_kv[:, :, 1], v], axis=1)
        n_prev = past_kv.shape[1]
    lam_init = 0.8 - 0.6 * math.exp(-0.3 * layer_idx)
    lv = lam_vecs.astype(f32)
    lam = jnp.exp(jnp.sum(lv[0] * lv[1])) - jnp.exp(jnp.sum(lv[2] * lv[3])) + lam_init
    L = k_all.shape[1]
    k1, k2 = k_all[..., :DH_B], k_all[..., DH_B:]
    kidx = jnp.arange(L)
    qb = _qblock(T, QBLK)
    scale = DH_B ** -0.5
    def blk(j, q_b):
        qidx = n_prev + j * qb + jnp.arange(qb)
        mask = kidx[None, :] <= qidx[:, None]
        s1 = jnp.einsum('bqhd,bkhd->bhqk', q_b[..., :DH_B], k1).astype(f32) * scale
        s2 = jnp.einsum('bqhd,bkhd->bhqk', q_b[..., DH_B:], k2).astype(f32) * scale
        p = jax.nn.softmax(jnp.where(mask, s1, NEG), axis=-1) - lam * jax.nn.softmax(jnp.where(mask, s2, NEG), axis=-1)
        return jnp.einsum('bhqk,bkhe->bqhe', p.astype(v_all.dtype), v_all)
    o = _blocked(blk, qb, q)
    o = _rmsnorm(o, norm_g).astype(f32) * (1.0 - lam_init)
    return o.reshape(B, T, H_B * DV_B).astype(zq.dtype), new_kv


def _nsa_attend(q, rows, wrows, n_prev, n_prev_w, pos, gates, cmp_w, cmp_pe):
    B, T = q.shape[:2]
    L = rows.shape[1]
    f32 = jnp.float32
    scale = DK_C ** -0.5
    qg = q.reshape(B, T, G_C, HPG_C, DK_C)
    qr = _rope(q, pos).reshape(B, T, G_C, HPG_C, DK_C)
    kc, vc, ks, vs = rows[:, :, 0], rows[:, :, 1], rows[:, :, 2], rows[:, :, 3]
    n_c = L // CMP_BLK
    def compress(a, w, pe):
        blk = a[:, :n_c * CMP_BLK].reshape(B, n_c, CMP_BLK, G_C, DK_C) + pe[None, None, :, None, :]
        return jnp.einsum('bnlgd,lde->bnge', blk, w)
    k_cmp = compress(kc, cmp_w[0], cmp_pe[0])
    v_cmp = compress(vc, cmp_w[1], cmp_pe[1])
    s = jnp.einsum('btgpd,bngd->bgptn', qg, k_cmp).astype(f32) * scale
    cmask = ((jnp.arange(n_c) + 1) * CMP_BLK - 1)[None, :] <= pos[:, None]
    p_cmp = jax.nn.softmax(jnp.where(cmask, s, NEG), axis=-1) * jnp.any(cmask, axis=-1)[:, None]
    o_cmp = jnp.einsum('bgptn,bnge->btgpe', p_cmp.astype(v_cmp.dtype), v_cmp)
    R = SEL_BLK // CMP_BLK
    n_s = -(-L // SEL_BLK)
    imp = p_cmp.sum(axis=2)
    imp = jnp.pad(imp, ((0, 0), (0, 0), (0, 0), (0, n_s * R - n_c))).reshape(B, G_C, T, n_s, R).sum(-1)
    sb = jnp.arange(n_s)[None, :]
    cur = (pos // SEL_BLK)[:, None]
    valid = sb <= cur
    forced = (sb == 0) | (sb == cur) | (sb == cur - 1)
    score = jnp.where(valid, imp + FORCE_BONUS * forced.astype(f32), -1.0)
    n_top = min(N_SEL, n_s)
    top_v, top_i = lax.top_k(score, n_top)
    top_ok = jnp.transpose(top_v > -0.5, (0, 2, 1, 3))
    top_i = jnp.transpose(top_i, (0, 2, 1, 3))
    pad_s = n_s * SEL_BLK - L
    def blockify(a):
        a = jnp.pad(a, ((0, 0), (0, pad_s), (0, 0), (0, 0)))
        return jnp.transpose(a.reshape(B, n_s, SEL_BLK, G_C, DK_C), (0, 3, 1, 2, 4))
    ks_b, vs_b = blockify(ks), blockify(vs)
    gather = jax.vmap(jax.vmap(lambda a, i: a[i]))
    qs = _qblock(T, SEL_QBLK)
    def sel_fn(j, q_b, i_b, ok_b):
        nq = q_b.shape[1]
        idx = jnp.transpose(i_b, (0, 2, 1, 3))
        ok = jnp.transpose(ok_b, (0, 2, 1, 3))
        flat = idx.reshape(B, G_C, nq * n_top)
        kg = gather(ks_b, flat).reshape(B, G_C, nq, n_top * SEL_BLK, DK_C)
        vg = gather(vs_b, flat).reshape(B, G_C, nq, n_top * SEL_BLK, DV_C)
        kpos = (idx[..., None] * SEL_BLK + jnp.arange(SEL_BLK)).reshape(B, G_C, nq, n_top * SEL_BLK)
        tq = n_prev + j * qs + jnp.arange(nq)
        m = jnp.repeat(ok, SEL_BLK, axis=-1) & (kpos <= tq[None, None, :, None])
        sc = jnp.einsum('bqgpd,bgqkd->bgpqk', q_b, kg).astype(f32) * scale
        p = jax.nn.softmax(jnp.where(m[:, :, None], sc, NEG), axis=-1)
        return jnp.einsum('bgpqk,bgqke->bqgpe', p.astype(vg.dtype), vg)
    o_sel = _blocked(sel_fn, qs, qr, top_i, top_ok)
    padw = WINDOW - n_prev_w
    kwp = jnp.pad(wrows[:, :, 0], ((0, 0), (padw, 0), (0, 0), (0, 0)))
    vwp = jnp.pad(wrows[:, :, 1], ((0, 0), (padw, 0), (0, 0), (0, 0)))
    qw = _qblock(T, QBLK)
    def win_fn(j, q_b):
        start = j * qw
        kb = lax.dynamic_slice_in_dim(kwp, start, qw + WINDOW, axis=1)
        vb = lax.dynamic_slice_in_dim(vwp, start, qw + WINDOW, axis=1)
        pl = jnp.arange(qw + WINDOW)
        rel = jnp.arange(qw)[:, None] + WINDOW - pl[None, :]
        m = (rel >= 0) & (rel <= WINDOW) & ((start + pl) >= padw)[None, :]
        sc = jnp.einsum('bqgpd,bkgd->bgpqk', q_b, kb).astype(f32) * scale
        p = jax.nn.softmax(jnp.where(m, sc, NEG), axis=-1)
        return jnp.einsum('bgpqk,bkge->bqgpe', p.astype(vb.dtype), vb)
    o_win = _blocked(win_fn, qw, qr)
    g = gates.reshape(B, T, G_C, HPG_C, 3)
    o = g[..., 0:1] * o_cmp + g[..., 1:2] * o_sel + g[..., 2:3] * o_win
    return o.reshape(B, T, H_C * DV_C).astype(q.dtype)


def _nsa_mixer(zq, zkc, zvc, zks, zvs, zkw, zvw, zg, pos, past_rows, win_buf, cmp_w, cmp_pe):
    B, T, _ = zq.shape
    heads = lambda z: z.reshape(B, T, G_C, DK_C)
    q = zq.reshape(B, T, H_C, DK_C)
    rows = jnp.stack([heads(zkc), heads(zvc), _rope(heads(zks), pos), heads(zvs)], axis=2)
    wrows = jnp.stack([_rope(heads(zkw), pos), heads(zvw)], axis=2)
    if past_rows is None:
        all_rows, n_prev = rows, 0
    else:
        all_rows, n_prev = jnp.concatenate([past_rows, rows], axis=1), past_rows.shape[1]
    if win_buf is None:
        all_w, n_prev_w = wrows, 0
    else:
        all_w, n_prev_w = jnp.concatenate([win_buf.astype(wrows.dtype), wrows], axis=1), win_buf.shape[1]
    new_win = all_w[:, -min(WINDOW, n_prev_w + T):]
    gates = jax.nn.sigmoid(zg.reshape(B, T, H_C, 3).astype(jnp.float32))
    o = _nsa_attend(q, all_rows, all_w, n_prev, n_prev_w, pos, gates, cmp_w, cmp_pe)
    return o, rows, new_win


def _mlstm_scan(q, k, v, ig, lf, C0, n0, m0):
    B, T = q.shape[:2]
    c = MLSTM_CHUNK if T >= MLSTM_CHUNK else T
    tri = jnp.tril(jnp.ones((c, c), dtype=bool))
    def step(carry, xs):
        C, n, m = carry
        qc, kc, vc, ic, fc = xs
        b = jnp.cumsum(fc, axis=-1)
        a = ic - b
        mt = b + jnp.maximum(m[..., None], lax.cummax(a, axis=2))
        dprev = jnp.exp(b + m[..., None] - mt)
        Dm = jnp.exp(jnp.where(tri, a[:, :, None, :] + b[:, :, :, None] - mt[:, :, :, None], NEG))
        s = jnp.einsum('bhtd,bhsd->bhts', qc, kc) * Dm
        num = dprev[..., None] * jnp.einsum('bhtd,bhde->bhte', qc, C) + jnp.einsum('bhts,bhse->bhte', s, vc)
        den = dprev * jnp.einsum('bhtd,bhd->bht', qc, n) + s.sum(-1)
        h = num / jnp.maximum(jnp.abs(den), jnp.exp(-mt))[..., None]
        mL = mt[..., -1]
        w = jnp.exp(a + b[..., -1:] - mL[..., None])
        dl = jnp.exp(b[..., -1] + m - mL)
        C = dl[..., None, None] * C + jnp.einsum('bhs,bhsd,bhse->bhde', w, kc, vc)
        n = dl[..., None] * n + jnp.einsum('bhs,bhsd->bhd', w, kc)
        return (C, n, mL), h
    xs = (_chunked(q, c, 0.0), _chunked(k, c, 0.0), _chunked(v, c, 0.0), _chunked(ig, c, NEG), _chunked(lf, c, 0.0))
    (C, n, m), h = lax.scan(step, (C0, n0, m0), xs)
    return _unchunk(h, T), C, n, m


def _mlstm_mixer(zqk, zv, zi, zf, zo, conv_w, conv_b, gate_b, norm_g, conv_buf, C0, n0, m0):
    B, T, _ = zqk.shape
    f32 = jnp.float32
    xp = jnp.concatenate([conv_buf.astype(zqk.dtype), zqk], axis=1)
    y = conv_b
    for j in range(CONV_W):
        y = y + conv_w[j] * xp[:, j:j + T]
    qk = jax.nn.silu(y)
    q = qk[..., :H_D * DK_D].reshape(B, T, H_D, DK_D).astype(f32)
    k = qk[..., H_D * DK_D:].reshape(B, T, H_D, DK_D).astype(f32) * (DK_D ** -0.5)
    v = zv.reshape(B, T, H_D, DV_D).astype(f32)
    ig = zi.astype(f32) + gate_b[0].astype(f32)
    lf = jax.nn.log_sigmoid(zf.astype(f32) + gate_b[1].astype(f32))
    h, C, n, m = _mlstm_scan(q, k, v, ig, lf, C0.astype(f32), n0.astype(f32), m0.astype(f32))
    h = _rmsnorm(h, norm_g) * jax.nn.sigmoid(zo.reshape(B, T, H_D, DV_D).astype(f32))
    return h.reshape(B, T, H_D * DV_D).astype(zqk.dtype), xp[:, -(CONV_W - 1):], C, n, m


def _cross_attn(h, kv, wq, wo):
    B, T, _ = h.shape
    q = (h @ wq).reshape(B, T, MEM_HEADS, MEM_DH)
    s = jnp.einsum('bthd,bmhd->bhtm', q, kv[:, :, 0]).astype(jnp.float32) * (MEM_DH ** -0.5)
    p = jax.nn.softmax(s, axis=-1)
    o = jnp.einsum('bhtm,bmhd->bthd', p.astype(kv.dtype), kv[:, :, 1])
    return o.reshape(B, T, MEM_W) @ wo


def _run_group(x, pos0, mem_kv, past, W):
    B, T, _ = x.shape
    pos = pos0 + jnp.arange(T, dtype=jnp.int32)
    lbs = _hgrn_lower_bounds(W['hgrn_lb_raw'])
    new = {name: [] for name in ('diff_kv', 'hgrn', 'nsa_kv', 'nsa_win', 'm_C', 'm_n', 'm_m', 'm_conv')}
    for l in range(DEPTH):
        g = W['norm_w'][l]
        x = x + 0.5 * _swiglu(_rmsnorm(x, g[0]), W['ffn_w_gate'][l, 0], W['ffn_w_up'][l, 0], W['ffn_w_down'][l, 0])
        hn = _rmsnorm(x, g[1])
        i = l // 2
        if l % 2 == 0:
            zs = _split(hn @ W['w_in_even'][i], EVEN_SPLITS)
            if past is None:
                S0 = jnp.zeros((B, H_A, DK_A, DV_A), jnp.float32)
                past_kv = None
            else:
                S0 = past['hgrn'][i]
                past_kv = _gather_pages(past['diff_pool'][i], past['page_table'])
            oa, S = _hgrn2_mixer(zs[0], zs[1], zs[2], zs[3], lbs[i], W['hgrn_norm'][i], S0)
            ob, kv_new = _diff_mixer(zs[4], zs[5], zs[6], pos, past_kv, W['diff_lam'][i], W['diff_norm'][i], l)
            x = x + jnp.concatenate([oa, ob.astype(oa.dtype)], axis=-1) @ W['w_out_even'][i]
            new['hgrn'].append(S)
            new['diff_kv'].append(kv_new)
        else:
            zs = _split(hn @ W['w_in_odd'][i], ODD_SPLITS)
            if past is None:
                past_rows, win_buf = None, None
                conv_buf = jnp.zeros((B, CONV_W - 1, QK_CH_D), x.dtype)
                C0 = jnp.zeros((B, H_D, DK_D, DV_D), jnp.float32)
                n0 = jnp.zeros((B, H_D, DK_D), jnp.float32)
                m0 = jnp.zeros((B, H_D), jnp.float32)
            else:
                past_rows = _gather_pages(past['nsa_pool'][i], past['page_table'])
                win_buf = past['nsa_win'][i]
                conv_buf, C0, n0, m0 = past['m_conv'][i], past['m_C'][i], past['m_n'][i], past['m_m'][i]
            oc, rows, win_new = _nsa_mixer(zs[0], zs[1], zs[2], zs[3], zs[4], zs[5], zs[6], zs[7], pos,
                                           past_rows, win_buf, W['nsa_cmp_w'][i], W['nsa_cmp_pe'][i])
            od, conv_new, C, n, m = _mlstm_mixer(zs[8], zs[9], zs[10], zs[11], zs[12], W['mlstm_conv_w'][i],
                                                 W['mlstm_conv_b'][i], W['mlstm_gate_b'][i], W['mlstm_norm'][i],
                                                 conv_buf, C0, n0, m0)
            x = x + jnp.concatenate([oc, od.astype(oc.dtype)], axis=-1) @ W['w_out_odd'][i]
            new['nsa_kv'].append(rows)
            new['nsa_win'].append(win_new)
            new['m_C'].append(C)
            new['m_n'].append(n)
            new['m_m'].append(m)
            new['m_conv'].append(conv_new)
        x = x + _cross_attn(_rmsnorm(x, g[2]), mem_kv[l], W['mem_wq'][l], W['mem_wo'][l])
        x = x + 0.5 * _swiglu(_rmsnorm(x, g[3]), W['ffn_w_gate'][l, 1], W['ffn_w_up'][l, 1], W['ffn_w_down'][l, 1])
    return _rmsnorm(x, W['norm_final']), new


def setup_inputs(seed: int = 0) -> dict:
    key = jax.random.key(seed)
    ks = iter(jax.random.split(key, 48))
    f32 = jnp.float32
    nrm = lambda shape, scale=1.0: scale * jax.random.normal(next(ks), shape, f32)
    gain = lambda shape: 1.0 + 0.02 * jax.random.normal(next(ks), shape, f32)
    n_pages = PAST_LEN // PAGE_SIZE
    n_used = DEC_BATCH * n_pages
    n_pool = (5 * n_used + 3) // 4
    w_buf = min(WINDOW, PAST_LEN)
    page_table = jax.random.permutation(next(ks), n_pool)[:n_used].reshape(DEC_BATCH, n_pages).astype(jnp.int32)
    mlstm_gate_b = jnp.stack([nrm((N_ODD, H_D), 0.1), 3.0 + nrm((N_ODD, H_D), 0.1)], axis=1)
    D = D_MODEL
    return {
        'x_prompt': nrm((BATCH, SEQ, D)),
        'x_sample': nrm((DEC_BATCH, DEC_SEQ, D)),
        'mem_prompt': nrm((BATCH, MEM_LEN, D)),
        'cache_diff_kv': nrm((N_EVEN, n_pool, PAGE_SIZE, 2, H_B, DV_B)),
        'cache_nsa_kv': nrm((N_ODD, n_pool, PAGE_SIZE, 4, G_C, DK_C)),
        'state_nsa_win': nrm((N_ODD, DEC_BATCH, w_buf, 2, G_C, DK_C)),
        'state_hgrn': nrm((N_EVEN, DEC_BATCH, H_A, DK_A, DV_A), 0.5),
        'state_mlstm_C': nrm((N_ODD, DEC_BATCH, H_D, DK_D, DV_D), 0.5),
        'state_mlstm_n': nrm((N_ODD, DEC_BATCH, H_D, DK_D), 0.5),
        'state_mlstm_m': nrm((N_ODD, DEC_BATCH, H_D)),
        'state_mlstm_conv': nrm((N_ODD, DEC_BATCH, CONV_W - 1, QK_CH_D)),
        'cache_mem_kv': nrm((DEPTH, DEC_BATCH, MEM_LEN, 2, MEM_HEADS, MEM_DH)),
        'page_table': page_table,
        'norm_w': gain((DEPTH, 4, D)),
        'norm_final': gain((D,)),
        'ffn_w_gate': nrm((DEPTH, 2, D, D_FF), D ** -0.5),
        'ffn_w_up': nrm((DEPTH, 2, D, D_FF), D ** -0.5),
        'ffn_w_down': nrm((DEPTH, 2, D_FF, D), D_FF ** -0.5),
        'w_in_even': nrm((N_EVEN, D, IN_EVEN), D ** -0.5),
        'w_out_even': nrm((N_EVEN, MIX_W, D), MIX_W ** -0.5),
        'w_in_odd': nrm((N_ODD, D, IN_ODD), D ** -0.5),
        'w_out_odd': nrm((N_ODD, MIX_W, D), MIX_W ** -0.5),
        'hgrn_lb_raw': nrm((N_EVEN, H_A * DK_A)),
        'hgrn_norm': gain((N_EVEN, DV_A)),
        'diff_lam': nrm((N_EVEN, 4, DH_B), 0.1),
        'diff_norm': gain((N_EVEN, DV_B)),
        'nsa_cmp_w': nrm((N_ODD, 2, CMP_BLK, DK_C, DK_C), (CMP_BLK * DK_C) ** -0.5),
        'nsa_cmp_pe': nrm((N_ODD, 2, CMP_BLK, DK_C), 0.1),
        'mlstm_conv_w': nrm((N_ODD, CONV_W, QK_CH_D), CONV_W ** -0.5),
        'mlstm_conv_b': nrm((N_ODD, QK_CH_D), 0.01),
        'mlstm_gate_b': mlstm_gate_b,
        'mlstm_norm': gain((N_ODD, DV_D)),
        'mem_wq': nrm((DEPTH, D, MEM_W), D ** -0.5),
        'mem_wkv': nrm((DEPTH, D, 2 * MEM_W), D ** -0.5),
        'mem_wo': nrm((DEPTH, MEM_W, D), MEM_W ** -0.5),
    }


def reference(x_prompt, x_sample, mem_prompt, cache_diff_kv, cache_nsa_kv, state_nsa_win, state_hgrn,
              state_mlstm_C, state_mlstm_n, state_mlstm_m, state_mlstm_conv, cache_mem_kv, page_table,
              norm_w, norm_final, ffn_w_gate, ffn_w_up, ffn_w_down, w_in_even, w_out_even, w_in_odd, w_out_odd,
              hgrn_lb_raw, hgrn_norm, diff_lam, diff_norm, nsa_cmp_w, nsa_cmp_pe, mlstm_conv_w, mlstm_conv_b,
              mlstm_gate_b, mlstm_norm, mem_wq, mem_wkv, mem_wo):
    W = {'norm_w': norm_w, 'norm_final': norm_final, 'ffn_w_gate': ffn_w_gate, 'ffn_w_up': ffn_w_up,
         'ffn_w_down': ffn_w_down, 'w_in_even': w_in_even, 'w_out_even': w_out_even, 'w_in_odd': w_in_odd,
         'w_out_odd': w_out_odd, 'hgrn_lb_raw': hgrn_lb_raw, 'hgrn_norm': hgrn_norm, 'diff_lam': diff_lam,
         'diff_norm': diff_norm, 'nsa_cmp_w': nsa_cmp_w, 'nsa_cmp_pe': nsa_cmp_pe, 'mlstm_conv_w': mlstm_conv_w,
         'mlstm_conv_b': mlstm_conv_b, 'mlstm_gate_b': mlstm_gate_b, 'mlstm_norm': mlstm_norm,
         'mem_wq': mem_wq, 'mem_wo': mem_wo}
    Bp, M = mem_prompt.shape[0], mem_prompt.shape[1]
    mem_kv_prompt = [(mem_prompt @ mem_wkv[l]).reshape(Bp, M, 2, MEM_HEADS, MEM_DH) for l in range(DEPTH)]
    y_prompt, newp = _run_group(x_prompt, 0, mem_kv_prompt, None, W)
    past = {'page_table': page_table, 'diff_pool': cache_diff_kv, 'nsa_pool': cache_nsa_kv,
            'nsa_win': state_nsa_win, 'hgrn': state_hgrn, 'm_C': state_mlstm_C, 'm_n': state_mlstm_n,
            'm_m': state_mlstm_m, 'm_conv': state_mlstm_conv}
    past_len = page_table.shape[1] * PAGE_SIZE
    y_sample, news = _run_group(x_sample, past_len, cache_mem_kv, past, W)
    return (y_prompt, y_sample,
            jnp.stack(newp['diff_kv']), jnp.stack(news['diff_kv']),
            jnp.stack(newp['nsa_kv']), jnp.stack(news['nsa_kv']),
            jnp.stack(newp['nsa_win']), jnp.stack(news['nsa_win']),
            jnp.stack(newp['hgrn']), jnp.stack(news['hgrn']),
            jnp.stack(newp['m_C']), jnp.stack(news['m_C']),
            jnp.stack(newp['m_n']), jnp.stack(news['m_n']),
            jnp.stack(newp['m_m']), jnp.stack(news['m_m']),
            jnp.stack(newp['m_conv']), jnp.stack(news['m_conv']),
            jnp.stack(mem_kv_prompt))
```

```python
import functools
import math

import numpy as np
import jax
import jax.numpy as jnp
from jax import lax
from jax.experimental import pallas as pl
from jax.experimental.pallas import tpu as pltpu

F32 = jnp.float32
BF16 = jnp.bfloat16

D_MODEL = 1024
DEPTH = 4
PAGE_SIZE = 128
N_EVEN = (DEPTH + 1) // 2
N_ODD = DEPTH // 2
GROUP_W = D_MODEL // 2
MIX_W = 2 * GROUP_W
H_A = 4
DK_A = 128
DV_A = GROUP_W // H_A
HGRN_CHUNK = 16
F_FLOOR = 1e-30
H_B = 4
DH_B = 64
DV_B = 2 * DH_B
H_C = 4
G_C = 2
HPG_C = H_C // G_C
DK_C = 128
DV_C = GROUP_W // H_C
CMP_BLK = 32
SEL_BLK = 64
N_SEL = 16
WINDOW = 512
SEL_QBLK = 32
H_D = 4
DK_D = 128
DV_D = GROUP_W // H_D
MLSTM_CHUNK = 64
CONV_W = 4
QK_CH_D = 2 * H_D * DK_D
MEM_HEADS = 4
MEM_DH = 128
MEM_W = MEM_HEADS * MEM_DH
MEM_LEN = 256
D_FF = 2816
ROPE_THETA = 10000.0
QBLK = 128
EPS = 1e-6
NEG = -1e30
FORCE_BONUS = 1e4

EVEN_SPLITS = (H_A * DK_A, H_A * DK_A, H_A * DV_A, H_A * DV_A, H_B * DV_B, H_B * DV_B, H_B * DV_B)
ODD_SPLITS = (H_C * DK_C,) + (G_C * DK_C,) * 6 + (3 * H_C, QK_CH_D, H_D * DV_D, H_D, H_D, H_D * DV_D)
IN_EVEN = sum(EVEN_SPLITS)
IN_ODD = sum(ODD_SPLITS)

VMEM_LIMIT_BYTES = 56 * 1024 * 1024
FFN_CHUNK = D_FF // 2


def _cparams(*sem):
    return pltpu.CompilerParams(dimension_semantics=sem, vmem_limit_bytes=VMEM_LIMIT_BYTES)


def _row_tile(m):
    for t in (512, 256, 128, 64, 32, 16, 8):
        if m % t == 0:
            return t
    raise ValueError(f"row count {m} is not a multiple of 8")


def _resident(shape):
    return pl.BlockSpec(shape, lambda *_: (0,) * len(shape), pipeline_mode=pl.Buffered(1))


def _rms(x, g):
    return x * lax.rsqrt(jnp.mean(x * x, axis=-1, keepdims=True) + EPS) * g


def _ffn_kernel(x_ref, g_ref, wg_ref, wu_ref, wd_ref, o_ref):
    x = x_ref[...]
    h = _rms(x, g_ref[...]).astype(BF16)
    acc = jnp.zeros_like(x)
    for c in range(D_FF // FFN_CHUNK):
        sl = slice(c * FFN_CHUNK, (c + 1) * FFN_CHUNK)
        gate = jnp.dot(h, wg_ref[:, sl], preferred_element_type=F32)
        up = jnp.dot(h, wu_ref[:, sl], preferred_element_type=F32)
        a = (jax.nn.silu(gate) * up).astype(BF16)
        acc = acc + jnp.dot(a, wd_ref[sl, :], preferred_element_type=F32)
    o_ref[...] = x + 0.5 * acc


def _ffn(x, g, wg, wu, wd):
    m, d = x.shape
    tm = _row_tile(m)
    return pl.pallas_call(
        _ffn_kernel,
        grid=(m // tm,),
        in_specs=[pl.BlockSpec((tm, d), lambda i: (i, 0)), _resident((1, d)),
                  _resident(wg.shape), _resident(wu.shape), _resident(wd.shape)],
        out_specs=pl.BlockSpec((tm, d), lambda i: (i, 0)),
        out_shape=jax.ShapeDtypeStruct((m, d), F32),
        compiler_params=_cparams("parallel"),
        name="ffn",
    )(x, g.reshape(1, d), wg, wu, wd)


def _normmm_kernel(x_ref, g_ref, w_ref, o_ref, *, norm):
    x = x_ref[...]
    if norm:
        x = _rms(x, g_ref[...])
    o_ref[...] = jnp.dot(x.astype(BF16), w_ref[...], preferred_element_type=F32)


def _normmm(x, g, w, *, norm=True):
    m, d = x.shape
    n = w.shape[1]
    tm = _row_tile(m)
    return pl.pallas_call(
        functools.partial(_normmm_kernel, norm=norm),
        grid=(m // tm,),
        in_specs=[pl.BlockSpec((tm, d), lambda i: (i, 0)), _resident((1, d)), _resident(w.shape)],
        out_specs=pl.BlockSpec((tm, n), lambda i: (i, 0)),
        out_shape=jax.ShapeDtypeStruct((m, n), F32),
        compiler_params=_cparams("parallel"),
        name="normmm",
    )(x, g.reshape(1, d), w)


def _mmres_kernel(*refs, n_in):
    x_ref, o_ref = refs[0], refs[-1]
    acc = x_ref[...]
    for i in range(n_in):
        acc = acc + jnp.dot(refs[1 + i][...].astype(BF16), refs[1 + n_in + i][...], preferred_element_type=F32)
    o_ref[...] = acc


def _mmres(x, acts, ws):
    m, d = x.shape
    tm = _row_tile(m)
    n_in = len(acts)
    return pl.pallas_call(
        functools.partial(_mmres_kernel, n_in=n_in),
        grid=(m // tm,),
        in_specs=([pl.BlockSpec((tm, d), lambda i: (i, 0))]
                  + [pl.BlockSpec((tm, a.shape[1]), lambda i: (i, 0)) for a in acts]
                  + [_resident(w.shape) for w in ws]),
        out_specs=pl.BlockSpec((tm, d), lambda i: (i, 0)),
        out_shape=jax.ShapeDtypeStruct((m, d), F32),
        compiler_params=_cparams("parallel"),
        name="mmres",
    )(x, *acts, *ws)


def _rmsnorm_kernel(x_ref, g_ref, o_ref):
    o_ref[...] = _rms(x_ref[...], g_ref[...])


def _rmsnorm_rows(x, g):
    m, d = x.shape
    tm = _row_tile(m)
    return pl.pallas_call(
        _rmsnorm_kernel,
        grid=(m // tm,),
        in_specs=[pl.BlockSpec((tm, d), lambda i: (i, 0)), _resident((1, d))],
        out_specs=pl.BlockSpec((tm, d), lambda i: (i, 0)),
        out_shape=jax.ShapeDtypeStruct((m, d), F32),
        compiler_params=_cparams("parallel"),
        name="final_norm",
    )(x, g.reshape(1, d))


def _xattn_kernel(q_ref, kv_ref, o_ref):
    scale = MEM_DH ** -0.5
    for h in range(MEM_HEADS):
        lo, hi = h * MEM_DH, (h + 1) * MEM_DH
        q = q_ref[:, lo:hi].astype(BF16)
        k = kv_ref[0, :, lo:hi].astype(BF16)
        v = kv_ref[0, :, MEM_W + lo:MEM_W + hi].astype(BF16)
        s = lax.dot_general(q, k, (((1,), (1,)), ((), ())), preferred_element_type=F32) * scale
        p = jnp.exp(s - jnp.max(s, axis=-1, keepdims=True))
        l = jnp.sum(p, axis=-1, keepdims=True)
        o = jnp.dot(p.astype(BF16), v, preferred_element_type=F32)
        o_ref[:, lo:hi] = o / l


def _xattn(q, kv, t):
    b = kv.shape[0]
    tq = min(t, 512)
    nq = t // tq
    return pl.pallas_call(
        _xattn_kernel,
        grid=(b, nq),
        in_specs=[pl.BlockSpec((tq, MEM_W), lambda i, j: (i * nq + j, 0)),
                  pl.BlockSpec((1, MEM_LEN, 2 * MEM_W), lambda i, j: (i, 0, 0))],
        out_specs=pl.BlockSpec((tq, MEM_W), lambda i, j: (i * nq + j, 0)),
        out_shape=jax.ShapeDtypeStruct((b * t, MEM_W), F32),
        compiler_params=_cparams("parallel", "parallel"),
        name="xattn",
    )(q, kv)


def _j_rmsnorm(x, g):
    xf = x.astype(jnp.float32)
    y = xf * lax.rsqrt(jnp.mean(xf * xf, axis=-1, keepdims=True) + EPS)
    return (y * g.astype(jnp.float32)).astype(x.dtype)


def _j_split(z, sizes):
    cuts = [int(c) for c in np.cumsum(sizes)[:-1]]
    return jnp.split(z, cuts, axis=-1)


def _j_rope(x, pos):
    half = x.shape[-1] // 2
    inv = ROPE_THETA ** (-jnp.arange(half, dtype=jnp.float32) / half)
    ang = pos.astype(jnp.float32)[:, None] * inv[None, :]
    cos = jnp.cos(ang)[None, :, None, :]
    sin = jnp.sin(ang)[None, :, None, :]
    xf = x.astype(jnp.float32)
    x1, x2 = xf[..., :half], xf[..., half:]
    return jnp.concatenate([x1 * cos - x2 * sin, x2 * cos + x1 * sin], axis=-1).astype(x.dtype)


def _j_qblock(T, pref):
    return pref if T % pref == 0 else T


def _j_blocked(fn, qb, *xs):
    B, T = xs[0].shape[:2]
    nb = T // qb
    def split(a):
        return jnp.moveaxis(a.reshape((B, nb, qb) + a.shape[2:]), 1, 0)
    out = lax.map(lambda args: fn(args[0], *args[1:]), (jnp.arange(nb),) + tuple(split(a) for a in xs))
    out = jnp.moveaxis(out, 0, 1)
    return out.reshape((B, T) + out.shape[3:])


def _j_gather_pages(pool, page_table):
    g = pool[page_table]
    return g.reshape((g.shape[0], g.shape[1] * g.shape[2]) + g.shape[3:])


def _j_chunked(a, c, fill):
    B, T = a.shape[:2]
    Tp = -(-T // c) * c
    a = jnp.pad(a, [(0, 0), (0, Tp - T)] + [(0, 0)] * (a.ndim - 2), constant_values=fill)
    a = a.reshape((B, Tp // c, c) + a.shape[2:])
    return jnp.transpose(a, (1, 0, 3, 2) + tuple(range(4, a.ndim)))


def _j_unchunk(o, T):
    nc, B, H, c, D = o.shape
    return jnp.transpose(o, (1, 0, 3, 2, 4)).reshape(B, nc * c, H, D)[:, :T]


def _j_hgrn_lower_bounds(lb_raw):
    sm = jax.nn.softmax(lb_raw.astype(jnp.float32), axis=0)
    return jnp.cumsum(sm, axis=0) - sm[0:1]


def _j_gla_scan(q, k, v, log_f, S0):
    B, T = q.shape[:2]
    c = HGRN_CHUNK if T >= HGRN_CHUNK else T
    tri = jnp.tril(jnp.ones((c, c), dtype=bool))[:, :, None]
    def step(S, xs):
        qc, kc, vc, gc = xs
        b = jnp.cumsum(gc, axis=2)
        o = jnp.einsum('bhtd,bhde->bhte', qc * jnp.exp(b), S)
        diff = b[:, :, :, None, :] - b[:, :, None, :, :]
        decay = jnp.where(tri, jnp.exp(jnp.where(tri, diff, 0.0)), 0.0)
        att = jnp.einsum('bhtd,bhtsd,bhsd->bhts', qc, decay, kc)
        o = o + jnp.einsum('bhts,bhse->bhte', att, vc)
        bl = b[:, :, -1:, :]
        S = jnp.exp(bl[:, :, 0, :])[..., None] * S + jnp.einsum('bhsd,bhse->bhde', kc * jnp.exp(bl - b), vc)
        return S, o
    xs = tuple(_j_chunked(a, c, 0.0) for a in (q, k, v, log_f))
    S, o = lax.scan(step, S0, xs)
    return _j_unchunk(o, T), S


def _j_hgrn2_mixer(zq, zf, zi, zg, lb, norm_g, S0):
    B, T, _ = zq.shape
    f32 = jnp.float32
    q = zq.reshape(B, T, H_A, DK_A).astype(f32)
    a = zf.reshape(B, T, H_A, DK_A).astype(f32)
    lbh = lb.reshape(H_A, DK_A)
    f = lbh + (1.0 - lbh) * jax.nn.sigmoid(a)
    log_f = jnp.log(jnp.maximum(f, F_FLOOR))
    k = (1.0 - lbh) * jax.nn.sigmoid(-a)
    v = zi.reshape(B, T, H_A, DV_A).astype(f32)
    o, S = _j_gla_scan(q, k, v, log_f, S0.astype(f32))
    o = _j_rmsnorm(o, norm_g) * jax.nn.silu(zg.reshape(B, T, H_A, DV_A).astype(f32))
    return o.reshape(B, T, H_A * DV_A).astype(zq.dtype), S


def _j_diff_mixer(zq, zk, zv, pos, past_kv, lam_vecs, norm_g, layer_idx):
    B, T, _ = zq.shape
    f32 = jnp.float32
    def rot(a):
        a = a.reshape(B, T, H_B, DV_B)
        return jnp.concatenate([_j_rope(a[..., :DH_B], pos), _j_rope(a[..., DH_B:], pos)], axis=-1)
    q, k = rot(zq), rot(zk)
    v = zv.reshape(B, T, H_B, DV_B)
    new_kv = jnp.stack([k, v], axis=2)
    if past_kv is None:
        k_all, v_all, n_prev = k, v, 0
    else:
        k_all = jnp.concatenate([past_kv[:, :, 0], k], axis=1)
        v_all = jnp.concatenate([past_kv[:, :, 1], v], axis=1)
        n_prev = past_kv.shape[1]
    lam_init = 0.8 - 0.6 * math.exp(-0.3 * layer_idx)
    lv = lam_vecs.astype(f32)
    lam = jnp.exp(jnp.sum(lv[0] * lv[1])) - jnp.exp(jnp.sum(lv[2] * lv[3])) + lam_init
    L = k_all.shape[1]
    k1, k2 = k_all[..., :DH_B], k_all[..., DH_B:]
    kidx = jnp.arange(L)
    qb = _j_qblock(T, QBLK)
    scale = DH_B ** -0.5
    def blk(j, q_b):
        qidx = n_prev + j * qb + jnp.arange(qb)
        mask = kidx[None, :] <= qidx[:, None]
        s1 = jnp.einsum('bqhd,bkhd->bhqk', q_b[..., :DH_B], k1).astype(f32) * scale
        s2 = jnp.einsum('bqhd,bkhd->bhqk', q_b[..., DH_B:], k2).astype(f32) * scale
        p = jax.nn.softmax(jnp.where(mask, s1, NEG), axis=-1) - lam * jax.nn.softmax(jnp.where(mask, s2, NEG), axis=-1)
        return jnp.einsum('bhqk,bkhe->bqhe', p.astype(v_all.dtype), v_all)
    o = _j_blocked(blk, qb, q)
    o = _j_rmsnorm(o, norm_g).astype(f32) * (1.0 - lam_init)
    return o.reshape(B, T, H_B * DV_B).astype(zq.dtype), new_kv


def _j_nsa_attend(q, rows, wrows, n_prev, n_prev_w, pos, gates, cmp_w, cmp_pe):
    B, T = q.shape[:2]
    L = rows.shape[1]
    f32 = jnp.float32
    scale = DK_C ** -0.5
    qg = q.reshape(B, T, G_C, HPG_C, DK_C)
    qr = _j_rope(q, pos).reshape(B, T, G_C, HPG_C, DK_C)
    kc, vc, ks, vs = rows[:, :, 0], rows[:, :, 1], rows[:, :, 2], rows[:, :, 3]
    n_c = L // CMP_BLK
    def compress(a, w, pe):
        blk = a[:, :n_c * CMP_BLK].reshape(B, n_c, CMP_BLK, G_C, DK_C) + pe[None, None, :, None, :]
        return jnp.einsum('bnlgd,lde->bnge', blk, w)
    k_cmp = compress(kc, cmp_w[0], cmp_pe[0])
    v_cmp = compress(vc, cmp_w[1], cmp_pe[1])
    s = jnp.einsum('btgpd,bngd->bgptn', qg, k_cmp).astype(f32) * scale
    cmask = ((jnp.arange(n_c) + 1) * CMP_BLK - 1)[None, :] <= pos[:, None]
    p_cmp = jax.nn.softmax(jnp.where(cmask, s, NEG), axis=-1) * jnp.any(cmask, axis=-1)[:, None]
    o_cmp = jnp.einsum('bgptn,bnge->btgpe', p_cmp.astype(v_cmp.dtype), v_cmp)
    R = SEL_BLK // CMP_BLK
    n_s = -(-L // SEL_BLK)
    imp = p_cmp.sum(axis=2)
    imp = jnp.pad(imp, ((0, 0), (0, 0), (0, 0), (0, n_s * R - n_c))).reshape(B, G_C, T, n_s, R).sum(-1)
    sb = jnp.arange(n_s)[None, :]
    cur = (pos // SEL_BLK)[:, None]
    valid = sb <= cur
    forced = (sb == 0) | (sb == cur) | (sb == cur - 1)
    score = jnp.where(valid, imp + FORCE_BONUS * forced.astype(f32), -1.0)
    n_top = min(N_SEL, n_s)
    top_v, top_i = lax.top_k(score, n_top)
    top_ok = jnp.transpose(top_v > -0.5, (0, 2, 1, 3))
    top_i = jnp.transpose(top_i, (0, 2, 1, 3))
    pad_s = n_s * SEL_BLK - L
    def blockify(a):
        a = jnp.pad(a, ((0, 0), (0, pad_s), (0, 0), (0, 0)))
        return jnp.transpose(a.reshape(B, n_s, SEL_BLK, G_C, DK_C), (0, 3, 1, 2, 4))
    ks_b, vs_b = blockify(ks), blockify(vs)
    gather = jax.vmap(jax.vmap(lambda a, i: a[i]))
    qs = _j_qblock(T, SEL_QBLK)
    def sel_fn(j, q_b, i_b, ok_b):
        nq = q_b.shape[1]
        idx = jnp.transpose(i_b, (0, 2, 1, 3))
        ok = jnp.transpose(ok_b, (0, 2, 1, 3))
        flat = idx.reshape(B, G_C, nq * n_top)
        kg = gather(ks_b, flat).reshape(B, G_C, nq, n_top * SEL_BLK, DK_C)
        vg = gather(vs_b, flat).reshape(B, G_C, nq, n_top * SEL_BLK, DV_C)
        kpos = (idx[..., None] * SEL_BLK + jnp.arange(SEL_BLK)).reshape(B, G_C, nq, n_top * SEL_BLK)
        tq = n_prev + j * qs + jnp.arange(nq)
        m = jnp.repeat(ok, SEL_BLK, axis=-1) & (kpos <= tq[None, None, :, None])
        sc = jnp.einsum('bqgpd,bgqkd->bgpqk', q_b, kg).astype(f32) * scale
        p = jax.nn.softmax(jnp.where(m[:, :, None], sc, NEG), axis=-1)
        return jnp.einsum('bgpqk,bgqke->bqgpe', p.astype(vg.dtype), vg)
    o_sel = _j_blocked(sel_fn, qs, qr, top_i, top_ok)
    padw = WINDOW - n_prev_w
    kwp = jnp.pad(wrows[:, :, 0], ((0, 0), (padw, 0), (0, 0), (0, 0)))
    vwp = jnp.pad(wrows[:, :, 1], ((0, 0), (padw, 0), (0, 0), (0, 0)))
    qw = _j_qblock(T, QBLK)
    def win_fn(j, q_b):
        start = j * qw
        kb = lax.dynamic_slice_in_dim(kwp, start, qw + WINDOW, axis=1)
        vb = lax.dynamic_slice_in_dim(vwp, start, qw + WINDOW, axis=1)
        pl_ = jnp.arange(qw + WINDOW)
        rel = jnp.arange(qw)[:, None] + WINDOW - pl_[None, :]
        m = (rel >= 0) & (rel <= WINDOW) & ((start + pl_) >= padw)[None, :]
        sc = jnp.einsum('bqgpd,bkgd->bgpqk', q_b, kb).astype(f32) * scale
        p = jax.nn.softmax(jnp.where(m, sc, NEG), axis=-1)
        return jnp.einsum('bgpqk,bkge->bqgpe', p.astype(vb.dtype), vb)
    o_win = _j_blocked(win_fn, qw, qr)
    g = gates.reshape(B, T, G_C, HPG_C, 3)
    o = g[..., 0:1] * o_cmp + g[..., 1:2] * o_sel + g[..., 2:3] * o_win
    return o.reshape(B, T, H_C * DV_C).astype(q.dtype)


def _j_nsa_mixer(zq, zkc, zvc, zks, zvs, zkw, zvw, zg, pos, past_rows, win_buf, cmp_w, cmp_pe):
    B, T, _ = zq.shape
    heads = lambda z: z.reshape(B, T, G_C, DK_C)
    q = zq.reshape(B, T, H_C, DK_C)
    rows = jnp.stack([heads(zkc), heads(zvc), _j_rope(heads(zks), pos), heads(zvs)], axis=2)
    wrows = jnp.stack([_j_rope(heads(zkw), pos), heads(zvw)], axis=2)
    if past_rows is None:
        all_rows, n_prev = rows, 0
    else:
        all_rows, n_prev = jnp.concatenate([past_rows, rows], axis=1), past_rows.shape[1]
    if win_buf is None:
        all_w, n_prev_w = wrows, 0
    else:
        all_w, n_prev_w = jnp.concatenate([win_buf.astype(wrows.dtype), wrows], axis=1), win_buf.shape[1]
    new_win = all_w[:, -min(WINDOW, n_prev_w + T):]
    gates = jax.nn.sigmoid(zg.reshape(B, T, H_C, 3).astype(jnp.float32))
    o = _j_nsa_attend(q, all_rows, all_w, n_prev, n_prev_w, pos, gates, cmp_w, cmp_pe)
    return o, rows, new_win


def _j_mlstm_scan(q, k, v, ig, lf, C0, n0, m0):
    B, T = q.shape[:2]
    c = MLSTM_CHUNK if T >= MLSTM_CHUNK else T
    tri = jnp.tril(jnp.ones((c, c), dtype=bool))
    def step(carry, xs):
        C, n, m = carry
        qc, kc, vc, ic, fc = xs
        b = jnp.cumsum(fc, axis=-1)
        a = ic - b
        mt = b + jnp.maximum(m[..., None], lax.cummax(a, axis=2))
        dprev = jnp.exp(b + m[..., None] - mt)
        Dm = jnp.exp(jnp.where(tri, a[:, :, None, :] + b[:, :, :, None] - mt[:, :, :, None], NEG))
        s = jnp.einsum('bhtd,bhsd->bhts', qc, kc) * Dm
        num = dprev[..., None] * jnp.einsum('bhtd,bhde->bhte', qc, C) + jnp.einsum('bhts,bhse->bhte', s, vc)
        den = dprev * jnp.einsum('bhtd,bhd->bht', qc, n) + s.sum(-1)
        h = num / jnp.maximum(jnp.abs(den), jnp.exp(-mt))[..., None]
        mL = mt[..., -1]
        w = jnp.exp(a + b[..., -1:] - mL[..., None])
        dl = jnp.exp(b[..., -1] + m - mL)
        C = dl[..., None, None] * C + jnp.einsum('bhs,bhsd,bhse->bhde', w, kc, vc)
        n = dl[..., None] * n + jnp.einsum('bhs,bhsd->bhd', w, kc)
        return (C, n, mL), h
    xs = (_j_chunked(q, c, 0.0), _j_chunked(k, c, 0.0), _j_chunked(v, c, 0.0), _j_chunked(ig, c, NEG), _j_chunked(lf, c, 0.0))
    (C, n, m), h = lax.scan(step, (C0, n0, m0), xs)
    return _j_unchunk(h, T), C, n, m


def _j_mlstm_mixer(zqk, zv, zi, zf, zo, conv_w, conv_b, gate_b, norm_g, conv_buf, C0, n0, m0):
    B, T, _ = zqk.shape
    f32 = jnp.float32
    xp = jnp.concatenate([conv_buf.astype(zqk.dtype), zqk], axis=1)
    y = conv_b
    for j in range(CONV_W):
        y = y + conv_w[j] * xp[:, j:j + T]
    qk = jax.nn.silu(y)
    q = qk[..., :H_D * DK_D].reshape(B, T, H_D, DK_D).astype(f32)
    k = qk[..., H_D * DK_D:].reshape(B, T, H_D, DK_D).astype(f32) * (DK_D ** -0.5)
    v = zv.reshape(B, T, H_D, DV_D).astype(f32)
    ig = zi.astype(f32) + gate_b[0].astype(f32)
    lf = jax.nn.log_sigmoid(zf.astype(f32) + gate_b[1].astype(f32))
    h, C, n, m = _j_mlstm_scan(q, k, v, ig, lf, C0.astype(f32), n0.astype(f32), m0.astype(f32))
    h = _j_rmsnorm(h, norm_g) * jax.nn.sigmoid(zo.reshape(B, T, H_D, DV_D).astype(f32))
    return h.reshape(B, T, H_D * DV_D).astype(zqk.dtype), xp[:, -(CONV_W - 1):], C, n, m


def _run_group(x, pos0, mem_kv, past, W):
    B, T, D = x.shape
    pos = pos0 + jnp.arange(T, dtype=jnp.int32)
    lbs = _j_hgrn_lower_bounds(W['hgrn_lb_raw'])
    new = {name: [] for name in ('diff_kv', 'hgrn', 'nsa_kv', 'nsa_win', 'm_C', 'm_n', 'm_m', 'm_conv')}
    x = x.reshape(B * T, D)
    for l in range(DEPTH):
        g = W['norm_w'][l]
        x = _ffn(x, g[0], W['ffn_w_gate'][l, 0], W['ffn_w_up'][l, 0], W['ffn_w_down'][l, 0])
        i = l // 2
        if l % 2 == 0:
            z = _normmm(x, g[1], W['w_in_even'][i]).reshape(B, T, IN_EVEN)
            zs = _j_split(z, EVEN_SPLITS)
            if past is None:
                S0 = jnp.zeros((B, H_A, DK_A, DV_A), jnp.float32)
                past_kv = None
            else:
                S0 = past['hgrn'][i]
                past_kv = _j_gather_pages(past['diff_pool'][i], past['page_table'])
            oa, S = _j_hgrn2_mixer(zs[0], zs[1], zs[2], zs[3], lbs[i], W['hgrn_norm'][i], S0)
            ob, kv_new = _j_diff_mixer(zs[4], zs[5], zs[6], pos, past_kv, W['diff_lam'][i], W['diff_norm'][i], l)
            w_out = W['w_out_even'][i]
            new['hgrn'].append(S)
            new['diff_kv'].append(kv_new)
        else:
            z = _normmm(x, g[1], W['w_in_odd'][i])[:, :IN_ODD].reshape(B, T, IN_ODD)
            zs = _j_split(z, ODD_SPLITS)
            if past is None:
                past_rows, win_buf = None, None
                conv_buf = jnp.zeros((B, CONV_W - 1, QK_CH_D), x.dtype)
                C0 = jnp.zeros((B, H_D, DK_D, DV_D), jnp.float32)
                n0 = jnp.zeros((B, H_D, DK_D), jnp.float32)
                m0 = jnp.zeros((B, H_D), jnp.float32)
            else:
                past_rows = _j_gather_pages(past['nsa_pool'][i], past['page_table'])
                win_buf = past['nsa_win'][i]
                conv_buf, C0, n0, m0 = past['m_conv'][i], past['m_C'][i], past['m_n'][i], past['m_m'][i]
            oa, rows, win_new = _j_nsa_mixer(zs[0], zs[1], zs[2], zs[3], zs[4], zs[5], zs[6], zs[7], pos,
                                             past_rows, win_buf, W['nsa_cmp_w'][i], W['nsa_cmp_pe'][i])
            ob, conv_new, C, n, m = _j_mlstm_mixer(zs[8], zs[9], zs[10], zs[11], zs[12], W['mlstm_conv_w'][i],
                                                   W['mlstm_conv_b'][i], W['mlstm_gate_b'][i], W['mlstm_norm'][i],
                                                   conv_buf, C0, n0, m0)
            w_out = W['w_out_odd'][i]
            new['nsa_kv'].append(rows)
            new['nsa_win'].append(win_new)
            new['m_C'].append(C)
            new['m_n'].append(n)
            new['m_m'].append(m)
            new['m_conv'].append(conv_new)
        x = _mmres(x, [oa.reshape(B * T, GROUP_W), ob.reshape(B * T, GROUP_W)], [w_out[:GROUP_W], w_out[GROUP_W:]])
        q = _normmm(x, g[2], W['mem_wq'][l])
        a = _xattn(q, mem_kv[l], T)
        x = _mmres(x, [a], [W['mem_wo'][l]])
        x = _ffn(x, g[3], W['ffn_w_gate'][l, 1], W['ffn_w_up'][l, 1], W['ffn_w_down'][l, 1])
    return _rmsnorm_rows(x, W['norm_final']).reshape(B, T, D), new


def kernel(x_prompt, x_sample, mem_prompt, cache_diff_kv, cache_nsa_kv, state_nsa_win, state_hgrn, state_mlstm_C, state_mlstm_n, state_mlstm_m, state_mlstm_conv, cache_mem_kv, page_table, norm_w, norm_final, ffn_w_gate, ffn_w_up, ffn_w_down, w_in_even, w_out_even, w_in_odd, w_out_odd, hgrn_lb_raw, hgrn_norm, diff_lam, diff_norm, nsa_cmp_w, nsa_cmp_pe, mlstm_conv_w, mlstm_conv_b, mlstm_gate_b, mlstm_norm, mem_wq, mem_wkv, mem_wo):
    bf = lambda w: w.astype(BF16)
    pad_odd = (-IN_ODD) % 128
    W = {'norm_w': norm_w, 'norm_final': norm_final, 'ffn_w_gate': bf(ffn_w_gate), 'ffn_w_up': bf(ffn_w_up),
         'ffn_w_down': bf(ffn_w_down), 'w_in_even': bf(w_in_even), 'w_out_even': bf(w_out_even),
         'w_in_odd': bf(jnp.pad(w_in_odd, ((0, 0), (0, 0), (0, pad_odd)))), 'w_out_odd': bf(w_out_odd),
         'hgrn_lb_raw': hgrn_lb_raw, 'hgrn_norm': hgrn_norm, 'diff_lam': diff_lam,
         'diff_norm': diff_norm, 'nsa_cmp_w': nsa_cmp_w, 'nsa_cmp_pe': nsa_cmp_pe, 'mlstm_conv_w': mlstm_conv_w,
         'mlstm_conv_b': mlstm_conv_b, 'mlstm_gate_b': mlstm_gate_b, 'mlstm_norm': mlstm_norm,
         'mem_wq': bf(mem_wq), 'mem_wo': bf(mem_wo)}
    Bp, M = mem_prompt.shape[0], mem_prompt.shape[1]
    wkv = bf(mem_wkv)
    ones = jnp.ones((D_MODEL,), F32)
    mem_flat = mem_prompt.reshape(Bp * M, D_MODEL)
    mem_kv_prompt = [_normmm(mem_flat, ones, wkv[l], norm=False).reshape(Bp, M, 2 * MEM_W) for l in range(DEPTH)]
    y_prompt, newp = _run_group(x_prompt, 0, mem_kv_prompt, None, W)
    past = {'page_table': page_table, 'diff_pool': cache_diff_kv, 'nsa_pool': cache_nsa_kv,
            'nsa_win': state_nsa_win, 'hgrn': state_hgrn, 'm_C': state_mlstm_C, 'm_n': state_mlstm_n,
            'm_m': state_mlstm_m, 'm_conv': state_mlstm_conv}
    past_len = page_table.shape[1] * PAGE_SIZE
    mem_kv_sample = [cache_mem_kv[l].reshape(cache_mem_kv.shape[1], MEM_LEN, 2 * MEM_W) for l in range(DEPTH)]
    y_sample, news = _run_group(x_sample, past_len, mem_kv_sample, past, W)
    mem_out = jnp.stack(mem_kv_prompt).reshape(DEPTH, Bp, M, 2, MEM_HEADS, MEM_DH)
    return (y_prompt, y_sample,
            jnp.stack(newp['diff_kv']), jnp.stack(news['diff_kv']),
            jnp.stack(newp['nsa_kv']), jnp.stack(news['nsa_kv']),
            jnp.stack(newp['nsa_win']), jnp.stack(news['nsa_win']),
            jnp.stack(newp['hgrn']), jnp.stack(news['hgrn']),
            jnp.stack(newp['m_C']), jnp.stack(news['m_C']),
            jnp.stack(newp['m_n']), jnp.stack(news['m_n']),
            jnp.stack(newp['m_m']), jnp.stack(news['m_m']),
            jnp.stack(newp['m_conv']), jnp.stack(news['m_conv']),
            mem_out)
```

```python
import functools
import math

import numpy as np
import jax
import jax.numpy as jnp
from jax import lax
from jax.experimental import pallas as pl
from jax.experimental.pallas import tpu as pltpu

F32 = jnp.float32
BF16 = jnp.bfloat16

D_MODEL = 1024
DEPTH = 4
PAGE_SIZE = 128
N_EVEN = (DEPTH + 1) // 2
N_ODD = DEPTH // 2
GROUP_W = D_MODEL // 2
MIX_W = 2 * GROUP_W
H_A = 4
DK_A = 128
DV_A = GROUP_W // H_A
HGRN_CHUNK = 16
F_FLOOR = 1e-30
H_B = 4
DH_B = 64
DV_B = 2 * DH_B
H_C = 4
G_C = 2
HPG_C = H_C // G_C
DK_C = 128
DV_C = GROUP_W // H_C
CMP_BLK = 32
SEL_BLK = 64
N_SEL = 16
WINDOW = 512
SEL_QBLK = 32
H_D = 4
DK_D = 128
DV_D = GROUP_W // H_D
MLSTM_CHUNK = 64
CONV_W = 4
QK_CH_D = 2 * H_D * DK_D
MEM_HEADS = 4
MEM_DH = 128
MEM_W = MEM_HEADS * MEM_DH
MEM_LEN = 256
D_FF = 2816
ROPE_THETA = 10000.0
QBLK = 128
EPS = 1e-6
NEG = -1e30
FORCE_BONUS = 1e4

EVEN_SPLITS = (H_A * DK_A, H_A * DK_A, H_A * DV_A, H_A * DV_A, H_B * DV_B, H_B * DV_B, H_B * DV_B)
ODD_SPLITS = (H_C * DK_C,) + (G_C * DK_C,) * 6 + (3 * H_C, QK_CH_D, H_D * DV_D, H_D, H_D, H_D * DV_D)
IN_EVEN = sum(EVEN_SPLITS)
IN_ODD = sum(ODD_SPLITS)

VMEM_LIMIT_BYTES = 56 * 1024 * 1024
FFN_CHUNK = D_FF // 2


def _cparams(*sem):
    return pltpu.CompilerParams(dimension_semantics=sem, vmem_limit_bytes=VMEM_LIMIT_BYTES)


def _row_tile(m):
    for t in (512, 256, 128, 64, 32, 16, 8):
        if m % t == 0:
            return t
    raise ValueError(f"row count {m} is not a multiple of 8")


def _resident(shape):
    return pl.BlockSpec(shape, lambda *_: (0,) * len(shape), pipeline_mode=pl.Buffered(1))


def _rms(x, g):
    return x * lax.rsqrt(jnp.mean(x * x, axis=-1, keepdims=True) + EPS) * g


def _ffn_kernel(x_ref, g_ref, wg_ref, wu_ref, wd_ref, o_ref):
    x = x_ref[...]
    h = _rms(x, g_ref[...]).astype(BF16)
    acc = jnp.zeros_like(x)
    for c in range(D_FF // FFN_CHUNK):
        sl = slice(c * FFN_CHUNK, (c + 1) * FFN_CHUNK)
        gate = jnp.dot(h, wg_ref[:, sl], preferred_element_type=F32)
        up = jnp.dot(h, wu_ref[:, sl], preferred_element_type=F32)
        a = (jax.nn.silu(gate) * up).astype(BF16)
        acc = acc + jnp.dot(a, wd_ref[sl, :], preferred_element_type=F32)
    o_ref[...] = x + 0.5 * acc


def _ffn(x, g, wg, wu, wd):
    m, d = x.shape
    tm = _row_tile(m)
    return pl.pallas_call(
        _ffn_kernel,
        grid=(m // tm,),
        in_specs=[pl.BlockSpec((tm, d), lambda i: (i, 0)), _resident((1, d)),
                  _resident(wg.shape), _resident(wu.shape), _resident(wd.shape)],
        out_specs=pl.BlockSpec((tm, d), lambda i: (i, 0)),
        out_shape=jax.ShapeDtypeStruct((m, d), F32),
        compiler_params=_cparams("parallel"),
        name="ffn",
    )(x, g.reshape(1, d), wg, wu, wd)


def _normmm_kernel(x_ref, g_ref, w_ref, o_ref, *, norm):
    x = x_ref[...]
    if norm:
        x = _rms(x, g_ref[...])
    o_ref[...] = jnp.dot(x.astype(BF16), w_ref[...], preferred_element_type=F32)


def _normmm(x, g, w, *, norm=True):
    m, d = x.shape
    n = w.shape[1]
    tm = _row_tile(m)
    return pl.pallas_call(
        functools.partial(_normmm_kernel, norm=norm),
        grid=(m // tm,),
        in_specs=[pl.BlockSpec((tm, d), lambda i: (i, 0)), _resident((1, d)), _resident(w.shape)],
        out_specs=pl.BlockSpec((tm, n), lambda i: (i, 0)),
        out_shape=jax.ShapeDtypeStruct((m, n), F32),
        compiler_params=_cparams("parallel"),
        name="normmm",
    )(x, g.reshape(1, d), w)


def _mmres_kernel(*refs, n_in):
    x_ref, o_ref = refs[0], refs[-1]
    acc = x_ref[...]
    for i in range(n_in):
        acc = acc + jnp.dot(refs[1 + i][...].astype(BF16), refs[1 + n_in + i][...], preferred_element_type=F32)
    o_ref[...] = acc


def _mmres(x, acts, ws):
    m, d = x.shape
    tm = _row_tile(m)
    n_in = len(acts)
    return pl.pallas_call(
        functools.partial(_mmres_kernel, n_in=n_in),
        grid=(m // tm,),
        in_specs=([pl.BlockSpec((tm, d), lambda i: (i, 0))]
                  + [pl.BlockSpec((tm, a.shape[1]), lambda i: (i, 0)) for a in acts]
                  + [_resident(w.shape) for w in ws]),
        out_specs=pl.BlockSpec((tm, d), lambda i: (i, 0)),
        out_shape=jax.ShapeDtypeStruct((m, d), F32),
        compiler_params=_cparams("parallel"),
        name="mmres",
    )(x, *acts, *ws)


def _rmsnorm_kernel(x_ref, g_ref, o_ref):
    o_ref[...] = _rms(x_ref[...], g_ref[...])


def _rmsnorm_rows(x, g):
    m, d = x.shape
    tm = _row_tile(m)
    return pl.pallas_call(
        _rmsnorm_kernel,
        grid=(m // tm,),
        in_specs=[pl.BlockSpec((tm, d), lambda i: (i, 0)), _resident((1, d))],
        out_specs=pl.BlockSpec((tm, d), lambda i: (i, 0)),
        out_shape=jax.ShapeDtypeStruct((m, d), F32),
        compiler_params=_cparams("parallel"),
        name="final_norm",
    )(x, g.reshape(1, d))


def _xattn_kernel(q_ref, kv_ref, o_ref):
    scale = MEM_DH ** -0.5
    for h in range(MEM_HEADS):
        lo, hi = h * MEM_DH, (h + 1) * MEM_DH
        q = q_ref[:, lo:hi].astype(BF16)
        k = kv_ref[0, :, lo:hi].astype(BF16)
        v = kv_ref[0, :, MEM_W + lo:MEM_W + hi].astype(BF16)
        s = lax.dot_general(q, k, (((1,), (1,)), ((), ())), preferred_element_type=F32) * scale
        p = jnp.exp(s - jnp.max(s, axis=-1, keepdims=True))
        l = jnp.sum(p, axis=-1, keepdims=True)
        o = jnp.dot(p.astype(BF16), v, preferred_element_type=F32)
        o_ref[:, lo:hi] = o / l


def _xattn(q, kv, t):
    b = kv.shape[0]
    tq = min(t, 512)
    nq = t // tq
    return pl.pallas_call(
        _xattn_kernel,
        grid=(b, nq),
        in_specs=[pl.BlockSpec((tq, MEM_W), lambda i, j: (i * nq + j, 0)),
                  pl.BlockSpec((1, MEM_LEN, 2 * MEM_W), lambda i, j: (i, 0, 0))],
        out_specs=pl.BlockSpec((tq, MEM_W), lambda i, j: (i * nq + j, 0)),
        out_shape=jax.ShapeDtypeStruct((b * t, MEM_W), F32),
        compiler_params=_cparams("parallel", "parallel"),
        name="xattn",
    )(q, kv)


def _rope_tables(pos, half, width):
    inv = ROPE_THETA ** (-jnp.arange(half, dtype=F32) / half)
    ang = pos.astype(F32)[:, None] * inv[None, :]
    cos, sin = jnp.cos(ang), jnp.sin(ang)
    reps = width // (2 * half)
    return (jnp.tile(jnp.concatenate([cos, cos], axis=-1), (1, reps)),
            jnp.tile(jnp.concatenate([-sin, sin], axis=-1), (1, reps)))


def _rot128(x, cosf, sinf):
    return x * cosf + pltpu.roll(x, 64, axis=1) * sinf


def _rot64(x, cosf, sinf):
    lane = lax.broadcasted_iota(jnp.int32, x.shape, 1)
    swapped = jnp.where((lane & 63) < 32, pltpu.roll(x, 96, axis=1), pltpu.roll(x, 32, axis=1))
    return x * cosf + swapped * sinf


def _dot_nt(a, b):
    return lax.dot_general(a, b, (((1,), (1,)), ((), ())), preferred_element_type=F32)


OZ_KV, OZ_Q, OZ_QK, OZ_V, OZ_O, OZ_SM = 0, 1536, 2048, 3072, 3584, 4096
OZ_WIDTH = 4224
NSA_TQ = 128


def _nsa_prep_kernel(z_ref, cos_ref, sin_ref, pe_ref, w_ref, rows_ref, wrows_ref, cmp_ref, x_ref, *, tm):
    cosf, sinf = cos_ref[...], sin_ref[...]
    gw = G_C * DK_C
    rows_ref[:, 0:2 * gw] = z_ref[:, 0:2 * gw]
    rows_ref[:, 3 * gw:4 * gw] = z_ref[:, 3 * gw:4 * gw]
    wrows_ref[:, gw:2 * gw] = z_ref[:, 5 * gw:6 * gw]
    for g in range(G_C):
        lo = g * DK_C
        rows_ref[:, 2 * gw + lo:2 * gw + lo + DK_C] = _rot128(z_ref[:, 2 * gw + lo:2 * gw + lo + DK_C], cosf, sinf)
        wrows_ref[:, lo:lo + DK_C] = _rot128(z_ref[:, 4 * gw + lo:4 * gw + lo + DK_C], cosf, sinf)
    nb = tm // CMP_BLK
    acc = [jnp.zeros((nb, DK_C), F32) for _ in range(2 * G_C)]
    for i in range(2 * G_C):
        x_ref[i] = z_ref[:, i * DK_C:(i + 1) * DK_C]
    for l in range(CMP_BLK):
        for i in range(2 * G_C):
            c = i // G_C
            x = x_ref[i, pl.ds(l, nb, stride=CMP_BLK), :] + pe_ref[c, l:l + 1, :]
            acc[i] += jnp.dot(x.astype(BF16), w_ref[c, l], preferred_element_type=F32)
    for i in range(2 * G_C):
        cmp_ref[:, i * DK_C:(i + 1) * DK_C] = acc[i]


def _nsa_prep(z, cosf, sinf, cmp_w, cmp_pe, t):
    m = z.shape[0]
    tm = min(512, t)
    nt = t // tm
    gw = G_C * DK_C
    return pl.pallas_call(
        functools.partial(_nsa_prep_kernel, tm=tm),
        grid=(m // tm,),
        in_specs=[pl.BlockSpec((tm, 6 * gw), lambda i: (i, OZ_KV // (6 * gw))),
                  pl.BlockSpec((tm, DK_C), lambda i: (i % nt, 0)),
                  pl.BlockSpec((tm, DK_C), lambda i: (i % nt, 0)),
                  _resident(cmp_pe.shape), _resident(cmp_w.shape)],
        out_specs=[pl.BlockSpec((tm, 4 * gw), lambda i: (i, 0)),
                   pl.BlockSpec((tm, 2 * gw), lambda i: (i, 0)),
                   pl.BlockSpec((tm // CMP_BLK, 2 * gw), lambda i: (i, 0))],
        out_shape=[jax.ShapeDtypeStruct((m, 4 * gw), F32), jax.ShapeDtypeStruct((m, 2 * gw), F32),
                   jax.ShapeDtypeStruct((m // CMP_BLK, 2 * gw), F32)],
        scratch_shapes=[pltpu.VMEM((2 * G_C, tm, DK_C), F32)],
        compiler_params=_cparams("parallel"),
        name="nsa_prep",
    )(z, cosf, sinf, cmp_pe, cmp_w.astype(BF16))


def _flash_step(s, valid, v, m_ref, l_ref, acc_ref):
    m_old = m_ref[...]
    m_new = jnp.maximum(m_old, jnp.max(jnp.where(valid, s, NEG), axis=-1, keepdims=True))
    alpha = jnp.exp(m_old - m_new)
    p = jnp.where(valid, jnp.exp(s - m_new), 0.0)
    l_ref[...] = alpha * l_ref[...] + jnp.sum(p, axis=-1, keepdims=True)
    acc_ref[...] = alpha * acc_ref[...] + jnp.dot(p.astype(BF16), v, preferred_element_type=F32)
    m_ref[...] = m_new


def _nsa_kernel(zq_ref, zsm_ref, cos_ref, sin_ref, kcmp_ref, vcmp_ref, ks_ref, vs_ref, kw_ref, vw_ref, e_ref,
                o_ref, m_ref, l_ref, acc_ref, *, tq):
    g = pl.program_id(1)
    t0 = pl.program_id(2) * tq
    scale = DK_C ** -0.5
    n_c = kcmp_ref.shape[0]
    tpos = t0 + lax.broadcasted_iota(jnp.int32, (tq, 1), 0)
    lane = lax.broadcasted_iota(jnp.int32, (tq, n_c), 1)
    cosf, sinf = cos_ref[...], sin_ref[...]
    kc = kcmp_ref[...].astype(BF16)
    vc = vcmp_ref[...].astype(BF16)
    cmask = ((lane + 1) * CMP_BLK - 1) <= tpos
    anyc = tpos >= CMP_BLK - 1
    imp = jnp.zeros((tq, n_c), F32)
    o_cmp = []
    for p in range(HPG_C):
        q = zq_ref[:, p * DK_C:(p + 1) * DK_C].astype(BF16)
        s = jnp.where(cmask, _dot_nt(q, kc) * scale, NEG)
        e = jnp.exp(s - jnp.max(s, axis=-1, keepdims=True))
        pc = jnp.where(anyc, e / jnp.sum(e, axis=-1, keepdims=True), 0.0)
        o_cmp.append(jnp.dot(pc.astype(BF16), vc, preferred_element_type=F32))
        imp = imp + pc
    imp2 = imp + pltpu.roll(imp, n_c - 1, axis=1)
    cur2 = (tpos // SEL_BLK) * 2
    valid = ((lane & 1) == 0) & (lane <= cur2)
    forced = (lane == 0) | (lane == cur2) | (lane == cur2 - 2)
    score = jnp.where(valid, imp2 + jnp.where(forced, FORCE_BONUS, 0.0), -1.0)
    sel = jnp.zeros((tq, n_c), F32)
    lanef = lane.astype(F32)
    for _ in range(N_SEL):
        top = jnp.max(score, axis=-1, keepdims=True)
        idx = jnp.min(jnp.where(score == top, lanef, float(n_c)), axis=-1, keepdims=True)
        pick = lanef == idx
        sel = jnp.where(pick, jnp.where(top > -0.5, 1.0, 0.0), sel)
        score = jnp.where(pick, -3.0, score)
    selb = sel.astype(BF16)
    qr = [_rot128(zq_ref[:, p * DK_C:(p + 1) * DK_C], cosf, sinf).astype(BF16) for p in range(HPG_C)]
    m_ref[...] = jnp.full(m_ref.shape, NEG, F32)
    l_ref[...] = jnp.zeros(l_ref.shape, F32)
    acc_ref[...] = jnp.zeros(acc_ref.shape, F32)
    kiota = lax.broadcasted_iota(jnp.int32, (1, tq), 1)

    def sel_body(jk, carry):
        off = pl.multiple_of(jk * tq, tq)
        k = ks_ref[pl.ds(off, tq), :].astype(BF16)
        v = vs_ref[pl.ds(off, tq), :].astype(BF16)
        hit = jnp.dot(selb, e_ref[jk], preferred_element_type=F32)
        ok = (hit > 0.5) & ((off + kiota) <= tpos)
        for p in range(HPG_C):
            _flash_step(_dot_nt(qr[p], k) * scale, ok, v, m_ref.at[p], l_ref.at[p], acc_ref.at[p])
        return carry

    def win_body(jk, carry):
        off = pl.multiple_of(jk * tq, tq)
        k = kw_ref[pl.ds(off, tq), :].astype(BF16)
        v = vw_ref[pl.ds(off, tq), :].astype(BF16)
        rel = tpos - (off + kiota)
        ok = (rel >= 0) & (rel <= WINDOW)
        for p in range(HPG_C):
            i = HPG_C + p
            _flash_step(_dot_nt(qr[p], k) * scale, ok, v, m_ref.at[i], l_ref.at[i], acc_ref.at[i])
        return carry

    n_k = pl.program_id(2) + 1
    lax.fori_loop(0, n_k, sel_body, 0)
    lax.fori_loop(jnp.maximum(n_k - 1 - (WINDOW + tq - 1) // tq, 0), n_k, win_body, 0)
    for p in range(HPG_C):
        def gate(c):
            a = zsm_ref[:, p * 3 + c:p * 3 + c + 1]
            b = zsm_ref[:, (HPG_C + p) * 3 + c:(HPG_C + p) * 3 + c + 1]
            return jax.nn.sigmoid(jnp.where(g == 0, a, b))
        o_sel = acc_ref[p] / l_ref[p]
        o_win = acc_ref[HPG_C + p] / l_ref[HPG_C + p]
        o_ref[:, p * DV_C:(p + 1) * DV_C] = gate(0) * o_cmp[p] + gate(1) * o_sel + gate(2) * o_win


def _nsa_prompt(z, rows, wrows, cmp, cosf, sinf, b, t):
    tq = NSA_TQ
    nq = t // tq
    n_c = t // CMP_BLK
    assert G_C == 2 and HPG_C == 2 and n_c % 128 == 0 and SEL_BLK == 2 * CMP_BLK and tq == 2 * SEL_BLK
    sb = np.arange(n_c)[None, :, None]
    key = np.arange(nq)[:, None, None] * tq + np.arange(tq)[None, None, :]
    expand = jnp.asarray((sb % 2 == 0) & (key // SEL_BLK == sb // 2), BF16)
    kv_spec = lambda c: pl.BlockSpec((t, DK_C), lambda i, g, j, c=c: (i, c + g))
    return pl.pallas_call(
        functools.partial(_nsa_kernel, tq=tq),
        grid=(b, G_C, nq),
        in_specs=[pl.BlockSpec((tq, HPG_C * DK_C), lambda i, g, j: (i * nq + j, OZ_Q // (HPG_C * DK_C) + g)),
                  pl.BlockSpec((tq, 128), lambda i, g, j: (i * nq + j, OZ_SM // 128)),
                  pl.BlockSpec((tq, DK_C), lambda i, g, j: (j, 0)),
                  pl.BlockSpec((tq, DK_C), lambda i, g, j: (j, 0)),
                  pl.BlockSpec((n_c, DK_C), lambda i, g, j: (i, g)),
                  pl.BlockSpec((n_c, DK_C), lambda i, g, j: (i, G_C + g)),
                  kv_spec(2 * G_C), kv_spec(3 * G_C), kv_spec(0), kv_spec(G_C),
                  _resident(expand.shape)],
        out_specs=pl.BlockSpec((tq, HPG_C * DV_C), lambda i, g, j: (i * nq + j, g)),
        out_shape=jax.ShapeDtypeStruct((b * t, H_C * DV_C), F32),
        scratch_shapes=[pltpu.VMEM((2 * HPG_C, tq, 1), F32), pltpu.VMEM((2 * HPG_C, tq, 1), F32),
                        pltpu.VMEM((2 * HPG_C, tq, DV_C), F32)],
        compiler_params=_cparams("parallel", "parallel", "arbitrary"),
        name="nsa_prompt",
    )(z, z, cosf, sinf, cmp, cmp, rows, rows, wrows, wrows, expand)


def _split3_terms(x):
    x1 = x.astype(BF16)
    r1 = x - x1.astype(F32)
    x2 = r1.astype(BF16)
    x3 = (r1 - x2.astype(F32)).astype(BF16)
    return x1, x2, x3


def _split3(x):
    return jnp.concatenate(_split3_terms(x), axis=-1)


def _sum3(y, w):
    return y[:, 0:w] + y[:, w:2 * w] + y[:, 2 * w:3 * w]


def _gla_tables(c):
    n_lvl = int(math.log2(c))
    assert 1 << n_lvl == c
    t = np.arange(c)[:, None]
    u = np.arange(c)[None, :]
    blocks = [u <= t, u > t]
    lvl = np.full((c, c), -1, np.int32)
    lvl[np.arange(c), np.arange(c)] = n_lvl
    for l in range(n_lvl):
        m = c >> (l + 1)
        r = (t // (2 * m)) * 2 * m + m - 1
        upper = (t % (2 * m)) >= m
        blocks.append(upper & (u > r) & (u <= t))
        blocks.append(~upper & (u > t) & (u <= r))
        same = (t // (2 * m)) == (u // (2 * m))
        lvl[same & upper & ((u % (2 * m)) < m)] = l
    sel = np.concatenate(blocks, axis=0).astype(np.float32)
    return jnp.asarray(sel, BF16), jnp.asarray(lvl), n_lvl


def _hgrn_kernel(zq_ref, zf_ref, zi_ref, zg_ref, s0_ref, lbraw_ref, ng_ref, sel_ref, lvl_ref, o_ref, s_ref, st_ref,
                 *, layer, c, n_lvl):
    j = pl.program_id(1)

    @pl.when(j == 0)
    def _():
        for h in range(H_A):
            st_ref[h] = s0_ref[0, h].T

    raw = [lbraw_ref[i:i + 1, :] for i in range(N_EVEN)]
    mx = functools.reduce(jnp.maximum, raw)
    ex = [jnp.exp(r - mx) for r in raw]
    den = functools.reduce(lambda a, b: a + b, ex)
    sm = [e / den for e in ex]
    lb_all = functools.reduce(lambda a, b: a + b, sm[:layer + 1]) - sm[0]
    lvl = lvl_ref[...]
    for h in range(H_A):
        lo, hi = h * DK_A, (h + 1) * DK_A
        lb = lb_all[:, lo:hi]
        a = zf_ref[:, lo:hi]
        q = zq_ref[:, lo:hi]
        v = zi_ref[:, lo:hi].astype(BF16)
        g = jnp.log(jnp.maximum(lb + (1.0 - lb) * jax.nn.sigmoid(a), F_FLOOR))
        k = (1.0 - lb) * jax.nn.sigmoid(-a)
        ex = jnp.exp(_sum3(jnp.dot(sel_ref[...], _split3(g), preferred_element_type=F32), DK_A))
        blk = lambda i: ex[i * c:(i + 1) * c]
        st = st_ref[h]
        o = _dot_nt((q * blk(0)).astype(BF16), st.astype(BF16))
        att = jnp.where(lvl == n_lvl, _dot_nt(q.astype(BF16), k.astype(BF16)), 0.0)
        for l in range(n_lvl):
            a_l = _dot_nt((q * blk(2 + 2 * l)).astype(BF16), (k * blk(3 + 2 * l)).astype(BF16))
            att = jnp.where(lvl == l, a_l, att)
        o = o + jnp.dot(att.astype(BF16), v, preferred_element_type=F32)
        ks = (k * blk(1)).astype(BF16)
        st_ref[h] = st * ex[c - 1:c] + lax.dot_general(v, ks, (((0,), (0,)), ((), ())), preferred_element_type=F32)
        o = o * lax.rsqrt(jnp.mean(o * o, axis=-1, keepdims=True) + EPS) * ng_ref[...]
        o_ref[:, lo:hi] = o * jax.nn.silu(zg_ref[:, lo:hi])

    @pl.when(j == pl.num_programs(1) - 1)
    def _():
        for h in range(H_A):
            s_ref[0, h] = st_ref[h].T


def _hgrn(z, s0, lb_raw, norm_g, layer, b, t):
    c = min(t, 128)
    nc = t // c
    sel, lvl, n_lvl = _gla_tables(c)
    w = H_A * DK_A
    col = lambda k: pl.BlockSpec((c, w), lambda i, j, k=k: (i * nc + j, k))
    return pl.pallas_call(
        functools.partial(_hgrn_kernel, layer=layer, c=c, n_lvl=n_lvl),
        grid=(b, nc),
        in_specs=[col(0), col(1), col(2), col(3),
                  pl.BlockSpec((1, H_A, DK_A, DV_A), lambda i, j: (i, 0, 0, 0)),
                  _resident(lb_raw.shape), _resident((1, DV_A)), _resident(sel.shape), _resident(lvl.shape)],
        out_specs=[pl.BlockSpec((c, w), lambda i, j: (i * nc + j, 0)),
                   pl.BlockSpec((1, H_A, DK_A, DV_A), lambda i, j: (i, 0, 0, 0))],
        out_shape=[jax.ShapeDtypeStruct((b * t, w), F32), jax.ShapeDtypeStruct((b, H_A, DK_A, DV_A), F32)],
        scratch_shapes=[pltpu.VMEM((H_A, DV_A, DK_A), F32)],
        compiler_params=_cparams("parallel", "arbitrary"),
        name="hgrn",
    )(z, z, z, z, s0, lb_raw, norm_g.reshape(1, DV_A), sel, lvl)


SM_GATE, SM_I, SM_F = 0, 3 * H_C, 3 * H_C + H_D
HIST = 8


def _log_sigmoid(x):
    return jnp.minimum(x, 0.0) - jnp.log1p(jnp.exp(-jnp.abs(x)))


def _mlstm_kernel(zqk_ref, zv_ref, zo_ref, sm_ref, cbuf_ref, c0_ref, n0_ref, m0_ref, cw_ref, cb_ref, gb_ref, ng_ref,
                  tri_ref, o_ref, conv_ref, c_ref, n_ref, m_ref, xh_ref, st_ref, mm_ref, *, c):
    j = pl.program_id(1)
    last = pl.num_programs(1) - 1
    lane128 = lax.broadcasted_iota(jnp.int32, (DK_D, DV_D), 1)

    @pl.when(j == 0)
    def _():
        xh_ref[HIST - (CONV_W - 1):HIST, :] = cbuf_ref[0]
        mm_ref[...] = m0_ref[0]
        for h in range(H_D):
            st_ref[h, :, 0:DV_D] = c0_ref[0, h]
            ncol = jnp.broadcast_to(n0_ref[0, h:h + 1, :], (DK_D, DK_D)).T
            st_ref[h, :, DV_D:2 * DV_D] = jnp.where(lane128 == 0, ncol, 0.0)

    xh_ref[HIST:HIST + c, :] = zqk_ref[...]
    y = cb_ref[...]
    for jj in range(CONV_W):
        y = y + cw_ref[jj:jj + 1, :] * xh_ref[HIST - (CONV_W - 1) + jj:HIST - (CONV_W - 1) + jj + c, :]
    qk = jax.nn.silu(y)
    tail = xh_ref[HIST + c - (CONV_W - 1):HIST + c, :]
    xh_ref[HIST - (CONV_W - 1):HIST, :] = tail

    pre = sm_ref[...] + gb_ref[...]
    pre_t = pre.T
    lf_c = _log_sigmoid(pre)
    lf_r = _log_sigmoid(pre_t[SM_F:SM_F + 8, :])
    tri = tri_ref[...]
    b_c = _sum3(jnp.dot(tri, _split3(lf_c), preferred_element_type=F32), 128)
    b_r = functools.reduce(lambda x, y: x + y, [_dot_nt(term, tri) for term in _split3_terms(lf_r)])
    row_t = lax.broadcasted_iota(jnp.int32, (c, c), 0)
    col_s = lax.broadcasted_iota(jnp.int32, (c, c), 1)
    causal = col_s <= row_t
    lane = lax.broadcasted_iota(jnp.int32, (1, 128), 1)
    ones_col = jnp.where(lax.broadcasted_iota(jnp.int32, (c, DV_D), 1) == 0, 1.0, 0.0).astype(BF16)
    m_new_row = mm_ref[...]
    for h in range(H_D):
        lo, hi = h * DK_D, (h + 1) * DK_D
        q = qk[:, lo:hi].astype(BF16)
        kf = qk[:, H_D * DK_D + lo:H_D * DK_D + hi] * (DK_D ** -0.5)
        v_aug = jnp.concatenate([zv_ref[:, lo:hi].astype(BF16), ones_col], axis=-1)
        m_prev = mm_ref[:, h:h + 1]
        bc = b_c[:, SM_F + h:SM_F + h + 1]
        a_c = pre[:, SM_I + h:SM_I + h + 1] - bc
        a_r = pre_t[SM_I + h:SM_I + h + 1, :] - b_r[h:h + 1, :]
        cmax = jnp.max(jnp.where(causal, a_r, -jnp.inf), axis=-1, keepdims=True)
        mt = bc + jnp.maximum(m_prev, cmax)
        dprev = jnp.exp(bc + m_prev - mt)
        dm = jnp.exp(jnp.where(causal, a_r + (bc - mt), NEG))
        s = _dot_nt(q, kf.astype(BF16)) * dm
        st = st_ref[h]
        nd = dprev * jnp.dot(q, st.astype(BF16), preferred_element_type=F32) \
            + jnp.dot(s.astype(BF16), v_aug, preferred_element_type=F32)
        den = nd[:, DV_D:DV_D + 1]
        hh = nd[:, 0:DV_D] / jnp.maximum(jnp.abs(den), jnp.exp(-mt))
        b_l = bc[c - 1:c, :]
        m_l = mt[c - 1:c, :]
        w = jnp.exp(a_c + b_l - m_l)
        dl = jnp.exp(b_l + m_prev - m_l)
        st_ref[h] = dl * st + lax.dot_general((w * kf).astype(BF16), v_aug, (((0,), (0,)), ((), ())),
                                              preferred_element_type=F32)
        m_new_row = jnp.where(lane == h, m_l, m_new_row)
        hh = hh * lax.rsqrt(jnp.mean(hh * hh, axis=-1, keepdims=True) + EPS) * ng_ref[...]
        o_ref[:, lo:hi] = hh * jax.nn.sigmoid(zo_ref[:, lo:hi])
    mm_ref[...] = m_new_row

    @pl.when(j == last)
    def _():
        conv_ref[0] = tail
        m_ref[0] = m_new_row
        for h in range(H_D):
            c_ref[0, h] = st_ref[h, :, 0:DV_D]
            n_ref[0, h:h + 1, :] = st_ref[h, :, DV_D:2 * DV_D].T[0:1, :]


def _mlstm(z, conv_buf, c0, n0, m0, conv_w, conv_b, gate_b, norm_g, b, t):
    c = min(t, 128)
    nc = t // c
    tri = jnp.asarray(np.tril(np.ones((c, c), np.float32)), BF16)
    gb = jnp.zeros((1, 128), F32).at[0, SM_I:SM_I + H_D].set(gate_b[0]).at[0, SM_F:SM_F + H_D].set(gate_b[1])
    m0p = jnp.zeros((b, 1, 128), F32).at[:, 0, :H_D].set(m0)
    w = H_D * DV_D
    blk = lambda width, k: pl.BlockSpec((c, width), lambda i, j, k=k: (i * nc + j, k))
    per_b = lambda shape: pl.BlockSpec((1,) + shape, lambda i, j: (i,) + (0,) * len(shape))
    o, conv, cc, nn, mm = pl.pallas_call(
        functools.partial(_mlstm_kernel, c=c),
        grid=(b, nc),
        in_specs=[blk(QK_CH_D, OZ_QK // QK_CH_D), blk(w, OZ_V // w), blk(w, OZ_O // w), blk(128, OZ_SM // 128),
                  per_b((CONV_W - 1, QK_CH_D)), per_b((H_D, DK_D, DV_D)), per_b((H_D, DK_D)), per_b((1, 128)),
                  _resident((CONV_W, QK_CH_D)), _resident((1, QK_CH_D)), _resident((1, 128)), _resident((1, DV_D)),
                  _resident((c, c))],
        out_specs=[pl.BlockSpec((c, w), lambda i, j: (i * nc + j, 0)),
                   per_b((CONV_W - 1, QK_CH_D)), per_b((H_D, DK_D, DV_D)), per_b((H_D, DK_D)), per_b((1, 128))],
        out_shape=[jax.ShapeDtypeStruct((b * t, w), F32), jax.ShapeDtypeStruct((b, CONV_W - 1, QK_CH_D), F32),
                   jax.ShapeDtypeStruct((b, H_D, DK_D, DV_D), F32), jax.ShapeDtypeStruct((b, H_D, DK_D), F32),
                   jax.ShapeDtypeStruct((b, 1, 128), F32)],
        scratch_shapes=[pltpu.VMEM((HIST + c, QK_CH_D), F32), pltpu.VMEM((H_D, DK_D, 2 * DV_D), F32),
                        pltpu.VMEM((1, 128), F32)],
        compiler_params=_cparams("parallel", "arbitrary"),
        name="mlstm",
    )(z, z, z, z, conv_buf, c0, n0, m0p, conv_w, conv_b.reshape(1, QK_CH_D), gb, norm_g.reshape(1, DV_D), tri)
    return o, conv, cc, nn, mm[:, 0, :H_D]


EZ_DQ, EZ_DK, EZ_DV = 4 * H_A * DK_A, 4 * H_A * DK_A + H_B * DV_B, 4 * H_A * DK_A + 2 * H_B * DV_B
DIFF_TQ = 256


def _diff_prep_kernel(zk_ref, zv_ref, cos_ref, sin_ref, kv_ref):
    cosf, sinf = cos_ref[...], sin_ref[...]
    w = H_B * DV_B
    for h in range(H_B):
        kv_ref[:, h * DV_B:(h + 1) * DV_B] = _rot64(zk_ref[:, h * DV_B:(h + 1) * DV_B], cosf, sinf)
    kv_ref[:, w:2 * w] = zv_ref[...]


def _diff_prep(z, cosf, sinf, t):
    m = z.shape[0]
    tm = min(512, t)
    nt = t // tm
    w = H_B * DV_B
    return pl.pallas_call(
        _diff_prep_kernel,
        grid=(m // tm,),
        in_specs=[pl.BlockSpec((tm, w), lambda i: (i, EZ_DK // w)), pl.BlockSpec((tm, w), lambda i: (i, EZ_DV // w)),
                  pl.BlockSpec((tm, DV_B), lambda i: (i % nt, 0)), pl.BlockSpec((tm, DV_B), lambda i: (i % nt, 0))],
        out_specs=pl.BlockSpec((tm, 2 * w), lambda i: (i, 0)),
        out_shape=jax.ShapeDtypeStruct((m, 2 * w), F32),
        compiler_params=_cparams("parallel"),
        name="diff_prep",
    )(z, z, cosf, sinf)


def _diff_lambda(lv_ref, lam_init):
    lv = lv_ref[...]
    return (jnp.exp(jnp.sum(lv[0:1] * lv[1:2], axis=-1, keepdims=True))
            - jnp.exp(jnp.sum(lv[2:3] * lv[3:4], axis=-1, keepdims=True)) + lam_init)


def _diff_queries(zq, cosf, sinf):
    q = _rot64(zq, cosf, sinf)
    lane = lax.broadcasted_iota(jnp.int32, q.shape, 1)
    return jnp.concatenate([jnp.where(lane < DH_B, q, 0.0), jnp.where(lane >= DH_B, q, 0.0)], axis=0).astype(BF16)


def _diff_finish(acc, l, lam, lam_init, ng, tq):
    o = acc[0:tq] / l[0:tq] - lam * (acc[tq:2 * tq] / l[tq:2 * tq])
    o = o * lax.rsqrt(jnp.mean(o * o, axis=-1, keepdims=True) + EPS) * ng
    return o * (1.0 - lam_init)


def _diff_kernel(zq_ref, cos_ref, sin_ref, k_ref, v_ref, lv_ref, ng_ref, o_ref, m_ref, l_ref, acc_ref,
                 *, tq, lam_init):
    t0 = pl.program_id(2) * tq
    scale = DH_B ** -0.5
    q2 = _diff_queries(zq_ref[...], cos_ref[...], sin_ref[...])
    m_ref[...] = jnp.full(m_ref.shape, NEG, F32)
    l_ref[...] = jnp.zeros(l_ref.shape, F32)
    acc_ref[...] = jnp.zeros(acc_ref.shape, F32)
    qpos = t0 + (lax.broadcasted_iota(jnp.int32, (2 * tq, 1), 0) & (tq - 1))
    kiota = lax.broadcasted_iota(jnp.int32, (1, tq), 1)

    def body(jk, carry):
        off = pl.multiple_of(jk * tq, tq)
        k = k_ref[pl.ds(off, tq), :].astype(BF16)
        v = v_ref[pl.ds(off, tq), :].astype(BF16)
        _flash_step(_dot_nt(q2, k) * scale, (off + kiota) <= qpos, v, m_ref, l_ref, acc_ref)
        return carry

    lax.fori_loop(0, pl.program_id(2) + 1, body, 0)
    lam = _diff_lambda(lv_ref, lam_init)
    o_ref[...] = _diff_finish(acc_ref[...], l_ref[...], lam, lam_init, ng_ref[...], tq)


def _diff_prompt(z, kv, cosf, sinf, lam_vecs, norm_g, layer_idx, b, t):
    tq = DIFF_TQ
    nq = t // tq
    assert tq & (tq - 1) == 0 and t % tq == 0
    lam_init = 0.8 - 0.6 * math.exp(-0.3 * layer_idx)
    return pl.pallas_call(
        functools.partial(_diff_kernel, tq=tq, lam_init=lam_init),
        grid=(b, H_B, nq),
        in_specs=[pl.BlockSpec((tq, DV_B), lambda i, h, j: (i * nq + j, EZ_DQ // DV_B + h)),
                  pl.BlockSpec((tq, DV_B), lambda i, h, j: (j, 0)),
                  pl.BlockSpec((tq, DV_B), lambda i, h, j: (j, 0)),
                  pl.BlockSpec((t, DV_B), lambda i, h, j: (i, h)),
                  pl.BlockSpec((t, DV_B), lambda i, h, j: (i, H_B + h)),
                  _resident(lam_vecs.shape), _resident((1, DV_B))],
        out_specs=pl.BlockSpec((tq, DV_B), lambda i, h, j: (i * nq + j, h)),
        out_shape=jax.ShapeDtypeStruct((b * t, H_B * DV_B), F32),
        scratch_shapes=[pltpu.VMEM((2 * tq, 1), F32), pltpu.VMEM((2 * tq, 1), F32), pltpu.VMEM((2 * tq, DV_B), F32)],
        compiler_params=_cparams("parallel", "parallel", "arbitrary"),
        name="diff_prompt",
    )(z, cosf, sinf, kv, kv, lam_vecs, norm_g.reshape(1, DV_B))


def _j_rmsnorm(x, g):
    xf = x.astype(jnp.float32)
    y = xf * lax.rsqrt(jnp.mean(xf * xf, axis=-1, keepdims=True) + EPS)
    return (y * g.astype(jnp.float32)).astype(x.dtype)


def _j_split(z, sizes):
    cuts = [int(c) for c in np.cumsum(sizes)[:-1]]
    return jnp.split(z, cuts, axis=-1)


def _j_rope(x, pos):
    half = x.shape[-1] // 2
    inv = ROPE_THETA ** (-jnp.arange(half, dtype=jnp.float32) / half)
    ang = pos.astype(jnp.float32)[:, None] * inv[None, :]
    cos = jnp.cos(ang)[None, :, None, :]
    sin = jnp.sin(ang)[None, :, None, :]
    xf = x.astype(jnp.float32)
    x1, x2 = xf[..., :half], xf[..., half:]
    return jnp.concatenate([x1 * cos - x2 * sin, x2 * cos + x1 * sin], axis=-1).astype(x.dtype)


def _j_qblock(T, pref):
    return pref if T % pref == 0 else T


def _j_blocked(fn, qb, *xs):
    B, T = xs[0].shape[:2]
    nb = T // qb
    def split(a):
        return jnp.moveaxis(a.reshape((B, nb, qb) + a.shape[2:]), 1, 0)
    out = lax.map(lambda args: fn(args[0], *args[1:]), (jnp.arange(nb),) + tuple(split(a) for a in xs))
    out = jnp.moveaxis(out, 0, 1)
    return out.reshape((B, T) + out.shape[3:])


def _j_gather_pages(pool, page_table):
    g = pool[page_table]
    return g.reshape((g.shape[0], g.shape[1] * g.shape[2]) + g.shape[3:])


def _j_chunked(a, c, fill):
    B, T = a.shape[:2]
    Tp = -(-T // c) * c
    a = jnp.pad(a, [(0, 0), (0, Tp - T)] + [(0, 0)] * (a.ndim - 2), constant_values=fill)
    a = a.reshape((B, Tp // c, c) + a.shape[2:])
    return jnp.transpose(a, (1, 0, 3, 2) + tuple(range(4, a.ndim)))


def _j_unchunk(o, T):
    nc, B, H, c, D = o.shape
    return jnp.transpose(o, (1, 0, 3, 2, 4)).reshape(B, nc * c, H, D)[:, :T]


def _j_hgrn_lower_bounds(lb_raw):
    sm = jax.nn.softmax(lb_raw.astype(jnp.float32), axis=0)
    return jnp.cumsum(sm, axis=0) - sm[0:1]


def _j_gla_scan(q, k, v, log_f, S0):
    B, T = q.shape[:2]
    c = HGRN_CHUNK if T >= HGRN_CHUNK else T
    tri = jnp.tril(jnp.ones((c, c), dtype=bool))[:, :, None]
    def step(S, xs):
        qc, kc, vc, gc = xs
        b = jnp.cumsum(gc, axis=2)
        o = jnp.einsum('bhtd,bhde->bhte', qc * jnp.exp(b), S)
        diff = b[:, :, :, None, :] - b[:, :, None, :, :]
        decay = jnp.where(tri, jnp.exp(jnp.where(tri, diff, 0.0)), 0.0)
        att = jnp.einsum('bhtd,bhtsd,bhsd->bhts', qc, decay, kc)
        o = o + jnp.einsum('bhts,bhse->bhte', att, vc)
        bl = b[:, :, -1:, :]
        S = jnp.exp(bl[:, :, 0, :])[..., None] * S + jnp.einsum('bhsd,bhse->bhde', kc * jnp.exp(bl - b), vc)
        return S, o
    xs = tuple(_j_chunked(a, c, 0.0) for a in (q, k, v, log_f))
    S, o = lax.scan(step, S0, xs)
    return _j_unchunk(o, T), S


def _j_hgrn2_mixer(zq, zf, zi, zg, lb, norm_g, S0):
    B, T, _ = zq.shape
    f32 = jnp.float32
    q = zq.reshape(B, T, H_A, DK_A).astype(f32)
    a = zf.reshape(B, T, H_A, DK_A).astype(f32)
    lbh = lb.reshape(H_A, DK_A)
    f = lbh + (1.0 - lbh) * jax.nn.sigmoid(a)
    log_f = jnp.log(jnp.maximum(f, F_FLOOR))
    k = (1.0 - lbh) * jax.nn.sigmoid(-a)
    v = zi.reshape(B, T, H_A, DV_A).astype(f32)
    o, S = _j_gla_scan(q, k, v, log_f, S0.astype(f32))
    o = _j_rmsnorm(o, norm_g) * jax.nn.silu(zg.reshape(B, T, H_A, DV_A).astype(f32))
    return o.reshape(B, T, H_A * DV_A).astype(zq.dtype), S


def _j_diff_mixer(zq, zk, zv, pos, past_kv, lam_vecs, norm_g, layer_idx):
    B, T, _ = zq.shape
    f32 = jnp.float32
    def rot(a):
        a = a.reshape(B, T, H_B, DV_B)
        return jnp.concatenate([_j_rope(a[..., :DH_B], pos), _j_rope(a[..., DH_B:], pos)], axis=-1)
    q, k = rot(zq), rot(zk)
    v = zv.reshape(B, T, H_B, DV_B)
    new_kv = jnp.stack([k, v], axis=2)
    if past_kv is None:
        k_all, v_all, n_prev = k, v, 0
    else:
        k_all = jnp.concatenate([past_kv[:, :, 0], k], axis=1)
        v_all = jnp.concatenate([past_kv[:, :, 1], v], axis=1)
        n_prev = past_kv.shape[1]
    lam_init = 0.8 - 0.6 * math.exp(-0.3 * layer_idx)
    lv = lam_vecs.astype(f32)
    lam = jnp.exp(jnp.sum(lv[0] * lv[1])) - jnp.exp(jnp.sum(lv[2] * lv[3])) + lam_init
    L = k_all.shape[1]
    k1, k2 = k_all[..., :DH_B], k_all[..., DH_B:]
    kidx = jnp.arange(L)
    qb = _j_qblock(T, QBLK)
    scale = DH_B ** -0.5
    def blk(j, q_b):
        qidx = n_prev + j * qb + jnp.arange(qb)
        mask = kidx[None, :] <= qidx[:, None]
        s1 = jnp.einsum('bqhd,bkhd->bhqk', q_b[..., :DH_B], k1).astype(f32) * scale
        s2 = jnp.einsum('bqhd,bkhd->bhqk', q_b[..., DH_B:], k2).astype(f32) * scale
        p = jax.nn.softmax(jnp.where(mask, s1, NEG), axis=-1) - lam * jax.nn.softmax(jnp.where(mask, s2, NEG), axis=-1)
        return jnp.einsum('bhqk,bkhe->bqhe', p.astype(v_all.dtype), v_all)
    o = _j_blocked(blk, qb, q)
    o = _j_rmsnorm(o, norm_g).astype(f32) * (1.0 - lam_init)
    return o.reshape(B, T, H_B * DV_B).astype(zq.dtype), new_kv


def _j_nsa_attend(q, rows, wrows, n_prev, n_prev_w, pos, gates, cmp_w, cmp_pe):
    B, T = q.shape[:2]
    L = rows.shape[1]
    f32 = jnp.float32
    scale = DK_C ** -0.5
    qg = q.reshape(B, T, G_C, HPG_C, DK_C)
    qr = _j_rope(q, pos).reshape(B, T, G_C, HPG_C, DK_C)
    kc, vc, ks, vs = rows[:, :, 0], rows[:, :, 1], rows[:, :, 2], rows[:, :, 3]
    n_c = L // CMP_BLK
    def compress(a, w, pe):
        blk = a[:, :n_c * CMP_BLK].reshape(B, n_c, CMP_BLK, G_C, DK_C) + pe[None, None, :, None, :]
        return jnp.einsum('bnlgd,lde->bnge', blk, w)
    k_cmp = compress(kc, cmp_w[0], cmp_pe[0])
    v_cmp = compress(vc, cmp_w[1], cmp_pe[1])
    s = jnp.einsum('btgpd,bngd->bgptn', qg, k_cmp).astype(f32) * scale
    cmask = ((jnp.arange(n_c) + 1) * CMP_BLK - 1)[None, :] <= pos[:, None]
    p_cmp = jax.nn.softmax(jnp.where(cmask, s, NEG), axis=-1) * jnp.any(cmask, axis=-1)[:, None]
    o_cmp = jnp.einsum('bgptn,bnge->btgpe', p_cmp.astype(v_cmp.dtype), v_cmp)
    R = SEL_BLK // CMP_BLK
    n_s = -(-L // SEL_BLK)
    imp = p_cmp.sum(axis=2)
    imp = jnp.pad(imp, ((0, 0), (0, 0), (0, 0), (0, n_s * R - n_c))).reshape(B, G_C, T, n_s, R).sum(-1)
    sb = jnp.arange(n_s)[None, :]
    cur = (pos // SEL_BLK)[:, None]
    valid = sb <= cur
    forced = (sb == 0) | (sb == cur) | (sb == cur - 1)
    score = jnp.where(valid, imp + FORCE_BONUS * forced.astype(f32), -1.0)
    n_top = min(N_SEL, n_s)
    top_v, top_i = lax.top_k(score, n_top)
    top_ok = jnp.transpose(top_v > -0.5, (0, 2, 1, 3))
    top_i = jnp.transpose(top_i, (0, 2, 1, 3))
    pad_s = n_s * SEL_BLK - L
    def blockify(a):
        a = jnp.pad(a, ((0, 0), (0, pad_s), (0, 0), (0, 0)))
        return jnp.transpose(a.reshape(B, n_s, SEL_BLK, G_C, DK_C), (0, 3, 1, 2, 4))
    ks_b, vs_b = blockify(ks), blockify(vs)
    gather = jax.vmap(jax.vmap(lambda a, i: a[i]))
    qs = _j_qblock(T, SEL_QBLK)
    def sel_fn(j, q_b, i_b, ok_b):
        nq = q_b.shape[1]
        idx = jnp.transpose(i_b, (0, 2, 1, 3))
        ok = jnp.transpose(ok_b, (0, 2, 1, 3))
        flat = idx.reshape(B, G_C, nq * n_top)
        kg = gather(ks_b, flat).reshape(B, G_C, nq, n_top * SEL_BLK, DK_C)
        vg = gather(vs_b, flat).reshape(B, G_C, nq, n_top * SEL_BLK, DV_C)
        kpos = (idx[..., None] * SEL_BLK + jnp.arange(SEL_BLK)).reshape(B, G_C, nq, n_top * SEL_BLK)
        tq = n_prev + j * qs + jnp.arange(nq)
        m = jnp.repeat(ok, SEL_BLK, axis=-1) & (kpos <= tq[None, None, :, None])
        sc = jnp.einsum('bqgpd,bgqkd->bgpqk', q_b, kg).astype(f32) * scale
        p = jax.nn.softmax(jnp.where(m[:, :, None], sc, NEG), axis=-1)
        return jnp.einsum('bgpqk,bgqke->bqgpe', p.astype(vg.dtype), vg)
    o_sel = _j_blocked(sel_fn, qs, qr, top_i, top_ok)
    padw = WINDOW - n_prev_w
    kwp = jnp.pad(wrows[:, :, 0], ((0, 0), (padw, 0), (0, 0), (0, 0)))
    vwp = jnp.pad(wrows[:, :, 1], ((0, 0), (padw, 0), (0, 0), (0, 0)))
    qw = _j_qblock(T, QBLK)
    def win_fn(j, q_b):
        start = j * qw
        kb = lax.dynamic_slice_in_dim(kwp, start, qw + WINDOW, axis=1)
        vb = lax.dynamic_slice_in_dim(vwp, start, qw + WINDOW, axis=1)
        pl_ = jnp.arange(qw + WINDOW)
        rel = jnp.arange(qw)[:, None] + WINDOW - pl_[None, :]
        m = (rel >= 0) & (rel <= WINDOW) & ((start + pl_) >= padw)[None, :]
        sc = jnp.einsum('bqgpd,bkgd->bgpqk', q_b, kb).astype(f32) * scale
        p = jax.nn.softmax(jnp.where(m, sc, NEG), axis=-1)
        return jnp.einsum('bgpqk,bkge->bqgpe', p.astype(vb.dtype), vb)
    o_win = _j_blocked(win_fn, qw, qr)
    g = gates.reshape(B, T, G_C, HPG_C, 3)
    o = g[..., 0:1] * o_cmp + g[..., 1:2] * o_sel + g[..., 2:3] * o_win
    return o.reshape(B, T, H_C * DV_C).astype(q.dtype)


def _j_nsa_mixer(zq, zkc, zvc, zks, zvs, zkw, zvw, zg, pos, past_rows, win_buf, cmp_w, cmp_pe):
    B, T, _ = zq.shape
    heads = lambda z: z.reshape(B, T, G_C, DK_C)
    q = zq.reshape(B, T, H_C, DK_C)
    rows = jnp.stack([heads(zkc), heads(zvc), _j_rope(heads(zks), pos), heads(zvs)], axis=2)
    wrows = jnp.stack([_j_rope(heads(zkw), pos), heads(zvw)], axis=2)
    if past_rows is None:
        all_rows, n_prev = rows, 0
    else:
        all_rows, n_prev = jnp.concatenate([past_rows, rows], axis=1), past_rows.shape[1]
    if win_buf is None:
        all_w, n_prev_w = wrows, 0
    else:
        all_w, n_prev_w = jnp.concatenate([win_buf.astype(wrows.dtype), wrows], axis=1), win_buf.shape[1]
    new_win = all_w[:, -min(WINDOW, n_prev_w + T):]
    gates = jax.nn.sigmoid(zg.reshape(B, T, H_C, 3).astype(jnp.float32))
    o = _j_nsa_attend(q, all_rows, all_w, n_prev, n_prev_w, pos, gates, cmp_w, cmp_pe)
    return o, rows, new_win


def _j_mlstm_scan(q, k, v, ig, lf, C0, n0, m0):
    B, T = q.shape[:2]
    c = MLSTM_CHUNK if T >= MLSTM_CHUNK else T
    tri = jnp.tril(jnp.ones((c, c), dtype=bool))
    def step(carry, xs):
        C, n, m = carry
        qc, kc, vc, ic, fc = xs
        b = jnp.cumsum(fc, axis=-1)
        a = ic - b
        mt = b + jnp.maximum(m[..., None], lax.cummax(a, axis=2))
        dprev = jnp.exp(b + m[..., None] - mt)
        Dm = jnp.exp(jnp.where(tri, a[:, :, None, :] + b[:, :, :, None] - mt[:, :, :, None], NEG))
        s = jnp.einsum('bhtd,bhsd->bhts', qc, kc) * Dm
        num = dprev[..., None] * jnp.einsum('bhtd,bhde->bhte', qc, C) + jnp.einsum('bhts,bhse->bhte', s, vc)
        den = dprev * jnp.einsum('bhtd,bhd->bht', qc, n) + s.sum(-1)
        h = num / jnp.maximum(jnp.abs(den), jnp.exp(-mt))[..., None]
        mL = mt[..., -1]
        w = jnp.exp(a + b[..., -1:] - mL[..., None])
        dl = jnp.exp(b[..., -1] + m - mL)
        C = dl[..., None, None] * C + jnp.einsum('bhs,bhsd,bhse->bhde', w, kc, vc)
        n = dl[..., None] * n + jnp.einsum('bhs,bhsd->bhd', w, kc)
        return (C, n, mL), h
    xs = (_j_chunked(q, c, 0.0), _j_chunked(k, c, 0.0), _j_chunked(v, c, 0.0), _j_chunked(ig, c, NEG), _j_chunked(lf, c, 0.0))
    (C, n, m), h = lax.scan(step, (C0, n0, m0), xs)
    return _j_unchunk(h, T), C, n, m


def _j_mlstm_mixer(zqk, zv, zi, zf, zo, conv_w, conv_b, gate_b, norm_g, conv_buf, C0, n0, m0):
    B, T, _ = zqk.shape
    f32 = jnp.float32
    xp = jnp.concatenate([conv_buf.astype(zqk.dtype), zqk], axis=1)
    y = conv_b
    for j in range(CONV_W):
        y = y + conv_w[j] * xp[:, j:j + T]
    qk = jax.nn.silu(y)
    q = qk[..., :H_D * DK_D].reshape(B, T, H_D, DK_D).astype(f32)
    k = qk[..., H_D * DK_D:].reshape(B, T, H_D, DK_D).astype(f32) * (DK_D ** -0.5)
    v = zv.reshape(B, T, H_D, DV_D).astype(f32)
    ig = zi.astype(f32) + gate_b[0].astype(f32)
    lf = jax.nn.log_sigmoid(zf.astype(f32) + gate_b[1].astype(f32))
    h, C, n, m = _j_mlstm_scan(q, k, v, ig, lf, C0.astype(f32), n0.astype(f32), m0.astype(f32))
    h = _j_rmsnorm(h, norm_g) * jax.nn.sigmoid(zo.reshape(B, T, H_D, DV_D).astype(f32))
    return h.reshape(B, T, H_D * DV_D).astype(zqk.dtype), xp[:, -(CONV_W - 1):], C, n, m


def _run_group(x, pos0, mem_kv, past, W):
    B, T, D = x.shape
    pos = pos0 + jnp.arange(T, dtype=jnp.int32)
    cos32, sin32 = _rope_tables(pos, DH_B // 2, DV_B)
    cos64, sin64 = _rope_tables(pos, DK_C // 2, DK_C)
    new = {name: [] for name in ('diff_kv', 'hgrn', 'nsa_kv', 'nsa_win', 'm_C', 'm_n', 'm_m', 'm_conv')}
    x = x.reshape(B * T, D)
    for l in range(DEPTH):
        g = W['norm_w'][l]
        x = _ffn(x, g[0], W['ffn_w_gate'][l, 0], W['ffn_w_up'][l, 0], W['ffn_w_down'][l, 0])
        i = l // 2
        if l % 2 == 0:
            z = _normmm(x, g[1], W['w_in_even'][i])
            S0 = jnp.zeros((B, H_A, DK_A, DV_A), F32) if past is None else past['hgrn'][i]
            oa, S = _hgrn(z, S0, W['hgrn_lb_raw'], W['hgrn_norm'][i], i, B, T)
            if past is None:
                kv_new = _diff_prep(z, cos32, sin32, T)
                ob = _diff_prompt(z, kv_new, cos32, sin32, W['diff_lam'][i], W['diff_norm'][i], l, B, T)
                kv_new = kv_new.reshape(B, T, 2, H_B, DV_B)
            else:
                zz = z.reshape(B, T, IN_EVEN)
                past_kv = _j_gather_pages(past['diff_pool'][i], past['page_table'])
                ob, kv_new = _j_diff_mixer(zz[..., EZ_DQ:EZ_DK], zz[..., EZ_DK:EZ_DV], zz[..., EZ_DV:], pos, past_kv,
                                           W['diff_lam'][i], W['diff_norm'][i], l)
            w_out = W['w_out_even'][i]
            new['hgrn'].append(S)
            new['diff_kv'].append(kv_new)
        else:
            z = _normmm(x, g[1], W['w_in_odd'][i])
            if past is None:
                conv_buf = jnp.zeros((B, CONV_W - 1, QK_CH_D), F32)
                C0 = jnp.zeros((B, H_D, DK_D, DV_D), F32)
                n0 = jnp.zeros((B, H_D, DK_D), F32)
                m0 = jnp.zeros((B, H_D), F32)
                rows, wrows, cmp = _nsa_prep(z, cos64, sin64, W['nsa_cmp_w'][i], W['nsa_cmp_pe'][i], T)
                oa = _nsa_prompt(z, rows, wrows, cmp, cos64, sin64, B, T)
                rows = rows.reshape(B, T, 4, G_C, DK_C)
                win_new = wrows.reshape(B, T, 2, G_C, DK_C)[:, -min(WINDOW, T):]
            else:
                conv_buf, C0, n0, m0 = past['m_conv'][i], past['m_C'][i], past['m_n'][i], past['m_m'][i]
                zz = z.reshape(B, T, OZ_WIDTH)
                gw = G_C * DK_C
                kvs = [zz[..., OZ_KV + k * gw:OZ_KV + (k + 1) * gw] for k in range(6)]
                past_rows = _j_gather_pages(past['nsa_pool'][i], past['page_table'])
                oa, rows, win_new = _j_nsa_mixer(zz[..., OZ_Q:OZ_Q + H_C * DK_C], *kvs,
                                                 zz[..., OZ_SM + SM_GATE:OZ_SM + SM_GATE + 3 * H_C], pos,
                                                 past_rows, past['nsa_win'][i], W['nsa_cmp_w'][i], W['nsa_cmp_pe'][i])
            ob, conv_new, C, n, m = _mlstm(z, conv_buf, C0, n0, m0, W['mlstm_conv_w'][i], W['mlstm_conv_b'][i],
                                           W['mlstm_gate_b'][i], W['mlstm_norm'][i], B, T)
            w_out = W['w_out_odd'][i]
            new['nsa_kv'].append(rows)
            new['nsa_win'].append(win_new)
            new['m_C'].append(C)
            new['m_n'].append(n)
            new['m_m'].append(m)
            new['m_conv'].append(conv_new)
        x = _mmres(x, [oa.reshape(B * T, GROUP_W), ob.reshape(B * T, GROUP_W)], [w_out[:GROUP_W], w_out[GROUP_W:]])
        q = _normmm(x, g[2], W['mem_wq'][l])
        a = _xattn(q, mem_kv[l], T)
        x = _mmres(x, [a], [W['mem_wo'][l]])
        x = _ffn(x, g[3], W['ffn_w_gate'][l, 1], W['ffn_w_up'][l, 1], W['ffn_w_down'][l, 1])
    return _rmsnorm_rows(x, W['norm_final']).reshape(B, T, D), new


def kernel(x_prompt, x_sample, mem_prompt, cache_diff_kv, cache_nsa_kv, state_nsa_win, state_hgrn, state_mlstm_C, state_mlstm_n, state_mlstm_m, state_mlstm_conv, cache_mem_kv, page_table, norm_w, norm_final, ffn_w_gate, ffn_w_up, ffn_w_down, w_in_even, w_out_even, w_in_odd, w_out_odd, hgrn_lb_raw, hgrn_norm, diff_lam, diff_norm, nsa_cmp_w, nsa_cmp_pe, mlstm_conv_w, mlstm_conv_b, mlstm_gate_b, mlstm_norm, mem_wq, mem_wkv, mem_wo):
    bf = lambda w: w.astype(BF16)
    o_q, o_kv, o_g, o_qk, o_v, o_i, o_f, o_o = (int(c) for c in np.cumsum((0,) + ODD_SPLITS)[[0, 1, 7, 8, 9, 10, 11, 12]])
    w_in_odd = jnp.concatenate(
        [w_in_odd[..., o_kv:o_g], w_in_odd[..., o_q:o_kv], w_in_odd[..., o_qk:o_v], w_in_odd[..., o_v:o_i],
         w_in_odd[..., o_o:], w_in_odd[..., o_g:o_qk], w_in_odd[..., o_i:o_o],
         jnp.zeros(w_in_odd.shape[:2] + (OZ_WIDTH - IN_ODD,), w_in_odd.dtype)], axis=-1)
    W = {'norm_w': norm_w, 'norm_final': norm_final, 'ffn_w_gate': bf(ffn_w_gate), 'ffn_w_up': bf(ffn_w_up),
         'ffn_w_down': bf(ffn_w_down), 'w_in_even': bf(w_in_even), 'w_out_even': bf(w_out_even),
         'w_in_odd': bf(w_in_odd), 'w_out_odd': bf(w_out_odd),
         'hgrn_lb_raw': hgrn_lb_raw, 'hgrn_norm': hgrn_norm, 'diff_lam': diff_lam,
         'diff_norm': diff_norm, 'nsa_cmp_w': nsa_cmp_w, 'nsa_cmp_pe': nsa_cmp_pe, 'mlstm_conv_w': mlstm_conv_w,
         'mlstm_conv_b': mlstm_conv_b, 'mlstm_gate_b': mlstm_gate_b, 'mlstm_norm': mlstm_norm,
         'mem_wq': bf(mem_wq), 'mem_wo': bf(mem_wo)}
    Bp, M = mem_prompt.shape[0], mem_prompt.shape[1]
    wkv = bf(mem_wkv)
    ones = jnp.ones((D_MODEL,), F32)
    mem_flat = mem_prompt.reshape(Bp * M, D_MODEL)
    mem_kv_prompt = [_normmm(mem_flat, ones, wkv[l], norm=False).reshape(Bp, M, 2 * MEM_W) for l in range(DEPTH)]
    y_prompt, newp = _run_group(x_prompt, 0, mem_kv_prompt, None, W)
    past = {'page_table': page_table, 'diff_pool': cache_diff_kv, 'nsa_pool': cache_nsa_kv,
            'nsa_win': state_nsa_win, 'hgrn': state_hgrn, 'm_C': state_mlstm_C, 'm_n': state_mlstm_n,
            'm_m': state_mlstm_m, 'm_conv': state_mlstm_conv}
    past_len = page_table.shape[1] * PAGE_SIZE
    mem_kv_sample = [cache_mem_kv[l].reshape(cache_mem_kv.shape[1], MEM_LEN, 2 * MEM_W) for l in range(DEPTH)]
    y_sample, news = _run_group(x_sample, past_len, mem_kv_sample, past, W)
    mem_out = jnp.stack(mem_kv_prompt).reshape(DEPTH, Bp, M, 2, MEM_HEADS, MEM_DH)
    return (y_prompt, y_sample,
            jnp.stack(newp['diff_kv']), jnp.stack(news['diff_kv']),
            jnp.stack(newp['nsa_kv']), jnp.stack(news['nsa_kv']),
            jnp.stack(newp['nsa_win']), jnp.stack(news['nsa_win']),
            jnp.stack(newp['hgrn']), jnp.stack(news['hgrn']),
            jnp.stack(newp['m_C']), jnp.stack(news['m_C']),
            jnp.stack(newp['m_n']), jnp.stack(news['m_n']),
            jnp.stack(newp['m_m']), jnp.stack(news['m_m']),
            jnp.stack(newp['m_conv']), jnp.stack(news['m_conv']),
            mem_out)
```

```python
import functools
import math

import numpy as np
import jax
import jax.numpy as jnp
from jax import lax
from jax.experimental import pallas as pl
from jax.experimental.pallas import tpu as pltpu

F32 = jnp.float32
BF16 = jnp.bfloat16

D_MODEL = 1024
DEPTH = 4
PAGE_SIZE = 128
N_EVEN = (DEPTH + 1) // 2
N_ODD = DEPTH // 2
GROUP_W = D_MODEL // 2
MIX_W = 2 * GROUP_W
H_A = 4
DK_A = 128
DV_A = GROUP_W // H_A
HGRN_CHUNK = 16
F_FLOOR = 1e-30
H_B = 4
DH_B = 64
DV_B = 2 * DH_B
H_C = 4
G_C = 2
HPG_C = H_C // G_C
DK_C = 128
DV_C = GROUP_W // H_C
CMP_BLK = 32
SEL_BLK = 64
N_SEL = 16
WINDOW = 512
SEL_QBLK = 32
H_D = 4
DK_D = 128
DV_D = GROUP_W // H_D
MLSTM_CHUNK = 64
CONV_W = 4
QK_CH_D = 2 * H_D * DK_D
MEM_HEADS = 4
MEM_DH = 128
MEM_W = MEM_HEADS * MEM_DH
MEM_LEN = 256
D_FF = 2816
ROPE_THETA = 10000.0
QBLK = 128
EPS = 1e-6
NEG = -1e30
FORCE_BONUS = 1e4

EVEN_SPLITS = (H_A * DK_A, H_A * DK_A, H_A * DV_A, H_A * DV_A, H_B * DV_B, H_B * DV_B, H_B * DV_B)
ODD_SPLITS = (H_C * DK_C,) + (G_C * DK_C,) * 6 + (3 * H_C, QK_CH_D, H_D * DV_D, H_D, H_D, H_D * DV_D)
IN_EVEN = sum(EVEN_SPLITS)
IN_ODD = sum(ODD_SPLITS)

VMEM_LIMIT_BYTES = 56 * 1024 * 1024
FFN_CHUNK = D_FF // 2


def _cparams(*sem):
    return pltpu.CompilerParams(dimension_semantics=sem, vmem_limit_bytes=VMEM_LIMIT_BYTES)


def _row_tile(m):
    for t in (512, 256, 128, 64, 32, 16, 8):
        if m % t == 0:
            return t
    raise ValueError(f"row count {m} is not a multiple of 8")


def _resident(shape):
    return pl.BlockSpec(shape, lambda *_: (0,) * len(shape), pipeline_mode=pl.Buffered(1))


def _rms(x, g):
    return x * lax.rsqrt(jnp.mean(x * x, axis=-1, keepdims=True) + EPS) * g


def _ffn_kernel(x_ref, g_ref, wg_ref, wu_ref, wd_ref, o_ref):
    x = x_ref[...]
    h = _rms(x, g_ref[...]).astype(BF16)
    acc = jnp.zeros_like(x)
    for c in range(D_FF // FFN_CHUNK):
        sl = slice(c * FFN_CHUNK, (c + 1) * FFN_CHUNK)
        gate = jnp.dot(h, wg_ref[:, sl], preferred_element_type=F32)
        up = jnp.dot(h, wu_ref[:, sl], preferred_element_type=F32)
        a = (jax.nn.silu(gate) * up).astype(BF16)
        acc = acc + jnp.dot(a, wd_ref[sl, :], preferred_element_type=F32)
    o_ref[...] = x + 0.5 * acc


def _ffn(x, g, wg, wu, wd):
    m, d = x.shape
    tm = _row_tile(m)
    return pl.pallas_call(
        _ffn_kernel,
        grid=(m // tm,),
        in_specs=[pl.BlockSpec((tm, d), lambda i: (i, 0)), _resident((1, d)),
                  _resident(wg.shape), _resident(wu.shape), _resident(wd.shape)],
        out_specs=pl.BlockSpec((tm, d), lambda i: (i, 0)),
        out_shape=jax.ShapeDtypeStruct((m, d), F32),
        compiler_params=_cparams("parallel"),
        name="ffn",
    )(x, g.reshape(1, d), wg, wu, wd)


def _normmm_kernel(x_ref, g_ref, w_ref, o_ref, *, norm):
    x = x_ref[...]
    if norm:
        x = _rms(x, g_ref[...])
    o_ref[...] = jnp.dot(x.astype(BF16), w_ref[...], preferred_element_type=F32)


def _normmm(x, g, w, *, norm=True):
    m, d = x.shape
    n = w.shape[1]
    tm = _row_tile(m)
    return pl.pallas_call(
        functools.partial(_normmm_kernel, norm=norm),
        grid=(m // tm,),
        in_specs=[pl.BlockSpec((tm, d), lambda i: (i, 0)), _resident((1, d)), _resident(w.shape)],
        out_specs=pl.BlockSpec((tm, n), lambda i: (i, 0)),
        out_shape=jax.ShapeDtypeStruct((m, n), F32),
        compiler_params=_cparams("parallel"),
        name="normmm",
    )(x, g.reshape(1, d), w)


def _mmres_kernel(*refs, n_in):
    x_ref, o_ref = refs[0], refs[-1]
    acc = x_ref[...]
    for i in range(n_in):
        acc = acc + jnp.dot(refs[1 + i][...].astype(BF16), refs[1 + n_in + i][...], preferred_element_type=F32)
    o_ref[...] = acc


def _mmres(x, acts, ws):
    m, d = x.shape
    tm = _row_tile(m)
    n_in = len(acts)
    return pl.pallas_call(
        functools.partial(_mmres_kernel, n_in=n_in),
        grid=(m // tm,),
        in_specs=([pl.BlockSpec((tm, d), lambda i: (i, 0))]
                  + [pl.BlockSpec((tm, a.shape[1]), lambda i: (i, 0)) for a in acts]
                  + [_resident(w.shape) for w in ws]),
        out_specs=pl.BlockSpec((tm, d), lambda i: (i, 0)),
        out_shape=jax.ShapeDtypeStruct((m, d), F32),
        compiler_params=_cparams("parallel"),
        name="mmres",
    )(x, *acts, *ws)


def _rmsnorm_kernel(x_ref, g_ref, o_ref):
    o_ref[...] = _rms(x_ref[...], g_ref[...])


def _rmsnorm_rows(x, g):
    m, d = x.shape
    tm = _row_tile(m)
    return pl.pallas_call(
        _rmsnorm_kernel,
        grid=(m // tm,),
        in_specs=[pl.BlockSpec((tm, d), lambda i: (i, 0)), _resident((1, d))],
        out_specs=pl.BlockSpec((tm, d), lambda i: (i, 0)),
        out_shape=jax.ShapeDtypeStruct((m, d), F32),
        compiler_params=_cparams("parallel"),
        name="final_norm",
    )(x, g.reshape(1, d))


def _xattn_kernel(q_ref, kv_ref, o_ref):
    scale = MEM_DH ** -0.5
    for h in range(MEM_HEADS):
        lo, hi = h * MEM_DH, (h + 1) * MEM_DH
        q = q_ref[:, lo:hi].astype(BF16)
        k = kv_ref[0, :, lo:hi].astype(BF16)
        v = kv_ref[0, :, MEM_W + lo:MEM_W + hi].astype(BF16)
        s = lax.dot_general(q, k, (((1,), (1,)), ((), ())), preferred_element_type=F32) * scale
        p = jnp.exp(s - jnp.max(s, axis=-1, keepdims=True))
        l = jnp.sum(p, axis=-1, keepdims=True)
        o = jnp.dot(p.astype(BF16), v, preferred_element_type=F32)
        o_ref[:, lo:hi] = o / l


def _xattn(q, kv, t):
    b = kv.shape[0]
    tq = min(t, 512)
    nq = t // tq
    return pl.pallas_call(
        _xattn_kernel,
        grid=(b, nq),
        in_specs=[pl.BlockSpec((tq, MEM_W), lambda i, j: (i * nq + j, 0)),
                  pl.BlockSpec((1, MEM_LEN, 2 * MEM_W), lambda i, j: (i, 0, 0))],
        out_specs=pl.BlockSpec((tq, MEM_W), lambda i, j: (i * nq + j, 0)),
        out_shape=jax.ShapeDtypeStruct((b * t, MEM_W), F32),
        compiler_params=_cparams("parallel", "parallel"),
        name="xattn",
    )(q, kv)


def _rope_tables(pos, half, width):
    inv = ROPE_THETA ** (-jnp.arange(half, dtype=F32) / half)
    ang = pos.astype(F32)[:, None] * inv[None, :]
    cos, sin = jnp.cos(ang), jnp.sin(ang)
    reps = width // (2 * half)
    return (jnp.tile(jnp.concatenate([cos, cos], axis=-1), (1, reps)),
            jnp.tile(jnp.concatenate([-sin, sin], axis=-1), (1, reps)))


def _rot128(x, cosf, sinf):
    return x * cosf + pltpu.roll(x, 64, axis=1) * sinf


def _rot64(x, cosf, sinf):
    lane = lax.broadcasted_iota(jnp.int32, x.shape, 1)
    swapped = jnp.where((lane & 63) < 32, pltpu.roll(x, 96, axis=1), pltpu.roll(x, 32, axis=1))
    return x * cosf + swapped * sinf


def _dot_nt(a, b):
    return lax.dot_general(a, b, (((1,), (1,)), ((), ())), preferred_element_type=F32)


OZ_KV, OZ_Q, OZ_QK, OZ_V, OZ_O, OZ_SM = 0, 1536, 2048, 3072, 3584, 4096
OZ_WIDTH = 4224
NSA_TQ = 128


def _compress_blocks(x_ref, pe_ref, w_ref, cmp_ref, nb):
    acc = [jnp.zeros((nb, DK_C), F32) for _ in range(2 * G_C)]
    for l in range(CMP_BLK):
        for i in range(2 * G_C):
            c = i // G_C
            x = x_ref[i, pl.ds(l, nb, stride=CMP_BLK), :] + pe_ref[c, l:l + 1, :]
            acc[i] += jnp.dot(x.astype(BF16), w_ref[c, l], preferred_element_type=F32)
    for i in range(2 * G_C):
        cmp_ref[:, i * DK_C:(i + 1) * DK_C] = acc[i]


def _nsa_prep_kernel(z_ref, cos_ref, sin_ref, pe_ref, w_ref, rows_ref, wrows_ref, cmp_ref, x_ref, *, tm):
    cosf, sinf = cos_ref[...], sin_ref[...]
    gw = G_C * DK_C
    rows_ref[:, 0:2 * gw] = z_ref[:, 0:2 * gw]
    rows_ref[:, 3 * gw:4 * gw] = z_ref[:, 3 * gw:4 * gw]
    wrows_ref[:, gw:2 * gw] = z_ref[:, 5 * gw:6 * gw]
    for g in range(G_C):
        lo = g * DK_C
        rows_ref[:, 2 * gw + lo:2 * gw + lo + DK_C] = _rot128(z_ref[:, 2 * gw + lo:2 * gw + lo + DK_C], cosf, sinf)
        wrows_ref[:, lo:lo + DK_C] = _rot128(z_ref[:, 4 * gw + lo:4 * gw + lo + DK_C], cosf, sinf)
    for i in range(2 * G_C):
        x_ref[i] = z_ref[:, i * DK_C:(i + 1) * DK_C]
    _compress_blocks(x_ref, pe_ref, w_ref, cmp_ref, tm // CMP_BLK)


def _nsa_prep(z, cosf, sinf, cmp_w, cmp_pe, t):
    m = z.shape[0]
    tm = min(512, t)
    nt = t // tm
    gw = G_C * DK_C
    return pl.pallas_call(
        functools.partial(_nsa_prep_kernel, tm=tm),
        grid=(m // tm,),
        in_specs=[pl.BlockSpec((tm, 6 * gw), lambda i: (i, OZ_KV // (6 * gw))),
                  pl.BlockSpec((tm, DK_C), lambda i: (i % nt, 0)),
                  pl.BlockSpec((tm, DK_C), lambda i: (i % nt, 0)),
                  _resident(cmp_pe.shape), _resident(cmp_w.shape)],
        out_specs=[pl.BlockSpec((tm, 4 * gw), lambda i: (i, 0)),
                   pl.BlockSpec((tm, 2 * gw), lambda i: (i, 0)),
                   pl.BlockSpec((tm // CMP_BLK, 2 * gw), lambda i: (i, 0))],
        out_shape=[jax.ShapeDtypeStruct((m, 4 * gw), F32), jax.ShapeDtypeStruct((m, 2 * gw), F32),
                   jax.ShapeDtypeStruct((m // CMP_BLK, 2 * gw), F32)],
        scratch_shapes=[pltpu.VMEM((2 * G_C, tm, DK_C), F32)],
        compiler_params=_cparams("parallel"),
        name="nsa_prep",
    )(z, cosf, sinf, cmp_pe, cmp_w.astype(BF16))


def _flash_step(s, valid, v, m_ref, l_ref, acc_ref):
    m_old = m_ref[...]
    m_new = jnp.maximum(m_old, jnp.max(jnp.where(valid, s, NEG), axis=-1, keepdims=True))
    alpha = jnp.exp(m_old - m_new)
    p = jnp.where(valid, jnp.exp(s - m_new), 0.0)
    l_ref[...] = alpha * l_ref[...] + jnp.sum(p, axis=-1, keepdims=True)
    acc_ref[...] = alpha * acc_ref[...] + jnp.dot(p.astype(BF16), v, preferred_element_type=F32)
    m_ref[...] = m_new


def _nsa_kernel(zq_ref, zsm_ref, cos_ref, sin_ref, kcmp_ref, vcmp_ref, ks_ref, vs_ref, kw_ref, vw_ref, e_ref,
                o_ref, m_ref, l_ref, acc_ref, *, tq):
    g = pl.program_id(1)
    t0 = pl.program_id(2) * tq
    scale = DK_C ** -0.5
    n_c = kcmp_ref.shape[0]
    tpos = t0 + lax.broadcasted_iota(jnp.int32, (tq, 1), 0)
    lane = lax.broadcasted_iota(jnp.int32, (tq, n_c), 1)
    cosf, sinf = cos_ref[...], sin_ref[...]
    kc = kcmp_ref[...].astype(BF16)
    vc = vcmp_ref[...].astype(BF16)
    cmask = ((lane + 1) * CMP_BLK - 1) <= tpos
    anyc = tpos >= CMP_BLK - 1
    imp = jnp.zeros((tq, n_c), F32)
    o_cmp = []
    for p in range(HPG_C):
        q = zq_ref[:, p * DK_C:(p + 1) * DK_C].astype(BF16)
        s = jnp.where(cmask, _dot_nt(q, kc) * scale, NEG)
        e = jnp.exp(s - jnp.max(s, axis=-1, keepdims=True))
        pc = jnp.where(anyc, e / jnp.sum(e, axis=-1, keepdims=True), 0.0)
        o_cmp.append(jnp.dot(pc.astype(BF16), vc, preferred_element_type=F32))
        imp = imp + pc
    imp2 = imp + pltpu.roll(imp, n_c - 1, axis=1)
    cur2 = (tpos // SEL_BLK) * 2
    valid = ((lane & 1) == 0) & (lane <= cur2)
    forced = (lane == 0) | (lane == cur2) | (lane == cur2 - 2)
    score = jnp.where(valid, imp2 + jnp.where(forced, FORCE_BONUS, 0.0), -1.0)
    sel = jnp.zeros((tq, n_c), F32)
    lanef = lane.astype(F32)
    for _ in range(N_SEL):
        top = jnp.max(score, axis=-1, keepdims=True)
        idx = jnp.min(jnp.where(score == top, lanef, float(n_c)), axis=-1, keepdims=True)
        pick = lanef == idx
        sel = jnp.where(pick, jnp.where(top > -0.5, 1.0, 0.0), sel)
        score = jnp.where(pick, -3.0, score)
    selb = sel.astype(BF16)
    qr = [_rot128(zq_ref[:, p * DK_C:(p + 1) * DK_C], cosf, sinf).astype(BF16) for p in range(HPG_C)]
    m_ref[...] = jnp.full(m_ref.shape, NEG, F32)
    l_ref[...] = jnp.zeros(l_ref.shape, F32)
    acc_ref[...] = jnp.zeros(acc_ref.shape, F32)
    kiota = lax.broadcasted_iota(jnp.int32, (1, tq), 1)

    def sel_body(jk, carry):
        off = pl.multiple_of(jk * tq, tq)
        k = ks_ref[pl.ds(off, tq), :].astype(BF16)
        v = vs_ref[pl.ds(off, tq), :].astype(BF16)
        hit = jnp.dot(selb, e_ref[jk], preferred_element_type=F32)
        ok = (hit > 0.5) & ((off + kiota) <= tpos)
        for p in range(HPG_C):
            _flash_step(_dot_nt(qr[p], k) * scale, ok, v, m_ref.at[p], l_ref.at[p], acc_ref.at[p])
        return carry

    def win_body(jk, carry):
        off = pl.multiple_of(jk * tq, tq)
        k = kw_ref[pl.ds(off, tq), :].astype(BF16)
        v = vw_ref[pl.ds(off, tq), :].astype(BF16)
        rel = tpos - (off + kiota)
        ok = (rel >= 0) & (rel <= WINDOW)
        for p in range(HPG_C):
            i = HPG_C + p
            _flash_step(_dot_nt(qr[p], k) * scale, ok, v, m_ref.at[i], l_ref.at[i], acc_ref.at[i])
        return carry

    n_k = pl.program_id(2) + 1
    lax.fori_loop(0, n_k, sel_body, 0)
    lax.fori_loop(jnp.maximum(n_k - 1 - (WINDOW + tq - 1) // tq, 0), n_k, win_body, 0)
    for p in range(HPG_C):
        def gate(c):
            a = zsm_ref[:, p * 3 + c:p * 3 + c + 1]
            b = zsm_ref[:, (HPG_C + p) * 3 + c:(HPG_C + p) * 3 + c + 1]
            return jax.nn.sigmoid(jnp.where(g == 0, a, b))
        o_sel = acc_ref[p] / l_ref[p]
        o_win = acc_ref[HPG_C + p] / l_ref[HPG_C + p]
        o_ref[:, p * DV_C:(p + 1) * DV_C] = gate(0) * o_cmp[p] + gate(1) * o_sel + gate(2) * o_win


def _nsa_prompt(z, rows, wrows, cmp, cosf, sinf, b, t):
    tq = NSA_TQ
    nq = t // tq
    n_c = t // CMP_BLK
    assert G_C == 2 and HPG_C == 2 and n_c % 128 == 0 and SEL_BLK == 2 * CMP_BLK and tq == 2 * SEL_BLK
    sb = np.arange(n_c)[None, :, None]
    key = np.arange(nq)[:, None, None] * tq + np.arange(tq)[None, None, :]
    expand = jnp.asarray((sb % 2 == 0) & (key // SEL_BLK == sb // 2), BF16)
    kv_spec = lambda c: pl.BlockSpec((t, DK_C), lambda i, g, j, c=c: (i, c + g))
    return pl.pallas_call(
        functools.partial(_nsa_kernel, tq=tq),
        grid=(b, G_C, nq),
        in_specs=[pl.BlockSpec((tq, HPG_C * DK_C), lambda i, g, j: (i * nq + j, OZ_Q // (HPG_C * DK_C) + g)),
                  pl.BlockSpec((tq, 128), lambda i, g, j: (i * nq + j, OZ_SM // 128)),
                  pl.BlockSpec((tq, DK_C), lambda i, g, j: (j, 0)),
                  pl.BlockSpec((tq, DK_C), lambda i, g, j: (j, 0)),
                  pl.BlockSpec((n_c, DK_C), lambda i, g, j: (i, g)),
                  pl.BlockSpec((n_c, DK_C), lambda i, g, j: (i, G_C + g)),
                  kv_spec(2 * G_C), kv_spec(3 * G_C), kv_spec(0), kv_spec(G_C),
                  _resident(expand.shape)],
        out_specs=pl.BlockSpec((tq, HPG_C * DV_C), lambda i, g, j: (i * nq + j, g)),
        out_shape=jax.ShapeDtypeStruct((b * t, H_C * DV_C), F32),
        scratch_shapes=[pltpu.VMEM((2 * HPG_C, tq, 1), F32), pltpu.VMEM((2 * HPG_C, tq, 1), F32),
                        pltpu.VMEM((2 * HPG_C, tq, DV_C), F32)],
        compiler_params=_cparams("parallel", "parallel", "arbitrary"),
        name="nsa_prompt",
    )(z, z, cosf, sinf, cmp, cmp, rows, rows, wrows, wrows, expand)


def _split3_terms(x):
    x1 = x.astype(BF16)
    r1 = x - x1.astype(F32)
    x2 = r1.astype(BF16)
    x3 = (r1 - x2.astype(F32)).astype(BF16)
    return x1, x2, x3


def _split3(x):
    return jnp.concatenate(_split3_terms(x), axis=-1)


def _sum3(y, w):
    return y[:, 0:w] + y[:, w:2 * w] + y[:, 2 * w:3 * w]


def _gla_tables(c):
    n_lvl = int(math.log2(c))
    assert 1 << n_lvl == c
    t = np.arange(c)[:, None]
    u = np.arange(c)[None, :]
    blocks = [u <= t, u > t]
    lvl = np.full((c, c), -1, np.int32)
    lvl[np.arange(c), np.arange(c)] = n_lvl
    for l in range(n_lvl):
        m = c >> (l + 1)
        r = (t // (2 * m)) * 2 * m + m - 1
        upper = (t % (2 * m)) >= m
        blocks.append(upper & (u > r) & (u <= t))
        blocks.append(~upper & (u > t) & (u <= r))
        same = (t // (2 * m)) == (u // (2 * m))
        lvl[same & upper & ((u % (2 * m)) < m)] = l
    sel = np.concatenate(blocks, axis=0).astype(np.float32)
    return jnp.asarray(sel, BF16), jnp.asarray(lvl), n_lvl


def _hgrn_kernel(zq_ref, zf_ref, zi_ref, zg_ref, s0_ref, lbraw_ref, ng_ref, sel_ref, lvl_ref, o_ref, s_ref, st_ref,
                 *, layer, c, n_lvl):
    j = pl.program_id(1)

    @pl.when(j == 0)
    def _():
        for h in range(H_A):
            st_ref[h] = s0_ref[0, h].T

    raw = [lbraw_ref[i:i + 1, :] for i in range(N_EVEN)]
    mx = functools.reduce(jnp.maximum, raw)
    ex = [jnp.exp(r - mx) for r in raw]
    den = functools.reduce(lambda a, b: a + b, ex)
    sm = [e / den for e in ex]
    lb_all = functools.reduce(lambda a, b: a + b, sm[:layer + 1]) - sm[0]
    lvl = lvl_ref[...]
    for h in range(H_A):
        lo, hi = h * DK_A, (h + 1) * DK_A
        lb = lb_all[:, lo:hi]
        a = zf_ref[:, lo:hi]
        q = zq_ref[:, lo:hi]
        v = zi_ref[:, lo:hi].astype(BF16)
        g = jnp.log(jnp.maximum(lb + (1.0 - lb) * jax.nn.sigmoid(a), F_FLOOR))
        k = (1.0 - lb) * jax.nn.sigmoid(-a)
        ex = jnp.exp(_sum3(jnp.dot(sel_ref[...], _split3(g), preferred_element_type=F32), DK_A))
        blk = lambda i: ex[i * c:(i + 1) * c]
        st = st_ref[h]
        o = _dot_nt((q * blk(0)).astype(BF16), st.astype(BF16))
        att = jnp.where(lvl == n_lvl, _dot_nt(q.astype(BF16), k.astype(BF16)), 0.0)
        for l in range(n_lvl):
            a_l = _dot_nt((q * blk(2 + 2 * l)).astype(BF16), (k * blk(3 + 2 * l)).astype(BF16))
            att = jnp.where(lvl == l, a_l, att)
        o = o + jnp.dot(att.astype(BF16), v, preferred_element_type=F32)
        ks = (k * blk(1)).astype(BF16)
        st_ref[h] = st * ex[c - 1:c] + lax.dot_general(v, ks, (((0,), (0,)), ((), ())), preferred_element_type=F32)
        o = o * lax.rsqrt(jnp.mean(o * o, axis=-1, keepdims=True) + EPS) * ng_ref[...]
        o_ref[:, lo:hi] = o * jax.nn.silu(zg_ref[:, lo:hi])

    @pl.when(j == pl.num_programs(1) - 1)
    def _():
        for h in range(H_A):
            s_ref[0, h] = st_ref[h].T


def _hgrn(z, s0, lb_raw, norm_g, layer, b, t):
    c = min(t, 128)
    nc = t // c
    sel, lvl, n_lvl = _gla_tables(c)
    w = H_A * DK_A
    col = lambda k: pl.BlockSpec((c, w), lambda i, j, k=k: (i * nc + j, k))
    return pl.pallas_call(
        functools.partial(_hgrn_kernel, layer=layer, c=c, n_lvl=n_lvl),
        grid=(b, nc),
        in_specs=[col(0), col(1), col(2), col(3),
                  pl.BlockSpec((1, H_A, DK_A, DV_A), lambda i, j: (i, 0, 0, 0)),
                  _resident(lb_raw.shape), _resident((1, DV_A)), _resident(sel.shape), _resident(lvl.shape)],
        out_specs=[pl.BlockSpec((c, w), lambda i, j: (i * nc + j, 0)),
                   pl.BlockSpec((1, H_A, DK_A, DV_A), lambda i, j: (i, 0, 0, 0))],
        out_shape=[jax.ShapeDtypeStruct((b * t, w), F32), jax.ShapeDtypeStruct((b, H_A, DK_A, DV_A), F32)],
        scratch_shapes=[pltpu.VMEM((H_A, DV_A, DK_A), F32)],
        compiler_params=_cparams("parallel", "arbitrary"),
        name="hgrn",
    )(z, z, z, z, s0, lb_raw, norm_g.reshape(1, DV_A), sel, lvl)


SM_GATE, SM_I, SM_F = 0, 3 * H_C, 3 * H_C + H_D
HIST = 8


def _log_sigmoid(x):
    return jnp.minimum(x, 0.0) - jnp.log1p(jnp.exp(-jnp.abs(x)))


def _mlstm_kernel(zqk_ref, zv_ref, zo_ref, sm_ref, cbuf_ref, c0_ref, n0_ref, m0_ref, cw_ref, cb_ref, gb_ref, ng_ref,
                  tri_ref, o_ref, conv_ref, c_ref, n_ref, m_ref, xh_ref, st_ref, mm_ref, *, c):
    j = pl.program_id(1)
    last = pl.num_programs(1) - 1
    lane128 = lax.broadcasted_iota(jnp.int32, (DK_D, DV_D), 1)

    @pl.when(j == 0)
    def _():
        xh_ref[HIST - (CONV_W - 1):HIST, :] = cbuf_ref[0]
        mm_ref[...] = m0_ref[0]
        for h in range(H_D):
            st_ref[h, :, 0:DV_D] = c0_ref[0, h]
            ncol = jnp.broadcast_to(n0_ref[0, h:h + 1, :], (DK_D, DK_D)).T
            st_ref[h, :, DV_D:2 * DV_D] = jnp.where(lane128 == 0, ncol, 0.0)

    xh_ref[HIST:HIST + c, :] = zqk_ref[...]
    y = cb_ref[...]
    for jj in range(CONV_W):
        y = y + cw_ref[jj:jj + 1, :] * xh_ref[HIST - (CONV_W - 1) + jj:HIST - (CONV_W - 1) + jj + c, :]
    qk = jax.nn.silu(y)
    tail = xh_ref[HIST + c - (CONV_W - 1):HIST + c, :]
    xh_ref[HIST - (CONV_W - 1):HIST, :] = tail

    pre = sm_ref[...] + gb_ref[...]
    pre_t = pre.T
    lf_c = _log_sigmoid(pre)
    lf_r = _log_sigmoid(pre_t[SM_F:SM_F + 8, :])
    tri = tri_ref[...]
    b_c = _sum3(jnp.dot(tri, _split3(lf_c), preferred_element_type=F32), 128)
    b_r = functools.reduce(lambda x, y: x + y, [_dot_nt(term, tri) for term in _split3_terms(lf_r)])
    row_t = lax.broadcasted_iota(jnp.int32, (c, c), 0)
    col_s = lax.broadcasted_iota(jnp.int32, (c, c), 1)
    causal = col_s <= row_t
    lane = lax.broadcasted_iota(jnp.int32, (1, 128), 1)
    ones_col = jnp.where(lax.broadcasted_iota(jnp.int32, (c, DV_D), 1) == 0, 1.0, 0.0).astype(BF16)
    m_new_row = mm_ref[...]
    for h in range(H_D):
        lo, hi = h * DK_D, (h + 1) * DK_D
        q = qk[:, lo:hi].astype(BF16)
        kf = qk[:, H_D * DK_D + lo:H_D * DK_D + hi] * (DK_D ** -0.5)
        v_aug = jnp.concatenate([zv_ref[:, lo:hi].astype(BF16), ones_col], axis=-1)
        m_prev = mm_ref[:, h:h + 1]
        bc = b_c[:, SM_F + h:SM_F + h + 1]
        a_c = pre[:, SM_I + h:SM_I + h + 1] - bc
        a_r = pre_t[SM_I + h:SM_I + h + 1, :] - b_r[h:h + 1, :]
        cmax = jnp.max(jnp.where(causal, a_r, -jnp.inf), axis=-1, keepdims=True)
        mt = bc + jnp.maximum(m_prev, cmax)
        dprev = jnp.exp(bc + m_prev - mt)
        dm = jnp.exp(jnp.where(causal, a_r + (bc - mt), NEG))
        s = _dot_nt(q, kf.astype(BF16)) * dm
        st = st_ref[h]
        nd = dprev * jnp.dot(q, st.astype(BF16), preferred_element_type=F32) \
            + jnp.dot(s.astype(BF16), v_aug, preferred_element_type=F32)
        den = nd[:, DV_D:DV_D + 1]
        hh = nd[:, 0:DV_D] / jnp.maximum(jnp.abs(den), jnp.exp(-mt))
        b_l = bc[c - 1:c, :]
        m_l = mt[c - 1:c, :]
        w = jnp.exp(a_c + b_l - m_l)
        dl = jnp.exp(b_l + m_prev - m_l)
        st_ref[h] = dl * st + lax.dot_general((w * kf).astype(BF16), v_aug, (((0,), (0,)), ((), ())),
                                              preferred_element_type=F32)
        m_new_row = jnp.where(lane == h, m_l, m_new_row)
        hh = hh * lax.rsqrt(jnp.mean(hh * hh, axis=-1, keepdims=True) + EPS) * ng_ref[...]
        o_ref[:, lo:hi] = hh * jax.nn.sigmoid(zo_ref[:, lo:hi])
    mm_ref[...] = m_new_row

    @pl.when(j == last)
    def _():
        conv_ref[0] = tail
        m_ref[0] = m_new_row
        for h in range(H_D):
            c_ref[0, h] = st_ref[h, :, 0:DV_D]
            n_ref[0, h:h + 1, :] = st_ref[h, :, DV_D:2 * DV_D].T[0:1, :]


def _mlstm(z, conv_buf, c0, n0, m0, conv_w, conv_b, gate_b, norm_g, b, t):
    c = min(t, 128)
    nc = t // c
    tri = jnp.asarray(np.tril(np.ones((c, c), np.float32)), BF16)
    gb = jnp.zeros((1, 128), F32).at[0, SM_I:SM_I + H_D].set(gate_b[0]).at[0, SM_F:SM_F + H_D].set(gate_b[1])
    m0p = jnp.zeros((b, 1, 128), F32).at[:, 0, :H_D].set(m0)
    w = H_D * DV_D
    blk = lambda width, k: pl.BlockSpec((c, width), lambda i, j, k=k: (i * nc + j, k))
    per_b = lambda shape: pl.BlockSpec((1,) + shape, lambda i, j: (i,) + (0,) * len(shape))
    o, conv, cc, nn, mm = pl.pallas_call(
        functools.partial(_mlstm_kernel, c=c),
        grid=(b, nc),
        in_specs=[blk(QK_CH_D, OZ_QK // QK_CH_D), blk(w, OZ_V // w), blk(w, OZ_O // w), blk(128, OZ_SM // 128),
                  per_b((CONV_W - 1, QK_CH_D)), per_b((H_D, DK_D, DV_D)), per_b((H_D, DK_D)), per_b((1, 128)),
                  _resident((CONV_W, QK_CH_D)), _resident((1, QK_CH_D)), _resident((1, 128)), _resident((1, DV_D)),
                  _resident((c, c))],
        out_specs=[pl.BlockSpec((c, w), lambda i, j: (i * nc + j, 0)),
                   per_b((CONV_W - 1, QK_CH_D)), per_b((H_D, DK_D, DV_D)), per_b((H_D, DK_D)), per_b((1, 128))],
        out_shape=[jax.ShapeDtypeStruct((b * t, w), F32), jax.ShapeDtypeStruct((b, CONV_W - 1, QK_CH_D), F32),
                   jax.ShapeDtypeStruct((b, H_D, DK_D, DV_D), F32), jax.ShapeDtypeStruct((b, H_D, DK_D), F32),
                   jax.ShapeDtypeStruct((b, 1, 128), F32)],
        scratch_shapes=[pltpu.VMEM((HIST + c, QK_CH_D), F32), pltpu.VMEM((H_D, DK_D, 2 * DV_D), F32),
                        pltpu.VMEM((1, 128), F32)],
        compiler_params=_cparams("parallel", "arbitrary"),
        name="mlstm",
    )(z, z, z, z, conv_buf, c0, n0, m0p, conv_w, conv_b.reshape(1, QK_CH_D), gb, norm_g.reshape(1, DV_D), tri)
    return o, conv, cc, nn, mm[:, 0, :H_D]


EZ_DQ, EZ_DK, EZ_DV = 4 * H_A * DK_A, 4 * H_A * DK_A + H_B * DV_B, 4 * H_A * DK_A + 2 * H_B * DV_B
DIFF_TQ = 256


def _diff_prep_kernel(zk_ref, zv_ref, cos_ref, sin_ref, kv_ref):
    cosf, sinf = cos_ref[...], sin_ref[...]
    w = H_B * DV_B
    for h in range(H_B):
        kv_ref[:, h * DV_B:(h + 1) * DV_B] = _rot64(zk_ref[:, h * DV_B:(h + 1) * DV_B], cosf, sinf)
    kv_ref[:, w:2 * w] = zv_ref[...]


def _diff_prep(z, cosf, sinf, t):
    m = z.shape[0]
    tm = min(512, t)
    nt = t // tm
    w = H_B * DV_B
    return pl.pallas_call(
        _diff_prep_kernel,
        grid=(m // tm,),
        in_specs=[pl.BlockSpec((tm, w), lambda i: (i, EZ_DK // w)), pl.BlockSpec((tm, w), lambda i: (i, EZ_DV // w)),
                  pl.BlockSpec((tm, DV_B), lambda i: (i % nt, 0)), pl.BlockSpec((tm, DV_B), lambda i: (i % nt, 0))],
        out_specs=pl.BlockSpec((tm, 2 * w), lambda i: (i, 0)),
        out_shape=jax.ShapeDtypeStruct((m, 2 * w), F32),
        compiler_params=_cparams("parallel"),
        name="diff_prep",
    )(z, z, cosf, sinf)


def _diff_lambda(lv_ref, lam_init):
    lv = lv_ref[...]
    return (jnp.exp(jnp.sum(lv[0:1] * lv[1:2], axis=-1, keepdims=True))
            - jnp.exp(jnp.sum(lv[2:3] * lv[3:4], axis=-1, keepdims=True)) + lam_init)


def _diff_queries(zq, cosf, sinf):
    q = _rot64(zq, cosf, sinf)
    lane = lax.broadcasted_iota(jnp.int32, q.shape, 1)
    return jnp.concatenate([jnp.where(lane < DH_B, q, 0.0), jnp.where(lane >= DH_B, q, 0.0)], axis=0).astype(BF16)


def _diff_finish(acc, l, lam, lam_init, ng, tq):
    o = acc[0:tq] / l[0:tq] - lam * (acc[tq:2 * tq] / l[tq:2 * tq])
    o = o * lax.rsqrt(jnp.mean(o * o, axis=-1, keepdims=True) + EPS) * ng
    return o * (1.0 - lam_init)


def _diff_kernel(zq_ref, cos_ref, sin_ref, k_ref, v_ref, lv_ref, ng_ref, o_ref, m_ref, l_ref, acc_ref,
                 *, tq, lam_init):
    t0 = pl.program_id(2) * tq
    scale = DH_B ** -0.5
    q2 = _diff_queries(zq_ref[...], cos_ref[...], sin_ref[...])
    m_ref[...] = jnp.full(m_ref.shape, NEG, F32)
    l_ref[...] = jnp.zeros(l_ref.shape, F32)
    acc_ref[...] = jnp.zeros(acc_ref.shape, F32)
    qpos = t0 + (lax.broadcasted_iota(jnp.int32, (2 * tq, 1), 0) & (tq - 1))
    kiota = lax.broadcasted_iota(jnp.int32, (1, tq), 1)

    def body(jk, carry):
        off = pl.multiple_of(jk * tq, tq)
        k = k_ref[pl.ds(off, tq), :].astype(BF16)
        v = v_ref[pl.ds(off, tq), :].astype(BF16)
        _flash_step(_dot_nt(q2, k) * scale, (off + kiota) <= qpos, v, m_ref, l_ref, acc_ref)
        return carry

    lax.fori_loop(0, pl.program_id(2) + 1, body, 0)
    lam = _diff_lambda(lv_ref, lam_init)
    o_ref[...] = _diff_finish(acc_ref[...], l_ref[...], lam, lam_init, ng_ref[...], tq)


def _diff_prompt(z, kv, cosf, sinf, lam_vecs, norm_g, layer_idx, b, t):
    tq = DIFF_TQ
    nq = t // tq
    assert tq & (tq - 1) == 0 and t % tq == 0
    lam_init = 0.8 - 0.6 * math.exp(-0.3 * layer_idx)
    return pl.pallas_call(
        functools.partial(_diff_kernel, tq=tq, lam_init=lam_init),
        grid=(b, H_B, nq),
        in_specs=[pl.BlockSpec((tq, DV_B), lambda i, h, j: (i * nq + j, EZ_DQ // DV_B + h)),
                  pl.BlockSpec((tq, DV_B), lambda i, h, j: (j, 0)),
                  pl.BlockSpec((tq, DV_B), lambda i, h, j: (j, 0)),
                  pl.BlockSpec((t, DV_B), lambda i, h, j: (i, h)),
                  pl.BlockSpec((t, DV_B), lambda i, h, j: (i, H_B + h)),
                  _resident(lam_vecs.shape), _resident((1, DV_B))],
        out_specs=pl.BlockSpec((tq, DV_B), lambda i, h, j: (i * nq + j, h)),
        out_shape=jax.ShapeDtypeStruct((b * t, H_B * DV_B), F32),
        scratch_shapes=[pltpu.VMEM((2 * tq, 1), F32), pltpu.VMEM((2 * tq, 1), F32), pltpu.VMEM((2 * tq, DV_B), F32)],
        compiler_params=_cparams("parallel", "parallel", "arbitrary"),
        name="diff_prompt",
    )(z, cosf, sinf, kv, kv, lam_vecs, norm_g.reshape(1, DV_B))


DEC_PAGES = 8


def _diff_dec_kernel(pt_ref, zq_ref, zk_ref, zv_ref, cos_ref, sin_ref, lv_ref, ng_ref, *rest, tq, n_pg, lam_init):
    pages = rest[:n_pg]
    o_ref, kv_ref, m_ref, l_ref, acc_ref, q2_ref = rest[n_pg:]
    step = pl.program_id(1)
    scale = DH_B ** -0.5
    w = H_B * DV_B
    cosf, sinf = cos_ref[...], sin_ref[...]

    @pl.when(step == 0)
    def _():
        m_ref[...] = jnp.full(m_ref.shape, NEG, F32)
        l_ref[...] = jnp.zeros(l_ref.shape, F32)
        acc_ref[...] = jnp.zeros(acc_ref.shape, F32)
        for h in range(H_B):
            q2_ref[h] = _diff_queries(zq_ref[:, h * DV_B:(h + 1) * DV_B], cosf, sinf)

    for h in range(H_B):
        lo, hi = h * DV_B, (h + 1) * DV_B
        q2 = q2_ref[h]
        sc = jnp.concatenate([_dot_nt(q2, pg[0, :, lo:hi].astype(BF16)) for pg in pages], axis=1) * scale
        m_old = m_ref[h]
        m_new = jnp.maximum(m_old, jnp.max(sc, axis=-1, keepdims=True))
        alpha = jnp.exp(m_old - m_new)
        p = jnp.exp(sc - m_new)
        l_ref[h] = alpha * l_ref[h] + jnp.sum(p, axis=-1, keepdims=True)
        pv = jnp.zeros((2 * tq, DV_B), F32)
        for i, pg in enumerate(pages):
            pv = pv + jnp.dot(p[:, i * PAGE_SIZE:(i + 1) * PAGE_SIZE].astype(BF16), pg[0, :, w + lo:w + hi].astype(BF16),
                              preferred_element_type=F32)
        acc_ref[h] = alpha * acc_ref[h] + pv
        m_ref[h] = m_new

    @pl.when(step == pl.num_programs(1) - 1)
    def _():
        lam = _diff_lambda(lv_ref, lam_init)
        qi = lax.broadcasted_iota(jnp.int32, (2 * tq, tq), 0) & (tq - 1)
        ki = lax.broadcasted_iota(jnp.int32, (2 * tq, tq), 1)
        kv_ref[:, w:2 * w] = zv_ref[...]
        for h in range(H_B):
            lo, hi = h * DV_B, (h + 1) * DV_B
            k_new = _rot64(zk_ref[:, lo:hi], cosf, sinf)
            kv_ref[:, lo:hi] = k_new
            s_new = _dot_nt(q2_ref[h].astype(F32), k_new) * scale
            valid = ki <= qi
            m_old = m_ref[h]
            m_new = jnp.maximum(m_old, jnp.max(jnp.where(valid, s_new, NEG), axis=-1, keepdims=True))
            alpha = jnp.exp(m_old - m_new)
            p = jnp.where(valid, jnp.exp(s_new - m_new), 0.0)
            l = alpha * l_ref[h] + jnp.sum(p, axis=-1, keepdims=True)
            acc = alpha * acc_ref[h] + jnp.dot(p, zv_ref[:, lo:hi], preferred_element_type=F32)
            o_ref[:, lo:hi] = _diff_finish(acc, l, lam, lam_init, ng_ref[...], tq)


def _diff_decode(z, pool, page_table, cosf, sinf, lam_vecs, norm_g, layer_idx, b, t):
    n_pages = page_table.shape[1]
    n_pg = DEC_PAGES
    assert n_pages % n_pg == 0 and t & (t - 1) == 0 and t % 8 == 0
    w = H_B * DV_B
    lam_init = 0.8 - 0.6 * math.exp(-0.3 * layer_idx)
    pool3 = pool.reshape(pool.shape[0], PAGE_SIZE, 2 * w)
    zcol = lambda k: pl.BlockSpec((t, w), lambda i, s, pt, k=k: (i, k))
    const = lambda shape: pl.BlockSpec(shape, lambda i, s, pt: (0,) * len(shape))
    page = lambda p: pl.BlockSpec((1, PAGE_SIZE, 2 * w), lambda i, s, pt, p=p: (pt[i, s * n_pg + p], 0, 0))
    grid_spec = pltpu.PrefetchScalarGridSpec(
        num_scalar_prefetch=1,
        grid=(b, n_pages // n_pg),
        in_specs=[zcol(EZ_DQ // w), zcol(EZ_DK // w), zcol(EZ_DV // w), const((t, DV_B)), const((t, DV_B)),
                  const(lam_vecs.shape), const((1, DV_B))] + [page(p) for p in range(n_pg)],
        out_specs=[pl.BlockSpec((t, w), lambda i, s, pt: (i, 0)), pl.BlockSpec((t, 2 * w), lambda i, s, pt: (i, 0))],
        scratch_shapes=[pltpu.VMEM((H_B, 2 * t, 1), F32), pltpu.VMEM((H_B, 2 * t, 1), F32),
                        pltpu.VMEM((H_B, 2 * t, DV_B), F32), pltpu.VMEM((H_B, 2 * t, DV_B), BF16)])
    return pl.pallas_call(
        functools.partial(_diff_dec_kernel, tq=t, n_pg=n_pg, lam_init=lam_init),
        grid_spec=grid_spec,
        out_shape=[jax.ShapeDtypeStruct((b * t, w), F32), jax.ShapeDtypeStruct((b * t, 2 * w), F32)],
        compiler_params=_cparams("parallel", "arbitrary"),
        name="diff_decode",
    )(page_table, z, z, z, cosf, sinf, lam_vecs, norm_g.reshape(1, DV_B), *([pool3] * n_pg))


CMP_PAGES = 16


def _nsa_dec_cmp_kernel(pt_ref, pe_ref, w_ref, *rest, n_pg):
    pages = rest[:n_pg]
    cmp_ref, x_ref = rest[n_pg:]
    for p, pg in enumerate(pages):
        for i in range(2 * G_C):
            x_ref[i, p * PAGE_SIZE:(p + 1) * PAGE_SIZE, :] = pg[0, :, i * DK_C:(i + 1) * DK_C]
    _compress_blocks(x_ref, pe_ref, w_ref, cmp_ref, n_pg * PAGE_SIZE // CMP_BLK)


def _nsa_dec_cmp(pool3, page_table, cmp_w, cmp_pe):
    b, n_pages = page_table.shape
    n_pg = CMP_PAGES
    assert n_pages % n_pg == 0 and PAGE_SIZE % CMP_BLK == 0
    gw = G_C * DK_C
    nb = n_pg * PAGE_SIZE // CMP_BLK
    steps = n_pages // n_pg
    const = lambda shape: pl.BlockSpec(shape, lambda i, s, pt: (0,) * len(shape))
    page = lambda p: pl.BlockSpec((1, PAGE_SIZE, 2 * gw), lambda i, s, pt, p=p: (pt[i, s * n_pg + p], 0, 0))
    grid_spec = pltpu.PrefetchScalarGridSpec(
        num_scalar_prefetch=1,
        grid=(b, steps),
        in_specs=[const(cmp_pe.shape), const(cmp_w.shape)] + [page(p) for p in range(n_pg)],
        out_specs=pl.BlockSpec((nb, 2 * gw), lambda i, s, pt: (i * steps + s, 0)),
        scratch_shapes=[pltpu.VMEM((2 * G_C, n_pg * PAGE_SIZE, DK_C), F32)])
    return pl.pallas_call(
        functools.partial(_nsa_dec_cmp_kernel, n_pg=n_pg),
        grid_spec=grid_spec,
        out_shape=jax.ShapeDtypeStruct((b * steps * nb, 2 * gw), F32),
        compiler_params=_cparams("parallel", "arbitrary"),
        name="nsa_dec_cmp",
    )(page_table, cmp_pe, cmp_w.astype(BF16), *([pool3] * n_pg))


def _softmax_rows(s):
    e = jnp.exp(s - jnp.max(s, axis=-1, keepdims=True))
    return e / jnp.sum(e, axis=-1, keepdims=True)


def _nsa_dec_kernel(pt_ref, zq_ref, zkv_ref, zsm_ref, cos_ref, sin_ref, cmp_ref, win_ref, e_ref, *rest,
                    tq, n_pg, pos0):
    pages = rest[:n_pg]
    o_ref, rows_ref, wout_ref, sel_ref, q_ref, ocmp_ref, m_ref, l_ref, acc_ref = rest[n_pg:]
    step = pl.program_id(1)
    scale = DK_C ** -0.5
    gw = G_C * DK_C
    n_c = cmp_ref.shape[0]
    cosf, sinf = cos_ref[...], sin_ref[...]
    rows2 = HPG_C * tq
    tpos = pos0 + (lax.broadcasted_iota(jnp.int32, (rows2, 1), 0) & (tq - 1))

    def stacked(fn):
        return lambda g: jnp.concatenate([fn(zq_ref[:, (g * HPG_C + p) * DK_C:(g * HPG_C + p + 1) * DK_C])
                                          for p in range(HPG_C)], axis=0)

    @pl.when(step == 0)
    def _():
        m_ref[...] = jnp.full(m_ref.shape, NEG, F32)
        l_ref[...] = jnp.zeros(l_ref.shape, F32)
        acc_ref[...] = jnp.zeros(acc_ref.shape, F32)
        lane = lax.broadcasted_iota(jnp.int32, (rows2, n_c), 1)
        lane_t = lane[0:tq]
        lanef = lane_t.astype(F32)
        for g in range(G_C):
            kc = cmp_ref[:, g * DK_C:(g + 1) * DK_C].astype(BF16)
            vc = cmp_ref[:, gw + g * DK_C:gw + (g + 1) * DK_C].astype(BF16)
            qg = stacked(lambda x: x)(g).astype(BF16)
            cmask = ((lane + 1) * CMP_BLK - 1) <= tpos
            pc = _softmax_rows(jnp.where(cmask, _dot_nt(qg, kc) * scale, NEG))
            pc = jnp.where(tpos >= CMP_BLK - 1, pc, 0.0)
            ocmp_ref[g] = jnp.dot(pc.astype(BF16), vc, preferred_element_type=F32)
            imp = functools.reduce(lambda a, b: a + b, [pc[p * tq:(p + 1) * tq] for p in range(HPG_C)])
            imp2 = jnp.concatenate([imp[:, c0:c0 + 128] + pltpu.roll(imp[:, c0:c0 + 128], 127, axis=1)
                                    for c0 in range(0, n_c, 128)], axis=1)
            cur2 = (tpos[0:tq] // SEL_BLK) * 2
            valid = ((lane_t & 1) == 0) & (lane_t <= cur2)
            forced = (lane_t == 0) | (lane_t == cur2) | (lane_t == cur2 - 2)
            score = jnp.where(valid, imp2 + jnp.where(forced, FORCE_BONUS, 0.0), -1.0)
            sel = jnp.zeros((tq, n_c), F32)
            for _ in range(N_SEL - 1):
                top = jnp.max(score, axis=-1, keepdims=True)
                idx = jnp.min(jnp.where(score == top, lanef, float(n_c)), axis=-1, keepdims=True)
                pick = lanef == idx
                sel = jnp.where(pick, jnp.where(top > -0.5, 1.0, 0.0), sel)
                score = jnp.where(pick, -3.0, score)
            sel_ref[g] = jnp.concatenate([sel] * HPG_C, axis=0).astype(BF16)
            q_ref[g] = stacked(lambda x: _rot128(x, cosf, sinf))(g).astype(BF16)

    for g in range(G_C):
        q = q_ref[g]
        selb = sel_ref[g]
        sc = jnp.concatenate([_dot_nt(q, pg[0, :, g * DK_C:(g + 1) * DK_C].astype(BF16)) for pg in pages], axis=1) * scale
        hit = jnp.concatenate([jnp.dot(selb, e_ref[p], preferred_element_type=F32) for p in range(n_pg)], axis=1)
        ok = hit > 0.5
        m_old = m_ref[g]
        m_new = jnp.maximum(m_old, jnp.max(jnp.where(ok, sc, NEG), axis=-1, keepdims=True))
        alpha = jnp.exp(m_old - m_new)
        p_ = jnp.where(ok, jnp.exp(sc - m_new), 0.0)
        l_ref[g] = alpha * l_ref[g] + jnp.sum(p_, axis=-1, keepdims=True)
        pv = jnp.zeros((rows2, DV_C), F32)
        for i, pg in enumerate(pages):
            pv = pv + jnp.dot(p_[:, i * PAGE_SIZE:(i + 1) * PAGE_SIZE].astype(BF16),
                              pg[0, :, gw + g * DK_C:gw + (g + 1) * DK_C].astype(BF16), preferred_element_type=F32)
        acc_ref[g] = alpha * acc_ref[g] + pv
        m_ref[g] = m_new

    @pl.when(step == pl.num_programs(1) - 1)
    def _():
        qi = lax.broadcasted_iota(jnp.int32, (rows2, tq), 0) & (tq - 1)
        ki = lax.broadcasted_iota(jnp.int32, (rows2, tq), 1)
        causal = ki <= qi
        wj = lax.broadcasted_iota(jnp.int32, (rows2, WINDOW), 1)
        wq = lax.broadcasted_iota(jnp.int32, (rows2, WINDOW), 0) & (tq - 1)
        in_win = wj >= wq
        rows_ref[:, 0:2 * gw] = zkv_ref[:, 0:2 * gw]
        rows_ref[:, 3 * gw:4 * gw] = zkv_ref[:, 3 * gw:4 * gw]
        wout_ref[0, 0:WINDOW - tq, :] = win_ref[0, tq:WINDOW, :]
        wout_ref[0, WINDOW - tq:WINDOW, gw:2 * gw] = zkv_ref[:, 5 * gw:6 * gw]
        for g in range(G_C):
            lo = g * DK_C
            q = q_ref[g].astype(F32)
            ks_new = _rot128(zkv_ref[:, 2 * gw + lo:2 * gw + lo + DK_C], cosf, sinf)
            kw_new = _rot128(zkv_ref[:, 4 * gw + lo:4 * gw + lo + DK_C], cosf, sinf)
            vs_new = zkv_ref[:, 3 * gw + lo:3 * gw + lo + DK_C]
            vw_new = zkv_ref[:, 5 * gw + lo:5 * gw + lo + DK_C]
            rows_ref[:, 2 * gw + lo:2 * gw + lo + DK_C] = ks_new
            wout_ref[0, WINDOW - tq:WINDOW, lo:lo + DK_C] = kw_new
            s_new = _dot_nt(q, ks_new) * scale
            m_old = m_ref[g]
            m_new = jnp.maximum(m_old, jnp.max(jnp.where(causal, s_new, NEG), axis=-1, keepdims=True))
            alpha = jnp.exp(m_old - m_new)
            p_ = jnp.where(causal, jnp.exp(s_new - m_new), 0.0)
            l_sel = alpha * l_ref[g] + jnp.sum(p_, axis=-1, keepdims=True)
            o_sel = (alpha * acc_ref[g] + jnp.dot(p_, vs_new, preferred_element_type=F32)) / l_sel
            kw = win_ref[0, :, lo:lo + DK_C].astype(BF16)
            vw = win_ref[0, :, gw + lo:gw + lo + DK_C].astype(BF16)
            s_buf = jnp.where(in_win, _dot_nt(q_ref[g], kw) * scale, NEG)
            s_own = jnp.where(causal, _dot_nt(q, kw_new) * scale, NEG)
            m_w = jnp.maximum(jnp.max(s_buf, axis=-1, keepdims=True), jnp.max(s_own, axis=-1, keepdims=True))
            p_buf = jnp.where(in_win, jnp.exp(s_buf - m_w), 0.0)
            p_own = jnp.where(causal, jnp.exp(s_own - m_w), 0.0)
            l_w = jnp.sum(p_buf, axis=-1, keepdims=True) + jnp.sum(p_own, axis=-1, keepdims=True)
            o_win = (jnp.dot(p_buf.astype(BF16), vw, preferred_element_type=F32)
                     + jnp.dot(p_own, vw_new, preferred_element_type=F32)) / l_w
            o_cmp = ocmp_ref[g]
            for p in range(HPG_C):
                h = g * HPG_C + p
                gate = lambda c: jax.nn.sigmoid(zsm_ref[:, SM_GATE + h * 3 + c:SM_GATE + h * 3 + c + 1])
                r = slice(p * tq, (p + 1) * tq)
                o_ref[:, h * DV_C:(h + 1) * DV_C] = gate(0) * o_cmp[r] + gate(1) * o_sel[r] + gate(2) * o_win[r]


def _nsa_decode(z, pool, page_table, win_buf, cosf, sinf, cmp_w, cmp_pe, pos0, b, t):
    n_pages = page_table.shape[1]
    n_pg = DEC_PAGES
    gw = G_C * DK_C
    n_c = n_pages * PAGE_SIZE // CMP_BLK
    assert pos0 == n_pages * PAGE_SIZE and pos0 % SEL_BLK == 0 and t <= SEL_BLK and t % 8 == 0 and t & (t - 1) == 0
    assert n_pages % n_pg == 0 and n_c % 128 == 0 and N_SEL >= 3 and win_buf.shape[1] == WINDOW and G_C == 2
    pool3 = pool.reshape(pool.shape[0], PAGE_SIZE, 4 * gw)
    cmp = _nsa_dec_cmp(pool3, page_table, cmp_w, cmp_pe)
    win3 = win_buf.reshape(b, WINDOW, 2 * gw)
    sb = np.arange(n_c)[None, :, None]
    key = np.arange(n_pages)[:, None, None] * PAGE_SIZE + np.arange(PAGE_SIZE)[None, None, :]
    expand = jnp.asarray((sb % 2 == 0) & (key // SEL_BLK == sb // 2), BF16)
    zblk = lambda width, k: pl.BlockSpec((t, width), lambda i, s, pt, k=k: (i, k))
    const = lambda shape: pl.BlockSpec(shape, lambda i, s, pt: (0,) * len(shape))
    page = lambda p: pl.BlockSpec((1, PAGE_SIZE, 2 * gw), lambda i, s, pt, p=p: (pt[i, s * n_pg + p], 0, 1))
    rows2 = HPG_C * t
    grid_spec = pltpu.PrefetchScalarGridSpec(
        num_scalar_prefetch=1,
        grid=(b, n_pages // n_pg),
        in_specs=[zblk(H_C * DK_C, OZ_Q // (H_C * DK_C)), zblk(6 * gw, OZ_KV // (6 * gw)), zblk(128, OZ_SM // 128),
                  const((t, DK_C)), const((t, DK_C)),
                  pl.BlockSpec((n_c, 2 * gw), lambda i, s, pt: (i, 0)),
                  pl.BlockSpec((1, WINDOW, 2 * gw), lambda i, s, pt: (i, 0, 0)),
                  pl.BlockSpec((n_pg, n_c, PAGE_SIZE), lambda i, s, pt: (s, 0, 0))] + [page(p) for p in range(n_pg)],
        out_specs=[pl.BlockSpec((t, H_C * DV_C), lambda i, s, pt: (i, 0)),
                   pl.BlockSpec((t, 4 * gw), lambda i, s, pt: (i, 0)),
                   pl.BlockSpec((1, WINDOW, 2 * gw), lambda i, s, pt: (i, 0, 0))],
        scratch_shapes=[pltpu.VMEM((G_C, rows2, n_c), BF16), pltpu.VMEM((G_C, rows2, DK_C), BF16),
                        pltpu.VMEM((G_C, rows2, DV_C), F32), pltpu.VMEM((G_C, rows2, 1), F32),
                        pltpu.VMEM((G_C, rows2, 1), F32), pltpu.VMEM((G_C, rows2, DV_C), F32)])
    o, rows, wout = pl.pallas_call(
        functools.partial(_nsa_dec_kernel, tq=t, n_pg=n_pg, pos0=pos0),
        grid_spec=grid_spec,
        out_shape=[jax.ShapeDtypeStruct((b * t, H_C * DV_C), F32), jax.ShapeDtypeStruct((b * t, 4 * gw), F32),
                   jax.ShapeDtypeStruct((b, WINDOW, 2 * gw), F32)],
        compiler_params=_cparams("parallel", "arbitrary"),
        name="nsa_decode",
    )(page_table, z, z, z, cosf, sinf, cmp, win3, expand, *([pool3] * n_pg))
    return o, rows, wout


def _j_rmsnorm(x, g):
    xf = x.astype(jnp.float32)
    y = xf * lax.rsqrt(jnp.mean(xf * xf, axis=-1, keepdims=True) + EPS)
    return (y * g.astype(jnp.float32)).astype(x.dtype)


def _j_split(z, sizes):
    cuts = [int(c) for c in np.cumsum(sizes)[:-1]]
    return jnp.split(z, cuts, axis=-1)


def _j_rope(x, pos):
    half = x.shape[-1] // 2
    inv = ROPE_THETA ** (-jnp.arange(half, dtype=jnp.float32) / half)
    ang = pos.astype(jnp.float32)[:, None] * inv[None, :]
    cos = jnp.cos(ang)[None, :, None, :]
    sin = jnp.sin(ang)[None, :, None, :]
    xf = x.astype(jnp.float32)
    x1, x2 = xf[..., :half], xf[..., half:]
    return jnp.concatenate([x1 * cos - x2 * sin, x2 * cos + x1 * sin], axis=-1).astype(x.dtype)


def _j_qblock(T, pref):
    return pref if T % pref == 0 else T


def _j_blocked(fn, qb, *xs):
    B, T = xs[0].shape[:2]
    nb = T // qb
    def split(a):
        return jnp.moveaxis(a.reshape((B, nb, qb) + a.shape[2:]), 1, 0)
    out = lax.map(lambda args: fn(args[0], *args[1:]), (jnp.arange(nb),) + tuple(split(a) for a in xs))
    out = jnp.moveaxis(out, 0, 1)
    return out.reshape((B, T) + out.shape[3:])


def _j_gather_pages(pool, page_table):
    g = pool[page_table]
    return g.reshape((g.shape[0], g.shape[1] * g.shape[2]) + g.shape[3:])


def _j_chunked(a, c, fill):
    B, T = a.shape[:2]
    Tp = -(-T // c) * c
    a = jnp.pad(a, [(0, 0), (0, Tp - T)] + [(0, 0)] * (a.ndim - 2), constant_values=fill)
    a = a.reshape((B, Tp // c, c) + a.shape[2:])
    return jnp.transpose(a, (1, 0, 3, 2) + tuple(range(4, a.ndim)))


def _j_unchunk(o, T):
    nc, B, H, c, D = o.shape
    return jnp.transpose(o, (1, 0, 3, 2, 4)).reshape(B, nc * c, H, D)[:, :T]


def _j_hgrn_lower_bounds(lb_raw):
    sm = jax.nn.softmax(lb_raw.astype(jnp.float32), axis=0)
    return jnp.cumsum(sm, axis=0) - sm[0:1]


def _j_gla_scan(q, k, v, log_f, S0):
    B, T = q.shape[:2]
    c = HGRN_CHUNK if T >= HGRN_CHUNK else T
    tri = jnp.tril(jnp.ones((c, c), dtype=bool))[:, :, None]
    def step(S, xs):
        qc, kc, vc, gc = xs
        b = jnp.cumsum(gc, axis=2)
        o = jnp.einsum('bhtd,bhde->bhte', qc * jnp.exp(b), S)
        diff = b[:, :, :, None, :] - b[:, :, None, :, :]
        decay = jnp.where(tri, jnp.exp(jnp.where(tri, diff, 0.0)), 0.0)
        att = jnp.einsum('bhtd,bhtsd,bhsd->bhts', qc, decay, kc)
        o = o + jnp.einsum('bhts,bhse->bhte', att, vc)
        bl = b[:, :, -1:, :]
        S = jnp.exp(bl[:, :, 0, :])[..., None] * S + jnp.einsum('bhsd,bhse->bhde', kc * jnp.exp(bl - b), vc)
        return S, o
    xs = tuple(_j_chunked(a, c, 0.0) for a in (q, k, v, log_f))
    S, o = lax.scan(step, S0, xs)
    return _j_unchunk(o, T), S


def _j_hgrn2_mixer(zq, zf, zi, zg, lb, norm_g, S0):
    B, T, _ = zq.shape
    f32 = jnp.float32
    q = zq.reshape(B, T, H_A, DK_A).astype(f32)
    a = zf.reshape(B, T, H_A, DK_A).astype(f32)
    lbh = lb.reshape(H_A, DK_A)
    f = lbh + (1.0 - lbh) * jax.nn.sigmoid(a)
    log_f = jnp.log(jnp.maximum(f, F_FLOOR))
    k = (1.0 - lbh) * jax.nn.sigmoid(-a)
    v = zi.reshape(B, T, H_A, DV_A).astype(f32)
    o, S = _j_gla_scan(q, k, v, log_f, S0.astype(f32))
    o = _j_rmsnorm(o, norm_g) * jax.nn.silu(zg.reshape(B, T, H_A, DV_A).astype(f32))
    return o.reshape(B, T, H_A * DV_A).astype(zq.dtype), S


def _j_diff_mixer(zq, zk, zv, pos, past_kv, lam_vecs, norm_g, layer_idx):
    B, T, _ = zq.shape
    f32 = jnp.float32
    def rot(a):
        a = a.reshape(B, T, H_B, DV_B)
        return jnp.concatenate([_j_rope(a[..., :DH_B], pos), _j_rope(a[..., DH_B:], pos)], axis=-1)
    q, k = rot(zq), rot(zk)
    v = zv.reshape(B, T, H_B, DV_B)
    new_kv = jnp.stack([k, v], axis=2)
    if past_kv is None:
        k_all, v_all, n_prev = k, v, 0
    else:
        k_all = jnp.concatenate([past_kv[:, :, 0], k], axis=1)
        v_all = jnp.concatenate([past_kv[:, :, 1], v], axis=1)
        n_prev = past_kv.shape[1]
    lam_init = 0.8 - 0.6 * math.exp(-0.3 * layer_idx)
    lv = lam_vecs.astype(f32)
    lam = jnp.exp(jnp.sum(lv[0] * lv[1])) - jnp.exp(jnp.sum(lv[2] * lv[3])) + lam_init
    L = k_all.shape[1]
    k1, k2 = k_all[..., :DH_B], k_all[..., DH_B:]
    kidx = jnp.arange(L)
    qb = _j_qblock(T, QBLK)
    scale = DH_B ** -0.5
    def blk(j, q_b):
        qidx = n_prev + j * qb + jnp.arange(qb)
        mask = kidx[None, :] <= qidx[:, None]
        s1 = jnp.einsum('bqhd,bkhd->bhqk', q_b[..., :DH_B], k1).astype(f32) * scale
        s2 = jnp.einsum('bqhd,bkhd->bhqk', q_b[..., DH_B:], k2).astype(f32) * scale
        p = jax.nn.softmax(jnp.where(mask, s1, NEG), axis=-1) - lam * jax.nn.softmax(jnp.where(mask, s2, NEG), axis=-1)
        return jnp.einsum('bhqk,bkhe->bqhe', p.astype(v_all.dtype), v_all)
    o = _j_blocked(blk, qb, q)
    o = _j_rmsnorm(o, norm_g).astype(f32) * (1.0 - lam_init)
    return o.reshape(B, T, H_B * DV_B).astype(zq.dtype), new_kv


def _j_nsa_attend(q, rows, wrows, n_prev, n_prev_w, pos, gates, cmp_w, cmp_pe):
    B, T = q.shape[:2]
    L = rows.shape[1]
    f32 = jnp.float32
    scale = DK_C ** -0.5
    qg = q.reshape(B, T, G_C, HPG_C, DK_C)
    qr = _j_rope(q, pos).reshape(B, T, G_C, HPG_C, DK_C)
    kc, vc, ks, vs = rows[:, :, 0], rows[:, :, 1], rows[:, :, 2], rows[:, :, 3]
    n_c = L // CMP_BLK
    def compress(a, w, pe):
        blk = a[:, :n_c * CMP_BLK].reshape(B, n_c, CMP_BLK, G_C, DK_C) + pe[None, None, :, None, :]
        return jnp.einsum('bnlgd,lde->bnge', blk, w)
    k_cmp = compress(kc, cmp_w[0], cmp_pe[0])
    v_cmp = compress(vc, cmp_w[1], cmp_pe[1])
    s = jnp.einsum('btgpd,bngd->bgptn', qg, k_cmp).astype(f32) * scale
    cmask = ((jnp.arange(n_c) + 1) * CMP_BLK - 1)[None, :] <= pos[:, None]
    p_cmp = jax.nn.softmax(jnp.where(cmask, s, NEG), axis=-1) * jnp.any(cmask, axis=-1)[:, None]
    o_cmp = jnp.einsum('bgptn,bnge->btgpe', p_cmp.astype(v_cmp.dtype), v_cmp)
    R = SEL_BLK // CMP_BLK
    n_s = -(-L // SEL_BLK)
    imp = p_cmp.sum(axis=2)
    imp = jnp.pad(imp, ((0, 0), (0, 0), (0, 0), (0, n_s * R - n_c))).reshape(B, G_C, T, n_s, R).sum(-1)
    sb = jnp.arange(n_s)[None, :]
    cur = (pos // SEL_BLK)[:, None]
    valid = sb <= cur
    forced = (sb == 0) | (sb == cur) | (sb == cur - 1)
    score = jnp.where(valid, imp + FORCE_BONUS * forced.astype(f32), -1.0)
    n_top = min(N_SEL, n_s)
    top_v, top_i = lax.top_k(score, n_top)
    top_ok = jnp.transpose(top_v > -0.5, (0, 2, 1, 3))
    top_i = jnp.transpose(top_i, (0, 2, 1, 3))
    pad_s = n_s * SEL_BLK - L
    def blockify(a):
        a = jnp.pad(a, ((0, 0), (0, pad_s), (0, 0), (0, 0)))
        return jnp.transpose(a.reshape(B, n_s, SEL_BLK, G_C, DK_C), (0, 3, 1, 2, 4))
    ks_b, vs_b = blockify(ks), blockify(vs)
    gather = jax.vmap(jax.vmap(lambda a, i: a[i]))
    qs = _j_qblock(T, SEL_QBLK)
    def sel_fn(j, q_b, i_b, ok_b):
        nq = q_b.shape[1]
        idx = jnp.transpose(i_b, (0, 2, 1, 3))
        ok = jnp.transpose(ok_b, (0, 2, 1, 3))
        flat = idx.reshape(B, G_C, nq * n_top)
        kg = gather(ks_b, flat).reshape(B, G_C, nq, n_top * SEL_BLK, DK_C)
        vg = gather(vs_b, flat).reshape(B, G_C, nq, n_top * SEL_BLK, DV_C)
        kpos = (idx[..., None] * SEL_BLK + jnp.arange(SEL_BLK)).reshape(B, G_C, nq, n_top * SEL_BLK)
        tq = n_prev + j * qs + jnp.arange(nq)
        m = jnp.repeat(ok, SEL_BLK, axis=-1) & (kpos <= tq[None, None, :, None])
        sc = jnp.einsum('bqgpd,bgqkd->bgpqk', q_b, kg).astype(f32) * scale
        p = jax.nn.softmax(jnp.where(m[:, :, None], sc, NEG), axis=-1)
        return jnp.einsum('bgpqk,bgqke->bqgpe', p.astype(vg.dtype), vg)
    o_sel = _j_blocked(sel_fn, qs, qr, top_i, top_ok)
    padw = WINDOW - n_prev_w
    kwp = jnp.pad(wrows[:, :, 0], ((0, 0), (padw, 0), (0, 0), (0, 0)))
    vwp = jnp.pad(wrows[:, :, 1], ((0, 0), (padw, 0), (0, 0), (0, 0)))
    qw = _j_qblock(T, QBLK)
    def win_fn(j, q_b):
        start = j * qw
        kb = lax.dynamic_slice_in_dim(kwp, start, qw + WINDOW, axis=1)
        vb = lax.dynamic_slice_in_dim(vwp, start, qw + WINDOW, axis=1)
        pl_ = jnp.arange(qw + WINDOW)
        rel = jnp.arange(qw)[:, None] + WINDOW - pl_[None, :]
        m = (rel >= 0) & (rel <= WINDOW) & ((start + pl_) >= padw)[None, :]
        sc = jnp.einsum('bqgpd,bkgd->bgpqk', q_b, kb).astype(f32) * scale
        p = jax.nn.softmax(jnp.where(m, sc, NEG), axis=-1)
        return jnp.einsum('bgpqk,bkge->bqgpe', p.astype(vb.dtype), vb)
    o_win = _j_blocked(win_fn, qw, qr)
    g = gates.reshape(B, T, G_C, HPG_C, 3)
    o = g[..., 0:1] * o_cmp + g[..., 1:2] * o_sel + g[..., 2:3] * o_win
    return o.reshape(B, T, H_C * DV_C).astype(q.dtype)


def _j_nsa_mixer(zq, zkc, zvc, zks, zvs, zkw, zvw, zg, pos, past_rows, win_buf, cmp_w, cmp_pe):
    B, T, _ = zq.shape
    heads = lambda z: z.reshape(B, T, G_C, DK_C)
    q = zq.reshape(B, T, H_C, DK_C)
    rows = jnp.stack([heads(zkc), heads(zvc), _j_rope(heads(zks), pos), heads(zvs)], axis=2)
    wrows = jnp.stack([_j_rope(heads(zkw), pos), heads(zvw)], axis=2)
    if past_rows is None:
        all_rows, n_prev = rows, 0
    else:
        all_rows, n_prev = jnp.concatenate([past_rows, rows], axis=1), past_rows.shape[1]
    if win_buf is None:
        all_w, n_prev_w = wrows, 0
    else:
        all_w, n_prev_w = jnp.concatenate([win_buf.astype(wrows.dtype), wrows], axis=1), win_buf.shape[1]
    new_win = all_w[:, -min(WINDOW, n_prev_w + T):]
    gates = jax.nn.sigmoid(zg.reshape(B, T, H_C, 3).astype(jnp.float32))
    o = _j_nsa_attend(q, all_rows, all_w, n_prev, n_prev_w, pos, gates, cmp_w, cmp_pe)
    return o, rows, new_win


def _j_mlstm_scan(q, k, v, ig, lf, C0, n0, m0):
    B, T = q.shape[:2]
    c = MLSTM_CHUNK if T >= MLSTM_CHUNK else T
    tri = jnp.tril(jnp.ones((c, c), dtype=bool))
    def step(carry, xs):
        C, n, m = carry
        qc, kc, vc, ic, fc = xs
        b = jnp.cumsum(fc, axis=-1)
        a = ic - b
        mt = b + jnp.maximum(m[..., None], lax.cummax(a, axis=2))
        dprev = jnp.exp(b + m[..., None] - mt)
        Dm = jnp.exp(jnp.where(tri, a[:, :, None, :] + b[:, :, :, None] - mt[:, :, :, None], NEG))
        s = jnp.einsum('bhtd,bhsd->bhts', qc, kc) * Dm
        num = dprev[..., None] * jnp.einsum('bhtd,bhde->bhte', qc, C) + jnp.einsum('bhts,bhse->bhte', s, vc)
        den = dprev * jnp.einsum('bhtd,bhd->bht', qc, n) + s.sum(-1)
        h = num / jnp.maximum(jnp.abs(den), jnp.exp(-mt))[..., None]
        mL = mt[..., -1]
        w = jnp.exp(a + b[..., -1:] - mL[..., None])
        dl = jnp.exp(b[..., -1] + m - mL)
        C = dl[..., None, None] * C + jnp.einsum('bhs,bhsd,bhse->bhde', w, kc, vc)
        n = dl[..., None] * n + jnp.einsum('bhs,bhsd->bhd', w, kc)
        return (C, n, mL), h
    xs = (_j_chunked(q, c, 0.0), _j_chunked(k, c, 0.0), _j_chunked(v, c, 0.0), _j_chunked(ig, c, NEG), _j_chunked(lf, c, 0.0))
    (C, n, m), h = lax.scan(step, (C0, n0, m0), xs)
    return _j_unchunk(h, T), C, n, m


def _j_mlstm_mixer(zqk, zv, zi, zf, zo, conv_w, conv_b, gate_b, norm_g, conv_buf, C0, n0, m0):
    B, T, _ = zqk.shape
    f32 = jnp.float32
    xp = jnp.concatenate([conv_buf.astype(zqk.dtype), zqk], axis=1)
    y = conv_b
    for j in range(CONV_W):
        y = y + conv_w[j] * xp[:, j:j + T]
    qk = jax.nn.silu(y)
    q = qk[..., :H_D * DK_D].reshape(B, T, H_D, DK_D).astype(f32)
    k = qk[..., H_D * DK_D:].reshape(B, T, H_D, DK_D).astype(f32) * (DK_D ** -0.5)
    v = zv.reshape(B, T, H_D, DV_D).astype(f32)
    ig = zi.astype(f32) + gate_b[0].astype(f32)
    lf = jax.nn.log_sigmoid(zf.astype(f32) + gate_b[1].astype(f32))
    h, C, n, m = _j_mlstm_scan(q, k, v, ig, lf, C0.astype(f32), n0.astype(f32), m0.astype(f32))
    h = _j_rmsnorm(h, norm_g) * jax.nn.sigmoid(zo.reshape(B, T, H_D, DV_D).astype(f32))
    return h.reshape(B, T, H_D * DV_D).astype(zqk.dtype), xp[:, -(CONV_W - 1):], C, n, m


def _run_group(x, pos0, mem_kv, past, W):
    B, T, D = x.shape
    pos = pos0 + jnp.arange(T, dtype=jnp.int32)
    cos32, sin32 = _rope_tables(pos, DH_B // 2, DV_B)
    cos64, sin64 = _rope_tables(pos, DK_C // 2, DK_C)
    new = {name: [] for name in ('diff_kv', 'hgrn', 'nsa_kv', 'nsa_win', 'm_C', 'm_n', 'm_m', 'm_conv')}
    x = x.reshape(B * T, D)
    for l in range(DEPTH):
        g = W['norm_w'][l]
        x = _ffn(x, g[0], W['ffn_w_gate'][l, 0], W['ffn_w_up'][l, 0], W['ffn_w_down'][l, 0])
        i = l // 2
        if l % 2 == 0:
            z = _normmm(x, g[1], W['w_in_even'][i])
            S0 = jnp.zeros((B, H_A, DK_A, DV_A), F32) if past is None else past['hgrn'][i]
            oa, S = _hgrn(z, S0, W['hgrn_lb_raw'], W['hgrn_norm'][i], i, B, T)
            if past is None:
                kv_new = _diff_prep(z, cos32, sin32, T)
                ob = _diff_prompt(z, kv_new, cos32, sin32, W['diff_lam'][i], W['diff_norm'][i], l, B, T)
                kv_new = kv_new.reshape(B, T, 2, H_B, DV_B)
            else:
                ob, kv_new = _diff_decode(z, past['diff_pool'][i], past['page_table'], cos32, sin32,
                                          W['diff_lam'][i], W['diff_norm'][i], l, B, T)
                kv_new = kv_new.reshape(B, T, 2, H_B, DV_B)
            w_out = W['w_out_even'][i]
            new['hgrn'].append(S)
            new['diff_kv'].append(kv_new)
        else:
            z = _normmm(x, g[1], W['w_in_odd'][i])
            if past is None:
                conv_buf = jnp.zeros((B, CONV_W - 1, QK_CH_D), F32)
                C0 = jnp.zeros((B, H_D, DK_D, DV_D), F32)
                n0 = jnp.zeros((B, H_D, DK_D), F32)
                m0 = jnp.zeros((B, H_D), F32)
                rows, wrows, cmp = _nsa_prep(z, cos64, sin64, W['nsa_cmp_w'][i], W['nsa_cmp_pe'][i], T)
                oa = _nsa_prompt(z, rows, wrows, cmp, cos64, sin64, B, T)
                rows = rows.reshape(B, T, 4, G_C, DK_C)
                win_new = wrows.reshape(B, T, 2, G_C, DK_C)[:, -min(WINDOW, T):]
            else:
                conv_buf, C0, n0, m0 = past['m_conv'][i], past['m_C'][i], past['m_n'][i], past['m_m'][i]
                oa, rows, win_new = _nsa_decode(z, past['nsa_pool'][i], past['page_table'], past['nsa_win'][i],
                                                cos64, sin64, W['nsa_cmp_w'][i], W['nsa_cmp_pe'][i], pos0, B, T)
                rows = rows.reshape(B, T, 4, G_C, DK_C)
                win_new = win_new.reshape(B, WINDOW, 2, G_C, DK_C)
            ob, conv_new, C, n, m = _mlstm(z, conv_buf, C0, n0, m0, W['mlstm_conv_w'][i], W['mlstm_conv_b'][i],
                                           W['mlstm_gate_b'][i], W['mlstm_norm'][i], B, T)
            w_out = W['w_out_odd'][i]
            new['nsa_kv'].append(rows)
            new['nsa_win'].append(win_new)
            new['m_C'].append(C)
            new['m_n'].append(n)
            new['m_m'].append(m)
            new['m_conv'].append(conv_new)
        x = _mmres(x, [oa.reshape(B * T, GROUP_W), ob.reshape(B * T, GROUP_W)], [w_out[:GROUP_W], w_out[GROUP_W:]])
        q = _normmm(x, g[2], W['mem_wq'][l])
        a = _xattn(q, mem_kv[l], T)
        x = _mmres(x, [a], [W['mem_wo'][l]])
        x = _ffn(x, g[3], W['ffn_w_gate'][l, 1], W['ffn_w_up'][l, 1], W['ffn_w_down'][l, 1])
    return _rmsnorm_rows(x, W['norm_final']).reshape(B, T, D), new


def kernel(x_prompt, x_sample, mem_prompt, cache_diff_kv, cache_nsa_kv, state_nsa_win, state_hgrn, state_mlstm_C, state_mlstm_n, state_mlstm_m, state_mlstm_conv, cache_mem_kv, page_table, norm_w, norm_final, ffn_w_gate, ffn_w_up, ffn_w_down, w_in_even, w_out_even, w_in_odd, w_out_odd, hgrn_lb_raw, hgrn_norm, diff_lam, diff_norm, nsa_cmp_w, nsa_cmp_pe, mlstm_conv_w, mlstm_conv_b, mlstm_gate_b, mlstm_norm, mem_wq, mem_wkv, mem_wo):
    bf = lambda w: w.astype(BF16)
    o_q, o_kv, o_g, o_qk, o_v, o_i, o_f, o_o = (int(c) for c in np.cumsum((0,) + ODD_SPLITS)[[0, 1, 7, 8, 9, 10, 11, 12]])
    w_in_odd = jnp.concatenate(
        [w_in_odd[..., o_kv:o_g], w_in_odd[..., o_q:o_kv], w_in_odd[..., o_qk:o_v], w_in_odd[..., o_v:o_i],
         w_in_odd[..., o_o:], w_in_odd[..., o_g:o_qk], w_in_odd[..., o_i:o_o],
         jnp.zeros(w_in_odd.shape[:2] + (OZ_WIDTH - IN_ODD,), w_in_odd.dtype)], axis=-1)
    W = {'norm_w': norm_w, 'norm_final': norm_final, 'ffn_w_gate': bf(ffn_w_gate), 'ffn_w_up': bf(ffn_w_up),
         'ffn_w_down': bf(ffn_w_down), 'w_in_even': bf(w_in_even), 'w_out_even': bf(w_out_even),
         'w_in_odd': bf(w_in_odd), 'w_out_odd': bf(w_out_odd),
         'hgrn_lb_raw': hgrn_lb_raw, 'hgrn_norm': hgrn_norm, 'diff_lam': diff_lam,
         'diff_norm': diff_norm, 'nsa_cmp_w': nsa_cmp_w, 'nsa_cmp_pe': nsa_cmp_pe, 'mlstm_conv_w': mlstm_conv_w,
         'mlstm_conv_b': mlstm_conv_b, 'mlstm_gate_b': mlstm_gate_b, 'mlstm_norm': mlstm_norm,
         'mem_wq': bf(mem_wq), 'mem_wo': bf(mem_wo)}
    Bp, M = mem_prompt.shape[0], mem_prompt.shape[1]
    wkv = bf(mem_wkv)
    ones = jnp.ones((D_MODEL,), F32)
    mem_flat = mem_prompt.reshape(Bp * M, D_MODEL)
    mem_kv_prompt = [_normmm(mem_flat, ones, wkv[l], norm=False).reshape(Bp, M, 2 * MEM_W) for l in range(DEPTH)]
    y_prompt, newp = _run_group(x_prompt, 0, mem_kv_prompt, None, W)
    past = {'page_table': page_table, 'diff_pool': cache_diff_kv, 'nsa_pool': cache_nsa_kv,
            'nsa_win': state_nsa_win, 'hgrn': state_hgrn, 'm_C': state_mlstm_C, 'm_n': state_mlstm_n,
            'm_m': state_mlstm_m, 'm_conv': state_mlstm_conv}
    past_len = page_table.shape[1] * PAGE_SIZE
    mem_kv_sample = [cache_mem_kv[l].reshape(cache_mem_kv.shape[1], MEM_LEN, 2 * MEM_W) for l in range(DEPTH)]
    y_sample, news = _run_group(x_sample, past_len, mem_kv_sample, past, W)
    mem_out = jnp.stack(mem_kv_prompt).reshape(DEPTH, Bp, M, 2, MEM_HEADS, MEM_DH)
    return (y_prompt, y_sample,
            jnp.stack(newp['diff_kv']), jnp.stack(news['diff_kv']),
            jnp.stack(newp['nsa_kv']), jnp.stack(news['nsa_kv']),
            jnp.stack(newp['nsa_win']), jnp.stack(news['nsa_win']),
            jnp.stack(newp['hgrn']), jnp.stack(news['hgrn']),
            jnp.stack(newp['m_C']), jnp.stack(news['m_C']),
            jnp.stack(newp['m_n']), jnp.stack(news['m_n']),
            jnp.stack(newp['m_m']), jnp.stack(news['m_m']),
            jnp.stack(newp['m_conv']), jnp.stack(news['m_conv']),
            mem_out)
```

```python
import functools
import math

import numpy as np
import jax
import jax.numpy as jnp
from jax import lax
from jax.experimental import pallas as pl
from jax.experimental.pallas import tpu as pltpu

F32 = jnp.float32
BF16 = jnp.bfloat16

D_MODEL = 1024
DEPTH = 4
PAGE_SIZE = 128
N_EVEN = (DEPTH + 1) // 2
N_ODD = DEPTH // 2
GROUP_W = D_MODEL // 2
MIX_W = 2 * GROUP_W
H_A = 4
DK_A = 128
DV_A = GROUP_W // H_A
HGRN_CHUNK = 16
F_FLOOR = 1e-30
H_B = 4
DH_B = 64
DV_B = 2 * DH_B
H_C = 4
G_C = 2
HPG_C = H_C // G_C
DK_C = 128
DV_C = GROUP_W // H_C
CMP_BLK = 32
SEL_BLK = 64
N_SEL = 16
WINDOW = 512
SEL_QBLK = 32
H_D = 4
DK_D = 128
DV_D = GROUP_W // H_D
MLSTM_CHUNK = 64
CONV_W = 4
QK_CH_D = 2 * H_D * DK_D
MEM_HEADS = 4
MEM_DH = 128
MEM_W = MEM_HEADS * MEM_DH
MEM_LEN = 256
D_FF = 2816
ROPE_THETA = 10000.0
QBLK = 128
EPS = 1e-6
NEG = -1e30
FORCE_BONUS = 1e4

EVEN_SPLITS = (H_A * DK_A, H_A * DK_A, H_A * DV_A, H_A * DV_A, H_B * DV_B, H_B * DV_B, H_B * DV_B)
ODD_SPLITS = (H_C * DK_C,) + (G_C * DK_C,) * 6 + (3 * H_C, QK_CH_D, H_D * DV_D, H_D, H_D, H_D * DV_D)
IN_EVEN = sum(EVEN_SPLITS)
IN_ODD = sum(ODD_SPLITS)

VMEM_LIMIT_BYTES = 56 * 1024 * 1024
FFN_CHUNK = D_FF // 2


def _cparams(*sem):
    return pltpu.CompilerParams(dimension_semantics=sem, vmem_limit_bytes=VMEM_LIMIT_BYTES)


def _row_tile(m):
    for t in (512, 256, 128, 64, 32, 16, 8):
        if m % t == 0:
            return t
    raise ValueError(f"row count {m} is not a multiple of 8")


def _resident(shape):
    return pl.BlockSpec(shape, lambda *_: (0,) * len(shape), pipeline_mode=pl.Buffered(1))


def _rms(x, g):
    return x * lax.rsqrt(jnp.mean(x * x, axis=-1, keepdims=True) + EPS) * g


def _ffn_kernel(x_ref, g_ref, wg_ref, wu_ref, wd_ref, o_ref):
    x = x_ref[...]
    h = _rms(x, g_ref[...]).astype(BF16)
    acc = jnp.zeros_like(x)
    for c in range(D_FF // FFN_CHUNK):
        sl = slice(c * FFN_CHUNK, (c + 1) * FFN_CHUNK)
        gate = jnp.dot(h, wg_ref[:, sl], preferred_element_type=F32)
        up = jnp.dot(h, wu_ref[:, sl], preferred_element_type=F32)
        a = (jax.nn.silu(gate) * up).astype(BF16)
        acc = acc + jnp.dot(a, wd_ref[sl, :], preferred_element_type=F32)
    o_ref[...] = x + 0.5 * acc


def _ffn(x, g, wg, wu, wd):
    m, d = x.shape
    tm = _row_tile(m)
    return pl.pallas_call(
        _ffn_kernel,
        grid=(m // tm,),
        in_specs=[pl.BlockSpec((tm, d), lambda i: (i, 0)), _resident((1, d)),
                  _resident(wg.shape), _resident(wu.shape), _resident(wd.shape)],
        out_specs=pl.BlockSpec((tm, d), lambda i: (i, 0)),
        out_shape=jax.ShapeDtypeStruct((m, d), F32),
        compiler_params=_cparams("parallel"),
        name="ffn",
    )(x, g.reshape(1, d), wg, wu, wd)


def _normmm_kernel(x_ref, g_ref, w_ref, o_ref, *, norm):
    x = x_ref[...]
    if norm:
        x = _rms(x, g_ref[...])
    o_ref[...] = jnp.dot(x.astype(BF16), w_ref[...], preferred_element_type=F32)


def _normmm(x, g, w, *, norm=True):
    m, d = x.shape
    n = w.shape[1]
    tm = _row_tile(m)
    return pl.pallas_call(
        functools.partial(_normmm_kernel, norm=norm),
        grid=(m // tm,),
        in_specs=[pl.BlockSpec((tm, d), lambda i: (i, 0)), _resident((1, d)), _resident(w.shape)],
        out_specs=pl.BlockSpec((tm, n), lambda i: (i, 0)),
        out_shape=jax.ShapeDtypeStruct((m, n), F32),
        compiler_params=_cparams("parallel"),
        name="normmm",
    )(x, g.reshape(1, d), w)


def _mmres_kernel(*refs, n_in):
    x_ref, o_ref = refs[0], refs[-1]
    acc = x_ref[...]
    for i in range(n_in):
        acc = acc + jnp.dot(refs[1 + i][...].astype(BF16), refs[1 + n_in + i][...], preferred_element_type=F32)
    o_ref[...] = acc


def _mmres(x, acts, ws):
    m, d = x.shape
    tm = _row_tile(m)
    n_in = len(acts)
    return pl.pallas_call(
        functools.partial(_mmres_kernel, n_in=n_in),
        grid=(m // tm,),
        in_specs=([pl.BlockSpec((tm, d), lambda i: (i, 0))]
                  + [pl.BlockSpec((tm, a.shape[1]), lambda i: (i, 0)) for a in acts]
                  + [_resident(w.shape) for w in ws]),
        out_specs=pl.BlockSpec((tm, d), lambda i: (i, 0)),
        out_shape=jax.ShapeDtypeStruct((m, d), F32),
        compiler_params=_cparams("parallel"),
        name="mmres",
    )(x, *acts, *ws)


def _rmsnorm_kernel(x_ref, g_ref, o_ref):
    o_ref[...] = _rms(x_ref[...], g_ref[...])


def _rmsnorm_rows(x, g):
    m, d = x.shape
    tm = _row_tile(m)
    return pl.pallas_call(
        _rmsnorm_kernel,
        grid=(m // tm,),
        in_specs=[pl.BlockSpec((tm, d), lambda i: (i, 0)), _resident((1, d))],
        out_specs=pl.BlockSpec((tm, d), lambda i: (i, 0)),
        out_shape=jax.ShapeDtypeStruct((m, d), F32),
        compiler_params=_cparams("parallel"),
        name="final_norm",
    )(x, g.reshape(1, d))


def _xattn_kernel(q_ref, kv_ref, o_ref):
    scale = MEM_DH ** -0.5
    for h in range(MEM_HEADS):
        lo, hi = h * MEM_DH, (h + 1) * MEM_DH
        q = q_ref[:, lo:hi].astype(BF16)
        k = kv_ref[0, :, lo:hi].astype(BF16)
        v = kv_ref[0, :, MEM_W + lo:MEM_W + hi].astype(BF16)
        s = lax.dot_general(q, k, (((1,), (1,)), ((), ())), preferred_element_type=F32) * scale
        p = jnp.exp(s - jnp.max(s, axis=-1, keepdims=True))
        l = jnp.sum(p, axis=-1, keepdims=True)
        o = jnp.dot(p.astype(BF16), v, preferred_element_type=F32)
        o_ref[:, lo:hi] = o / l


def _xattn(q, kv, t):
    b = kv.shape[0]
    tq = min(t, 512)
    nq = t // tq
    return pl.pallas_call(
        _xattn_kernel,
        grid=(b, nq),
        in_specs=[pl.BlockSpec((tq, MEM_W), lambda i, j: (i * nq + j, 0)),
                  pl.BlockSpec((1, MEM_LEN, 2 * MEM_W), lambda i, j: (i, 0, 0))],
        out_specs=pl.BlockSpec((tq, MEM_W), lambda i, j: (i * nq + j, 0)),
        out_shape=jax.ShapeDtypeStruct((b * t, MEM_W), F32),
        compiler_params=_cparams("parallel", "parallel"),
        name="xattn",
    )(q, kv)


def _rope_tables(pos, half, width):
    inv = ROPE_THETA ** (-jnp.arange(half, dtype=F32) / half)
    ang = pos.astype(F32)[:, None] * inv[None, :]
    cos, sin = jnp.cos(ang), jnp.sin(ang)
    reps = width // (2 * half)
    return (jnp.tile(jnp.concatenate([cos, cos], axis=-1), (1, reps)),
            jnp.tile(jnp.concatenate([-sin, sin], axis=-1), (1, reps)))


def _rot128(x, cosf, sinf):
    return x * cosf + pltpu.roll(x, 64, axis=1) * sinf


def _rot64(x, cosf, sinf):
    lane = lax.broadcasted_iota(jnp.int32, x.shape, 1)
    swapped = jnp.where((lane & 63) < 32, pltpu.roll(x, 96, axis=1), pltpu.roll(x, 32, axis=1))
    return x * cosf + swapped * sinf


def _dot_nt(a, b):
    return lax.dot_general(a, b, (((1,), (1,)), ((), ())), preferred_element_type=F32)


OZ_KV, OZ_Q, OZ_QK, OZ_V, OZ_O, OZ_SM = 0, 1536, 2048, 3072, 3584, 4096
OZ_WIDTH = 4224
NSA_TQ = 256


def _compress_blocks(x_ref, pe_ref, w_ref, cmp_ref, nb):
    acc = [jnp.zeros((nb, DK_C), F32) for _ in range(2 * G_C)]
    for l in range(CMP_BLK):
        for i in range(2 * G_C):
            c = i // G_C
            x = x_ref[i, pl.ds(l, nb, stride=CMP_BLK), :] + pe_ref[c, l:l + 1, :]
            acc[i] += jnp.dot(x.astype(BF16), w_ref[c, l], preferred_element_type=F32)
    for i in range(2 * G_C):
        cmp_ref[:, i * DK_C:(i + 1) * DK_C] = acc[i]


def _nsa_prep_kernel(z_ref, cos_ref, sin_ref, pe_ref, w_ref, rows_ref, wrows_ref, cmp_ref, x_ref, *, tm):
    cosf, sinf = cos_ref[...], sin_ref[...]
    gw = G_C * DK_C
    rows_ref[:, 0:2 * gw] = z_ref[:, 0:2 * gw]
    rows_ref[:, 3 * gw:4 * gw] = z_ref[:, 3 * gw:4 * gw]
    wrows_ref[:, gw:2 * gw] = z_ref[:, 5 * gw:6 * gw]
    for g in range(G_C):
        lo = g * DK_C
        rows_ref[:, 2 * gw + lo:2 * gw + lo + DK_C] = _rot128(z_ref[:, 2 * gw + lo:2 * gw + lo + DK_C], cosf, sinf)
        wrows_ref[:, lo:lo + DK_C] = _rot128(z_ref[:, 4 * gw + lo:4 * gw + lo + DK_C], cosf, sinf)
    for i in range(2 * G_C):
        x_ref[i] = z_ref[:, i * DK_C:(i + 1) * DK_C]
    _compress_blocks(x_ref, pe_ref, w_ref, cmp_ref, tm // CMP_BLK)


def _nsa_prep(z, cosf, sinf, cmp_w, cmp_pe, t):
    m = z.shape[0]
    tm = min(512, t)
    nt = t // tm
    gw = G_C * DK_C
    return pl.pallas_call(
        functools.partial(_nsa_prep_kernel, tm=tm),
        grid=(m // tm,),
        in_specs=[pl.BlockSpec((tm, 6 * gw), lambda i: (i, OZ_KV // (6 * gw))),
                  pl.BlockSpec((tm, DK_C), lambda i: (i % nt, 0)),
                  pl.BlockSpec((tm, DK_C), lambda i: (i % nt, 0)),
                  _resident(cmp_pe.shape), _resident(cmp_w.shape)],
        out_specs=[pl.BlockSpec((tm, 4 * gw), lambda i: (i, 0)),
                   pl.BlockSpec((tm, 2 * gw), lambda i: (i, 0)),
                   pl.BlockSpec((tm // CMP_BLK, 2 * gw), lambda i: (i, 0))],
        out_shape=[jax.ShapeDtypeStruct((m, 4 * gw), F32), jax.ShapeDtypeStruct((m, 2 * gw), F32),
                   jax.ShapeDtypeStruct((m // CMP_BLK, 2 * gw), F32)],
        scratch_shapes=[pltpu.VMEM((2 * G_C, tm, DK_C), F32)],
        compiler_params=_cparams("parallel"),
        name="nsa_prep",
    )(z, cosf, sinf, cmp_pe, cmp_w.astype(BF16))


def _flash_step_t(k, v, q, valid, m_ref, l_ref, acc_ref):
    s = _dot_nt(k, q)
    if valid is not None:
        s = jnp.where(valid, s, NEG)
    m_old = m_ref[...]
    m_new = jnp.maximum(m_old, jnp.max(s, axis=0, keepdims=True))
    alpha = jnp.exp(m_old - m_new)
    p = jnp.exp(s - m_new)
    if valid is not None:
        p = jnp.where(valid, p, 0.0)
    l_ref[...] = alpha * l_ref[...] + jnp.sum(p, axis=0, keepdims=True)
    acc_ref[...] = alpha * acc_ref[...] + lax.dot_general(v, p.astype(BF16), (((0,), (0,)), ((), ())),
                                                           preferred_element_type=F32)
    m_ref[...] = m_new


def _flash_init(m_ref, l_ref, acc_ref):
    m_ref[...] = jnp.full(m_ref.shape, NEG, F32)
    l_ref[...] = jnp.zeros(l_ref.shape, F32)
    acc_ref[...] = jnp.zeros(acc_ref.shape, F32)


def _nsa_kernel(zq_ref, zsm_ref, cos_ref, sin_ref, kcmp_ref, vcmp_ref, ks_ref, vs_ref, kw_ref, vw_ref, e_ref,
                o_ref, m_ref, l_ref, acc_ref, *, tq):
    g = pl.program_id(1)
    t0 = pl.program_id(2) * tq
    scale = DK_C ** -0.5
    n_c = kcmp_ref.shape[0]
    tpos = t0 + lax.broadcasted_iota(jnp.int32, (tq, 1), 0)
    lane = lax.broadcasted_iota(jnp.int32, (tq, n_c), 1)
    cosf, sinf = cos_ref[...], sin_ref[...]
    kc = kcmp_ref[...].astype(BF16)
    vc = vcmp_ref[...].astype(BF16)
    cmask = ((lane + 1) * CMP_BLK - 1) <= tpos
    anyc = tpos >= CMP_BLK - 1
    imp = jnp.zeros((tq, n_c), F32)
    o_cmp = []
    for p in range(HPG_C):
        q = zq_ref[:, p * DK_C:(p + 1) * DK_C].astype(BF16)
        s = jnp.where(cmask, _dot_nt(q, kc) * scale, NEG)
        e = jnp.exp(s - jnp.max(s, axis=-1, keepdims=True))
        pc = jnp.where(anyc, e / jnp.sum(e, axis=-1, keepdims=True), 0.0)
        o_cmp.append(jnp.dot(pc.astype(BF16), vc, preferred_element_type=F32))
        imp = imp + pc
    imp2 = imp + pltpu.roll(imp, n_c - 1, axis=1)
    cur2 = (tpos // SEL_BLK) * 2
    valid = ((lane & 1) == 0) & (lane <= cur2)
    forced = (lane == 0) | (lane == cur2) | (lane == cur2 - 2)
    score = jnp.where(valid, imp2 + jnp.where(forced, FORCE_BONUS, 0.0), -1.0)
    sel = jnp.zeros((tq, n_c), F32)
    lanef = lane.astype(F32)
    for _ in range(N_SEL):
        top = jnp.max(score, axis=-1, keepdims=True)
        idx = jnp.min(jnp.where(score == top, lanef, float(n_c)), axis=-1, keepdims=True)
        pick = lanef == idx
        sel = jnp.where(pick, jnp.where(top > -0.5, 1.0, 0.0), sel)
        score = jnp.where(pick, -3.0, score)
    rq = HPG_C * tq
    sel_t = sel.T.astype(BF16)
    sel_t = jnp.concatenate([sel_t] * HPG_C, axis=1)
    qr = jnp.concatenate([_rot128(zq_ref[:, p * DK_C:(p + 1) * DK_C] * scale, cosf, sinf)
                          for p in range(HPG_C)], axis=0).astype(BF16)
    j = pl.program_id(2)
    krow = lax.broadcasted_iota(jnp.int32, (tq, rq), 0)
    qlane = lax.broadcasted_iota(jnp.int32, (tq, rq), 1) & (tq - 1)
    sm, sl, sacc = m_ref.at[0], l_ref.at[0], acc_ref.at[0]
    wm, wl, wacc = m_ref.at[1], l_ref.at[1], acc_ref.at[1]
    _flash_init(m_ref, l_ref, acc_ref)

    def tile(k_ref, v_ref, jk):
        off = pl.multiple_of(jk * tq, tq)
        return k_ref[pl.ds(off, tq), :].astype(BF16), v_ref[pl.ds(off, tq), :].astype(BF16)

    def sel_body(jk, carry):
        k, v = tile(ks_ref, vs_ref, jk)
        hit = jnp.dot(e_ref[jk], sel_t, preferred_element_type=F32)
        _flash_step_t(k, v, qr, hit > 0.5, sm, sl, sacc)
        return carry

    lax.fori_loop(0, j, sel_body, 0)
    k, v = tile(ks_ref, vs_ref, j)
    hit = jnp.dot(e_ref[j], sel_t, preferred_element_type=F32)
    _flash_step_t(k, v, qr, jnp.where(krow <= qlane, hit, 0.0) > 0.5, sm, sl, sacc)

    def win_body(jk, carry):
        k, v = tile(kw_ref, vw_ref, jk)
        rel = (j - jk) * tq + qlane - krow
        _flash_step_t(k, v, qr, (rel >= 0) & (rel <= WINDOW), wm, wl, wacc)
        return carry

    lax.fori_loop(jnp.maximum(j - (WINDOW + tq - 1) // tq, 0), j + 1, win_body, 0)
    o_sel = (sacc[...] / sl[...]).T
    o_win = (wacc[...] / wl[...]).T
    for p in range(HPG_C):
        def gate(c):
            a = zsm_ref[:, p * 3 + c:p * 3 + c + 1]
            b = zsm_ref[:, (HPG_C + p) * 3 + c:(HPG_C + p) * 3 + c + 1]
            return jax.nn.sigmoid(jnp.where(g == 0, a, b))
        r = slice(p * tq, (p + 1) * tq)
        o_ref[:, p * DV_C:(p + 1) * DV_C] = gate(0) * o_cmp[p] + gate(1) * o_sel[r] + gate(2) * o_win[r]


def _nsa_prompt(z, rows, wrows, cmp, cosf, sinf, b, t):
    tq = NSA_TQ
    nq = t // tq
    n_c = t // CMP_BLK
    assert G_C == 2 and HPG_C == 2 and n_c % 128 == 0 and SEL_BLK == 2 * CMP_BLK and tq % SEL_BLK == 0
    assert tq & (tq - 1) == 0 and t % tq == 0
    sb = np.arange(n_c)[None, None, :]
    key = np.arange(nq)[:, None, None] * tq + np.arange(tq)[None, :, None]
    expand = jnp.asarray((sb % 2 == 0) & (key // SEL_BLK == sb // 2), BF16)
    kv_spec = lambda c: pl.BlockSpec((t, DK_C), lambda i, g, j, c=c: (i, c + g))
    return pl.pallas_call(
        functools.partial(_nsa_kernel, tq=tq),
        grid=(b, G_C, nq),
        in_specs=[pl.BlockSpec((tq, HPG_C * DK_C), lambda i, g, j: (i * nq + j, OZ_Q // (HPG_C * DK_C) + g)),
                  pl.BlockSpec((tq, 128), lambda i, g, j: (i * nq + j, OZ_SM // 128)),
                  pl.BlockSpec((tq, DK_C), lambda i, g, j: (j, 0)),
                  pl.BlockSpec((tq, DK_C), lambda i, g, j: (j, 0)),
                  pl.BlockSpec((n_c, DK_C), lambda i, g, j: (i, g)),
                  pl.BlockSpec((n_c, DK_C), lambda i, g, j: (i, G_C + g)),
                  kv_spec(2 * G_C), kv_spec(3 * G_C), kv_spec(0), kv_spec(G_C),
                  _resident(expand.shape)],
        out_specs=pl.BlockSpec((tq, HPG_C * DV_C), lambda i, g, j: (i * nq + j, g)),
        out_shape=jax.ShapeDtypeStruct((b * t, H_C * DV_C), F32),
        scratch_shapes=[pltpu.VMEM((2, 1, HPG_C * tq), F32), pltpu.VMEM((2, 1, HPG_C * tq), F32),
                        pltpu.VMEM((2, DV_C, HPG_C * tq), F32)],
        compiler_params=_cparams("parallel", "parallel", "arbitrary"),
        name="nsa_prompt",
    )(z, z, cosf, sinf, cmp, cmp, rows, rows, wrows, wrows, expand)


def _split3_terms(x):
    x1 = x.astype(BF16)
    r1 = x - x1.astype(F32)
    x2 = r1.astype(BF16)
    x3 = (r1 - x2.astype(F32)).astype(BF16)
    return x1, x2, x3


def _split3(x):
    return jnp.concatenate(_split3_terms(x), axis=-1)


def _sum3(y, w):
    return y[:, 0:w] + y[:, w:2 * w] + y[:, 2 * w:3 * w]


def _gla_tables(c):
    n_lvl = int(math.log2(c))
    assert 1 << n_lvl == c
    t = np.arange(c)[:, None]
    u = np.arange(c)[None, :]
    blocks = [u <= t, u > t]
    lvl = np.full((c, c), -1, np.int32)
    lvl[np.arange(c), np.arange(c)] = n_lvl
    for l in range(n_lvl):
        m = c >> (l + 1)
        r = (t // (2 * m)) * 2 * m + m - 1
        upper = (t % (2 * m)) >= m
        blocks.append(upper & (u > r) & (u <= t))
        blocks.append(~upper & (u > t) & (u <= r))
        same = (t // (2 * m)) == (u // (2 * m))
        lvl[same & upper & ((u % (2 * m)) < m)] = l
    sel = np.concatenate(blocks, axis=0).astype(np.float32)
    return jnp.asarray(sel, BF16), jnp.asarray(lvl), n_lvl


def _hgrn_kernel(zq_ref, zf_ref, zi_ref, zg_ref, s0_ref, lbraw_ref, ng_ref, sel_ref, lvl_ref, o_ref, s_ref, st_ref,
                 *, layer, c, n_lvl):
    j = pl.program_id(1)

    @pl.when(j == 0)
    def _():
        for h in range(H_A):
            st_ref[h] = s0_ref[0, h].T

    raw = [lbraw_ref[i:i + 1, :] for i in range(N_EVEN)]
    mx = functools.reduce(jnp.maximum, raw)
    ex = [jnp.exp(r - mx) for r in raw]
    den = functools.reduce(lambda a, b: a + b, ex)
    sm = [e / den for e in ex]
    lb_all = functools.reduce(lambda a, b: a + b, sm[:layer + 1]) - sm[0]
    lvl = lvl_ref[...]
    for h in range(H_A):
        lo, hi = h * DK_A, (h + 1) * DK_A
        lb = lb_all[:, lo:hi]
        a = zf_ref[:, lo:hi]
        q = zq_ref[:, lo:hi]
        v = zi_ref[:, lo:hi].astype(BF16)
        g = jnp.log(jnp.maximum(lb + (1.0 - lb) * jax.nn.sigmoid(a), F_FLOOR))
        k = (1.0 - lb) * jax.nn.sigmoid(-a)
        ex = jnp.exp(_sum3(jnp.dot(sel_ref[...], _split3(g), preferred_element_type=F32), DK_A))
        blk = lambda i: ex[i * c:(i + 1) * c]
        st = st_ref[h]
        o = _dot_nt((q * blk(0)).astype(BF16), st.astype(BF16))
        att = jnp.where(lvl == n_lvl, _dot_nt(q.astype(BF16), k.astype(BF16)), 0.0)
        for l in range(n_lvl):
            a_l = _dot_nt((q * blk(2 + 2 * l)).astype(BF16), (k * blk(3 + 2 * l)).astype(BF16))
            att = jnp.where(lvl == l, a_l, att)
        o = o + jnp.dot(att.astype(BF16), v, preferred_element_type=F32)
        ks = (k * blk(1)).astype(BF16)
        st_ref[h] = st * ex[c - 1:c] + lax.dot_general(v, ks, (((0,), (0,)), ((), ())), preferred_element_type=F32)
        o = o * lax.rsqrt(jnp.mean(o * o, axis=-1, keepdims=True) + EPS) * ng_ref[...]
        o_ref[:, lo:hi] = o * jax.nn.silu(zg_ref[:, lo:hi])

    @pl.when(j == pl.num_programs(1) - 1)
    def _():
        for h in range(H_A):
            s_ref[0, h] = st_ref[h].T


def _hgrn(z, s0, lb_raw, norm_g, layer, b, t):
    c = min(t, 128)
    nc = t // c
    sel, lvl, n_lvl = _gla_tables(c)
    w = H_A * DK_A
    col = lambda k: pl.BlockSpec((c, w), lambda i, j, k=k: (i * nc + j, k))
    return pl.pallas_call(
        functools.partial(_hgrn_kernel, layer=layer, c=c, n_lvl=n_lvl),
        grid=(b, nc),
        in_specs=[col(0), col(1), col(2), col(3),
                  pl.BlockSpec((1, H_A, DK_A, DV_A), lambda i, j: (i, 0, 0, 0)),
                  _resident(lb_raw.shape), _resident((1, DV_A)), _resident(sel.shape), _resident(lvl.shape)],
        out_specs=[pl.BlockSpec((c, w), lambda i, j: (i * nc + j, 0)),
                   pl.BlockSpec((1, H_A, DK_A, DV_A), lambda i, j: (i, 0, 0, 0))],
        out_shape=[jax.ShapeDtypeStruct((b * t, w), F32), jax.ShapeDtypeStruct((b, H_A, DK_A, DV_A), F32)],
        scratch_shapes=[pltpu.VMEM((H_A, DV_A, DK_A), F32)],
        compiler_params=_cparams("parallel", "arbitrary"),
        name="hgrn",
    )(z, z, z, z, s0, lb_raw, norm_g.reshape(1, DV_A), sel, lvl)


SM_GATE, SM_I, SM_F = 0, 3 * H_C, 3 * H_C + H_D
HIST = 8


def _log_sigmoid(x):
    return jnp.minimum(x, 0.0) - jnp.log1p(jnp.exp(-jnp.abs(x)))


def _mlstm_kernel(zqk_ref, zv_ref, zo_ref, sm_ref, cbuf_ref, c0_ref, n0_ref, m0_ref, cw_ref, cb_ref, gb_ref, ng_ref,
                  tri_ref, o_ref, conv_ref, c_ref, n_ref, m_ref, xh_ref, st_ref, mm_ref, *, c):
    j = pl.program_id(1)
    last = pl.num_programs(1) - 1
    lane128 = lax.broadcasted_iota(jnp.int32, (DK_D, DV_D), 1)

    @pl.when(j == 0)
    def _():
        xh_ref[HIST - (CONV_W - 1):HIST, :] = cbuf_ref[0]
        mm_ref[...] = m0_ref[0]
        for h in range(H_D):
            st_ref[h, :, 0:DV_D] = c0_ref[0, h]
            ncol = jnp.broadcast_to(n0_ref[0, h:h + 1, :], (DK_D, DK_D)).T
            st_ref[h, :, DV_D:2 * DV_D] = jnp.where(lane128 == 0, ncol, 0.0)

    xh_ref[HIST:HIST + c, :] = zqk_ref[...]
    y = cb_ref[...]
    for jj in range(CONV_W):
        y = y + cw_ref[jj:jj + 1, :] * xh_ref[HIST - (CONV_W - 1) + jj:HIST - (CONV_W - 1) + jj + c, :]
    qk = jax.nn.silu(y)
    tail = xh_ref[HIST + c - (CONV_W - 1):HIST + c, :]
    xh_ref[HIST - (CONV_W - 1):HIST, :] = tail

    pre = sm_ref[...] + gb_ref[...]
    pre_t = pre.T
    lf_c = _log_sigmoid(pre)
    lf_r = _log_sigmoid(pre_t[SM_F:SM_F + 8, :])
    tri = tri_ref[...]
    b_c = _sum3(jnp.dot(tri, _split3(lf_c), preferred_element_type=F32), 128)
    b_r = functools.reduce(lambda x, y: x + y, [_dot_nt(term, tri) for term in _split3_terms(lf_r)])
    row_t = lax.broadcasted_iota(jnp.int32, (c, c), 0)
    col_s = lax.broadcasted_iota(jnp.int32, (c, c), 1)
    causal = col_s <= row_t
    lane = lax.broadcasted_iota(jnp.int32, (1, 128), 1)
    ones_col = jnp.where(lax.broadcasted_iota(jnp.int32, (c, DV_D), 1) == 0, 1.0, 0.0).astype(BF16)
    m_new_row = mm_ref[...]
    for h in range(H_D):
        lo, hi = h * DK_D, (h + 1) * DK_D
        q = qk[:, lo:hi].astype(BF16)
        kf = qk[:, H_D * DK_D + lo:H_D * DK_D + hi] * (DK_D ** -0.5)
        v_aug = jnp.concatenate([zv_ref[:, lo:hi].astype(BF16), ones_col], axis=-1)
        m_prev = mm_ref[:, h:h + 1]
        bc = b_c[:, SM_F + h:SM_F + h + 1]
        a_c = pre[:, SM_I + h:SM_I + h + 1] - bc
        a_r = pre_t[SM_I + h:SM_I + h + 1, :] - b_r[h:h + 1, :]
        cmax = jnp.max(jnp.where(causal, a_r, -jnp.inf), axis=-1, keepdims=True)
        mt = bc + jnp.maximum(m_prev, cmax)
        dprev = jnp.exp(bc + m_prev - mt)
        dm = jnp.exp(jnp.where(causal, a_r + (bc - mt), NEG))
        s = _dot_nt(q, kf.astype(BF16)) * dm
        st = st_ref[h]
        nd = dprev * jnp.dot(q, st.astype(BF16), preferred_element_type=F32) \
            + jnp.dot(s.astype(BF16), v_aug, preferred_element_type=F32)
        den = nd[:, DV_D:DV_D + 1]
        hh = nd[:, 0:DV_D] / jnp.maximum(jnp.abs(den), jnp.exp(-mt))
        b_l = bc[c - 1:c, :]
        m_l = mt[c - 1:c, :]
        w = jnp.exp(a_c + b_l - m_l)
        dl = jnp.exp(b_l + m_prev - m_l)
        st_ref[h] = dl * st + lax.dot_general((w * kf).astype(BF16), v_aug, (((0,), (0,)), ((), ())),
                                              preferred_element_type=F32)
        m_new_row = jnp.where(lane == h, m_l, m_new_row)
        hh = hh * lax.rsqrt(jnp.mean(hh * hh, axis=-1, keepdims=True) + EPS) * ng_ref[...]
        o_ref[:, lo:hi] = hh * jax.nn.sigmoid(zo_ref[:, lo:hi])
    mm_ref[...] = m_new_row

    @pl.when(j == last)
    def _():
        conv_ref[0] = tail
        m_ref[0] = m_new_row
        for h in range(H_D):
            c_ref[0, h] = st_ref[h, :, 0:DV_D]
            n_ref[0, h:h + 1, :] = st_ref[h, :, DV_D:2 * DV_D].T[0:1, :]


def _mlstm(z, conv_buf, c0, n0, m0, conv_w, conv_b, gate_b, norm_g, b, t):
    c = min(t, 128)
    nc = t // c
    tri = jnp.asarray(np.tril(np.ones((c, c), np.float32)), BF16)
    gb = jnp.zeros((1, 128), F32).at[0, SM_I:SM_I + H_D].set(gate_b[0]).at[0, SM_F:SM_F + H_D].set(gate_b[1])
    m0p = jnp.zeros((b, 1, 128), F32).at[:, 0, :H_D].set(m0)
    w = H_D * DV_D
    blk = lambda width, k: pl.BlockSpec((c, width), lambda i, j, k=k: (i * nc + j, k))
    per_b = lambda shape: pl.BlockSpec((1,) + shape, lambda i, j: (i,) + (0,) * len(shape))
    o, conv, cc, nn, mm = pl.pallas_call(
        functools.partial(_mlstm_kernel, c=c),
        grid=(b, nc),
        in_specs=[blk(QK_CH_D, OZ_QK // QK_CH_D), blk(w, OZ_V // w), blk(w, OZ_O // w), blk(128, OZ_SM // 128),
                  per_b((CONV_W - 1, QK_CH_D)), per_b((H_D, DK_D, DV_D)), per_b((H_D, DK_D)), per_b((1, 128)),
                  _resident((CONV_W, QK_CH_D)), _resident((1, QK_CH_D)), _resident((1, 128)), _resident((1, DV_D)),
                  _resident((c, c))],
        out_specs=[pl.BlockSpec((c, w), lambda i, j: (i * nc + j, 0)),
                   per_b((CONV_W - 1, QK_CH_D)), per_b((H_D, DK_D, DV_D)), per_b((H_D, DK_D)), per_b((1, 128))],
        out_shape=[jax.ShapeDtypeStruct((b * t, w), F32), jax.ShapeDtypeStruct((b, CONV_W - 1, QK_CH_D), F32),
                   jax.ShapeDtypeStruct((b, H_D, DK_D, DV_D), F32), jax.ShapeDtypeStruct((b, H_D, DK_D), F32),
                   jax.ShapeDtypeStruct((b, 1, 128), F32)],
        scratch_shapes=[pltpu.VMEM((HIST + c, QK_CH_D), F32), pltpu.VMEM((H_D, DK_D, 2 * DV_D), F32),
                        pltpu.VMEM((1, 128), F32)],
        compiler_params=_cparams("parallel", "arbitrary"),
        name="mlstm",
    )(z, z, z, z, conv_buf, c0, n0, m0p, conv_w, conv_b.reshape(1, QK_CH_D), gb, norm_g.reshape(1, DV_D), tri)
    return o, conv, cc, nn, mm[:, 0, :H_D]


EZ_DQ, EZ_DK, EZ_DV = 4 * H_A * DK_A, 4 * H_A * DK_A + H_B * DV_B, 4 * H_A * DK_A + 2 * H_B * DV_B
DIFF_TQ = 256


def _diff_prep_kernel(zk_ref, zv_ref, cos_ref, sin_ref, kv_ref):
    cosf, sinf = cos_ref[...], sin_ref[...]
    w = H_B * DV_B
    for h in range(H_B):
        kv_ref[:, h * DV_B:(h + 1) * DV_B] = _rot64(zk_ref[:, h * DV_B:(h + 1) * DV_B], cosf, sinf)
    kv_ref[:, w:2 * w] = zv_ref[...]


def _diff_prep(z, cosf, sinf, t):
    m = z.shape[0]
    tm = min(512, t)
    nt = t // tm
    w = H_B * DV_B
    return pl.pallas_call(
        _diff_prep_kernel,
        grid=(m // tm,),
        in_specs=[pl.BlockSpec((tm, w), lambda i: (i, EZ_DK // w)), pl.BlockSpec((tm, w), lambda i: (i, EZ_DV // w)),
                  pl.BlockSpec((tm, DV_B), lambda i: (i % nt, 0)), pl.BlockSpec((tm, DV_B), lambda i: (i % nt, 0))],
        out_specs=pl.BlockSpec((tm, 2 * w), lambda i: (i, 0)),
        out_shape=jax.ShapeDtypeStruct((m, 2 * w), F32),
        compiler_params=_cparams("parallel"),
        name="diff_prep",
    )(z, z, cosf, sinf)


def _diff_lambda(lv_ref, lam_init):
    lv = lv_ref[...]
    return (jnp.exp(jnp.sum(lv[0:1] * lv[1:2], axis=-1, keepdims=True))
            - jnp.exp(jnp.sum(lv[2:3] * lv[3:4], axis=-1, keepdims=True)) + lam_init)


def _diff_queries(zq, cosf, sinf):
    q = _rot64(zq, cosf, sinf)
    lane = lax.broadcasted_iota(jnp.int32, q.shape, 1)
    return jnp.concatenate([jnp.where(lane < DH_B, q, 0.0), jnp.where(lane >= DH_B, q, 0.0)], axis=0).astype(BF16)


def _diff_finish(o1, o2, lam, lam_init, ng):
    o = o1 - lam * o2
    o = o * lax.rsqrt(jnp.mean(o * o, axis=-1, keepdims=True) + EPS) * ng
    return o * (1.0 - lam_init)


def _diff_kernel(zq_ref, cos_ref, sin_ref, k_ref, v_ref, lv_ref, ng_ref, o_ref, m_ref, l_ref, acc_ref,
                 *, tq, lam_init):
    j = pl.program_id(2)
    q2 = _diff_queries(zq_ref[...] * (DH_B ** -0.5), cos_ref[...], sin_ref[...])
    _flash_init(m_ref, l_ref, acc_ref)

    def tile(jk):
        off = pl.multiple_of(jk * tq, tq)
        return k_ref[pl.ds(off, tq), :].astype(BF16), v_ref[pl.ds(off, tq), :].astype(BF16)

    def body(jk, carry):
        k, v = tile(jk)
        _flash_step_t(k, v, q2, None, m_ref, l_ref, acc_ref)
        return carry

    lax.fori_loop(0, j, body, 0)
    krow = lax.broadcasted_iota(jnp.int32, (tq, 2 * tq), 0)
    qlane = lax.broadcasted_iota(jnp.int32, (tq, 2 * tq), 1) & (tq - 1)
    k, v = tile(j)
    _flash_step_t(k, v, q2, krow <= qlane, m_ref, l_ref, acc_ref)
    o_t = acc_ref[...] / l_ref[...]
    lam = _diff_lambda(lv_ref, lam_init)
    o_ref[...] = _diff_finish(o_t[:, 0:tq].T, o_t[:, tq:2 * tq].T, lam, lam_init, ng_ref[...])


def _diff_prompt(z, kv, cosf, sinf, lam_vecs, norm_g, layer_idx, b, t):
    tq = DIFF_TQ
    nq = t // tq
    assert tq & (tq - 1) == 0 and t % tq == 0
    lam_init = 0.8 - 0.6 * math.exp(-0.3 * layer_idx)
    return pl.pallas_call(
        functools.partial(_diff_kernel, tq=tq, lam_init=lam_init),
        grid=(b, H_B, nq),
        in_specs=[pl.BlockSpec((tq, DV_B), lambda i, h, j: (i * nq + j, EZ_DQ // DV_B + h)),
                  pl.BlockSpec((tq, DV_B), lambda i, h, j: (j, 0)),
                  pl.BlockSpec((tq, DV_B), lambda i, h, j: (j, 0)),
                  pl.BlockSpec((t, DV_B), lambda i, h, j: (i, h)),
                  pl.BlockSpec((t, DV_B), lambda i, h, j: (i, H_B + h)),
                  _resident(lam_vecs.shape), _resident((1, DV_B))],
        out_specs=pl.BlockSpec((tq, DV_B), lambda i, h, j: (i * nq + j, h)),
        out_shape=jax.ShapeDtypeStruct((b * t, H_B * DV_B), F32),
        scratch_shapes=[pltpu.VMEM((1, 2 * tq), F32), pltpu.VMEM((1, 2 * tq), F32), pltpu.VMEM((DV_B, 2 * tq), F32)],
        compiler_params=_cparams("parallel", "parallel", "arbitrary"),
        name="diff_prompt",
    )(z, cosf, sinf, kv, kv, lam_vecs, norm_g.reshape(1, DV_B))


DEC_PAGES = 8


def _pool_rows(pool):
    return pool.reshape(pool.shape[0], pool.shape[1], -1, pool.shape[-1])


def _page_spec(pool4, layer, page_of):
    return pl.BlockSpec((1, 1) + pool4.shape[2:], lambda i, s, pt: (layer, page_of(i, s, pt), 0, 0))


def _page_rows(pg, slot, n_slots):
    return pg[0, 0, pl.ds(slot, PAGE_SIZE, stride=n_slots), :]


def _diff_dec_kernel(pt_ref, zq_ref, zk_ref, zv_ref, cos_ref, sin_ref, lv_ref, ng_ref, *rest, tq, n_pg, lam_init):
    pages = rest[:n_pg]
    o_ref, kv_ref, m_ref, l_ref, acc_ref, q2_ref = rest[n_pg:]
    step = pl.program_id(1)
    scale = DH_B ** -0.5
    w = H_B * DV_B
    cosf, sinf = cos_ref[...], sin_ref[...]

    @pl.when(step == 0)
    def _():
        m_ref[...] = jnp.full(m_ref.shape, NEG, F32)
        l_ref[...] = jnp.zeros(l_ref.shape, F32)
        acc_ref[...] = jnp.zeros(acc_ref.shape, F32)
        for h in range(H_B):
            q2_ref[h] = _diff_queries(zq_ref[:, h * DV_B:(h + 1) * DV_B], cosf, sinf)

    for h in range(H_B):
        lo, hi = h * DV_B, (h + 1) * DV_B
        q2 = q2_ref[h]
        sc = jnp.concatenate([_dot_nt(q2, _page_rows(pg, h, 2 * H_B).astype(BF16)) for pg in pages], axis=1) * scale
        m_old = m_ref[h]
        m_new = jnp.maximum(m_old, jnp.max(sc, axis=-1, keepdims=True))
        alpha = jnp.exp(m_old - m_new)
        p = jnp.exp(sc - m_new)
        l_ref[h] = alpha * l_ref[h] + jnp.sum(p, axis=-1, keepdims=True)
        pv = jnp.zeros((2 * tq, DV_B), F32)
        for i, pg in enumerate(pages):
            pv = pv + jnp.dot(p[:, i * PAGE_SIZE:(i + 1) * PAGE_SIZE].astype(BF16),
                              _page_rows(pg, H_B + h, 2 * H_B).astype(BF16),
                              preferred_element_type=F32)
        acc_ref[h] = alpha * acc_ref[h] + pv
        m_ref[h] = m_new

    @pl.when(step == pl.num_programs(1) - 1)
    def _():
        lam = _diff_lambda(lv_ref, lam_init)
        qi = lax.broadcasted_iota(jnp.int32, (2 * tq, tq), 0) & (tq - 1)
        ki = lax.broadcasted_iota(jnp.int32, (2 * tq, tq), 1)
        kv_ref[:, w:2 * w] = zv_ref[...]
        for h in range(H_B):
            lo, hi = h * DV_B, (h + 1) * DV_B
            k_new = _rot64(zk_ref[:, lo:hi], cosf, sinf)
            kv_ref[:, lo:hi] = k_new
            s_new = _dot_nt(q2_ref[h].astype(F32), k_new) * scale
            valid = ki <= qi
            m_old = m_ref[h]
            m_new = jnp.maximum(m_old, jnp.max(jnp.where(valid, s_new, NEG), axis=-1, keepdims=True))
            alpha = jnp.exp(m_old - m_new)
            p = jnp.where(valid, jnp.exp(s_new - m_new), 0.0)
            l = alpha * l_ref[h] + jnp.sum(p, axis=-1, keepdims=True)
            acc = alpha * acc_ref[h] + jnp.dot(p, zv_ref[:, lo:hi], preferred_element_type=F32)
            o_ref[:, lo:hi] = _diff_finish(acc[0:tq] / l[0:tq], acc[tq:2 * tq] / l[tq:2 * tq], lam, lam_init,
                                           ng_ref[...])


def _diff_decode(z, pool, layer, page_table, cosf, sinf, lam_vecs, norm_g, layer_idx, b, t):
    n_pages = page_table.shape[1]
    n_pg = DEC_PAGES
    assert n_pages % n_pg == 0 and t & (t - 1) == 0 and t % 8 == 0
    w = H_B * DV_B
    lam_init = 0.8 - 0.6 * math.exp(-0.3 * layer_idx)
    pool4 = _pool_rows(pool)
    zcol = lambda k: pl.BlockSpec((t, w), lambda i, s, pt, k=k: (i, k))
    const = lambda shape: pl.BlockSpec(shape, lambda i, s, pt: (0,) * len(shape))
    page = lambda p: _page_spec(pool4, layer, lambda i, s, pt, p=p: pt[i, s * n_pg + p])
    grid_spec = pltpu.PrefetchScalarGridSpec(
        num_scalar_prefetch=1,
        grid=(b, n_pages // n_pg),
        in_specs=[zcol(EZ_DQ // w), zcol(EZ_DK // w), zcol(EZ_DV // w), const((t, DV_B)), const((t, DV_B)),
                  const(lam_vecs.shape), const((1, DV_B))] + [page(p) for p in range(n_pg)],
        out_specs=[pl.BlockSpec((t, w), lambda i, s, pt: (i, 0)), pl.BlockSpec((t, 2 * w), lambda i, s, pt: (i, 0))],
        scratch_shapes=[pltpu.VMEM((H_B, 2 * t, 1), F32), pltpu.VMEM((H_B, 2 * t, 1), F32),
                        pltpu.VMEM((H_B, 2 * t, DV_B), F32), pltpu.VMEM((H_B, 2 * t, DV_B), BF16)])
    return pl.pallas_call(
        functools.partial(_diff_dec_kernel, tq=t, n_pg=n_pg, lam_init=lam_init),
        grid_spec=grid_spec,
        out_shape=[jax.ShapeDtypeStruct((b * t, w), F32), jax.ShapeDtypeStruct((b * t, 2 * w), F32)],
        compiler_params=_cparams("parallel", "arbitrary"),
        name="diff_decode",
    )(page_table, z, z, z, cosf, sinf, lam_vecs, norm_g.reshape(1, DV_B), *([pool4] * n_pg))


CMP_PAGES = 16


def _nsa_dec_cmp_kernel(pt_ref, pe_ref, w_ref, *rest, n_pg):
    pages = rest[:n_pg]
    cmp_ref, x_ref = rest[n_pg:]
    for p, pg in enumerate(pages):
        for i in range(2 * G_C):
            x_ref[i, p * PAGE_SIZE:(p + 1) * PAGE_SIZE, :] = _page_rows(pg, i, 4 * G_C)
    _compress_blocks(x_ref, pe_ref, w_ref, cmp_ref, n_pg * PAGE_SIZE // CMP_BLK)


def _nsa_dec_cmp(pool4, layer, page_table, cmp_w, cmp_pe):
    b, n_pages = page_table.shape
    n_pg = CMP_PAGES
    assert n_pages % n_pg == 0 and PAGE_SIZE % CMP_BLK == 0
    gw = G_C * DK_C
    nb = n_pg * PAGE_SIZE // CMP_BLK
    steps = n_pages // n_pg
    const = lambda shape: pl.BlockSpec(shape, lambda i, s, pt: (0,) * len(shape))
    page = lambda p: _page_spec(pool4, layer, lambda i, s, pt, p=p: pt[i, s * n_pg + p])
    grid_spec = pltpu.PrefetchScalarGridSpec(
        num_scalar_prefetch=1,
        grid=(b, steps),
        in_specs=[const(cmp_pe.shape), const(cmp_w.shape)] + [page(p) for p in range(n_pg)],
        out_specs=pl.BlockSpec((nb, 2 * gw), lambda i, s, pt: (i * steps + s, 0)),
        scratch_shapes=[pltpu.VMEM((2 * G_C, n_pg * PAGE_SIZE, DK_C), F32)])
    return pl.pallas_call(
        functools.partial(_nsa_dec_cmp_kernel, n_pg=n_pg),
        grid_spec=grid_spec,
        out_shape=jax.ShapeDtypeStruct((b * steps * nb, 2 * gw), F32),
        compiler_params=_cparams("parallel", "arbitrary"),
        name="nsa_dec_cmp",
    )(page_table, cmp_pe, cmp_w.astype(BF16), *([pool4] * n_pg))


def _softmax_rows(s):
    e = jnp.exp(s - jnp.max(s, axis=-1, keepdims=True))
    return e / jnp.sum(e, axis=-1, keepdims=True)


def _nsa_dec_kernel(pt_ref, zq_ref, zkv_ref, zsm_ref, cos_ref, sin_ref, cmp_ref, win_ref, e_ref, *rest,
                    tq, n_pg, pos0):
    pages = rest[:n_pg]
    o_ref, rows_ref, wout_ref, sel_ref, q_ref, ocmp_ref, m_ref, l_ref, acc_ref = rest[n_pg:]
    step = pl.program_id(1)
    scale = DK_C ** -0.5
    gw = G_C * DK_C
    n_c = cmp_ref.shape[0]
    cosf, sinf = cos_ref[...], sin_ref[...]
    rows2 = HPG_C * tq
    tpos = pos0 + (lax.broadcasted_iota(jnp.int32, (rows2, 1), 0) & (tq - 1))

    def stacked(fn):
        return lambda g: jnp.concatenate([fn(zq_ref[:, (g * HPG_C + p) * DK_C:(g * HPG_C + p + 1) * DK_C])
                                          for p in range(HPG_C)], axis=0)

    @pl.when(step == 0)
    def _():
        m_ref[...] = jnp.full(m_ref.shape, NEG, F32)
        l_ref[...] = jnp.zeros(l_ref.shape, F32)
        acc_ref[...] = jnp.zeros(acc_ref.shape, F32)
        lane = lax.broadcasted_iota(jnp.int32, (rows2, n_c), 1)
        lane_t = lane[0:tq]
        lanef = lane_t.astype(F32)
        for g in range(G_C):
            kc = cmp_ref[:, g * DK_C:(g + 1) * DK_C].astype(BF16)
            vc = cmp_ref[:, gw + g * DK_C:gw + (g + 1) * DK_C].astype(BF16)
            qg = stacked(lambda x: x)(g).astype(BF16)
            cmask = ((lane + 1) * CMP_BLK - 1) <= tpos
            pc = _softmax_rows(jnp.where(cmask, _dot_nt(qg, kc) * scale, NEG))
            pc = jnp.where(tpos >= CMP_BLK - 1, pc, 0.0)
            ocmp_ref[g] = jnp.dot(pc.astype(BF16), vc, preferred_element_type=F32)
            imp = functools.reduce(lambda a, b: a + b, [pc[p * tq:(p + 1) * tq] for p in range(HPG_C)])
            imp2 = jnp.concatenate([imp[:, c0:c0 + 128] + pltpu.roll(imp[:, c0:c0 + 128], 127, axis=1)
                                    for c0 in range(0, n_c, 128)], axis=1)
            cur2 = (tpos[0:tq] // SEL_BLK) * 2
            valid = ((lane_t & 1) == 0) & (lane_t <= cur2)
            forced = (lane_t == 0) | (lane_t == cur2) | (lane_t == cur2 - 2)
            score = jnp.where(valid, imp2 + jnp.where(forced, FORCE_BONUS, 0.0), -1.0)
            sel = jnp.zeros((tq, n_c), F32)
            for _ in range(N_SEL - 1):
                top = jnp.max(score, axis=-1, keepdims=True)
                idx = jnp.min(jnp.where(score == top, lanef, float(n_c)), axis=-1, keepdims=True)
                pick = lanef == idx
                sel = jnp.where(pick, jnp.where(top > -0.5, 1.0, 0.0), sel)
                score = jnp.where(pick, -3.0, score)
            sel_ref[g] = jnp.concatenate([sel] * HPG_C, axis=0).astype(BF16)
            q_ref[g] = stacked(lambda x: _rot128(x, cosf, sinf))(g).astype(BF16)

    for g in range(G_C):
        q = q_ref[g]
        selb = sel_ref[g]
        sc = jnp.concatenate([_dot_nt(q, _page_rows(pg, 2 * G_C + g, 4 * G_C).astype(BF16)) for pg in pages],
                             axis=1) * scale
        hit = jnp.concatenate([jnp.dot(selb, e_ref[p], preferred_element_type=F32) for p in range(n_pg)], axis=1)
        ok = hit > 0.5
        m_old = m_ref[g]
        m_new = jnp.maximum(m_old, jnp.max(jnp.where(ok, sc, NEG), axis=-1, keepdims=True))
        alpha = jnp.exp(m_old - m_new)
        p_ = jnp.where(ok, jnp.exp(sc - m_new), 0.0)
        l_ref[g] = alpha * l_ref[g] + jnp.sum(p_, axis=-1, keepdims=True)
        pv = jnp.zeros((rows2, DV_C), F32)
        for i, pg in enumerate(pages):
            pv = pv + jnp.dot(p_[:, i * PAGE_SIZE:(i + 1) * PAGE_SIZE].astype(BF16),
                              _page_rows(pg, 3 * G_C + g, 4 * G_C).astype(BF16), preferred_element_type=F32)
        acc_ref[g] = alpha * acc_ref[g] + pv
        m_ref[g] = m_new

    @pl.when(step == pl.num_programs(1) - 1)
    def _():
        qi = lax.broadcasted_iota(jnp.int32, (rows2, tq), 0) & (tq - 1)
        ki = lax.broadcasted_iota(jnp.int32, (rows2, tq), 1)
        causal = ki <= qi
        wj = lax.broadcasted_iota(jnp.int32, (rows2, WINDOW), 1)
        wq = lax.broadcasted_iota(jnp.int32, (rows2, WINDOW), 0) & (tq - 1)
        in_win = wj >= wq
        rows_ref[:, 0:2 * gw] = zkv_ref[:, 0:2 * gw]
        rows_ref[:, 3 * gw:4 * gw] = zkv_ref[:, 3 * gw:4 * gw]
        wout_ref[0, 0:WINDOW - tq, :] = win_ref[0, tq:WINDOW, :]
        wout_ref[0, WINDOW - tq:WINDOW, gw:2 * gw] = zkv_ref[:, 5 * gw:6 * gw]
        for g in range(G_C):
            lo = g * DK_C
            q = q_ref[g].astype(F32)
            ks_new = _rot128(zkv_ref[:, 2 * gw + lo:2 * gw + lo + DK_C], cosf, sinf)
            kw_new = _rot128(zkv_ref[:, 4 * gw + lo:4 * gw + lo + DK_C], cosf, sinf)
            vs_new = zkv_ref[:, 3 * gw + lo:3 * gw + lo + DK_C]
            vw_new = zkv_ref[:, 5 * gw + lo:5 * gw + lo + DK_C]
            rows_ref[:, 2 * gw + lo:2 * gw + lo + DK_C] = ks_new
            wout_ref[0, WINDOW - tq:WINDOW, lo:lo + DK_C] = kw_new
            s_new = _dot_nt(q, ks_new) * scale
            m_old = m_ref[g]
            m_new = jnp.maximum(m_old, jnp.max(jnp.where(causal, s_new, NEG), axis=-1, keepdims=True))
            alpha = jnp.exp(m_old - m_new)
            p_ = jnp.where(causal, jnp.exp(s_new - m_new), 0.0)
            l_sel = alpha * l_ref[g] + jnp.sum(p_, axis=-1, keepdims=True)
            o_sel = (alpha * acc_ref[g] + jnp.dot(p_, vs_new, preferred_element_type=F32)) / l_sel
            kw = win_ref[0, :, lo:lo + DK_C].astype(BF16)
            vw = win_ref[0, :, gw + lo:gw + lo + DK_C].astype(BF16)
            s_buf = jnp.where(in_win, _dot_nt(q_ref[g], kw) * scale, NEG)
            s_own = jnp.where(causal, _dot_nt(q, kw_new) * scale, NEG)
            m_w = jnp.maximum(jnp.max(s_buf, axis=-1, keepdims=True), jnp.max(s_own, axis=-1, keepdims=True))
            p_buf = jnp.where(in_win, jnp.exp(s_buf - m_w), 0.0)
            p_own = jnp.where(causal, jnp.exp(s_own - m_w), 0.0)
            l_w = jnp.sum(p_buf, axis=-1, keepdims=True) + jnp.sum(p_own, axis=-1, keepdims=True)
            o_win = (jnp.dot(p_buf.astype(BF16), vw, preferred_element_type=F32)
                     + jnp.dot(p_own, vw_new, preferred_element_type=F32)) / l_w
            o_cmp = ocmp_ref[g]
            for p in range(HPG_C):
                h = g * HPG_C + p
                gate = lambda c: jax.nn.sigmoid(zsm_ref[:, SM_GATE + h * 3 + c:SM_GATE + h * 3 + c + 1])
                r = slice(p * tq, (p + 1) * tq)
                o_ref[:, h * DV_C:(h + 1) * DV_C] = gate(0) * o_cmp[r] + gate(1) * o_sel[r] + gate(2) * o_win[r]


def _nsa_decode(z, pool, layer, page_table, win_buf, cosf, sinf, cmp_w, cmp_pe, pos0, b, t):
    n_pages = page_table.shape[1]
    n_pg = DEC_PAGES
    gw = G_C * DK_C
    n_c = n_pages * PAGE_SIZE // CMP_BLK
    assert pos0 == n_pages * PAGE_SIZE and pos0 % SEL_BLK == 0 and t <= SEL_BLK and t % 8 == 0 and t & (t - 1) == 0
    assert n_pages % n_pg == 0 and n_c % 128 == 0 and N_SEL >= 3 and win_buf.shape[1] == WINDOW and G_C == 2
    pool4 = _pool_rows(pool)
    cmp = _nsa_dec_cmp(pool4, layer, page_table, cmp_w, cmp_pe)
    win3 = win_buf.reshape(b, WINDOW, 2 * gw)
    sb = np.arange(n_c)[None, :, None]
    key = np.arange(n_pages)[:, None, None] * PAGE_SIZE + np.arange(PAGE_SIZE)[None, None, :]
    expand = jnp.asarray((sb % 2 == 0) & (key // SEL_BLK == sb // 2), BF16)
    zblk = lambda width, k: pl.BlockSpec((t, width), lambda i, s, pt, k=k: (i, k))
    const = lambda shape: pl.BlockSpec(shape, lambda i, s, pt: (0,) * len(shape))
    page = lambda p: _page_spec(pool4, layer, lambda i, s, pt, p=p: pt[i, s * n_pg + p])
    rows2 = HPG_C * t
    grid_spec = pltpu.PrefetchScalarGridSpec(
        num_scalar_prefetch=1,
        grid=(b, n_pages // n_pg),
        in_specs=[zblk(H_C * DK_C, OZ_Q // (H_C * DK_C)), zblk(6 * gw, OZ_KV // (6 * gw)), zblk(128, OZ_SM // 128),
                  const((t, DK_C)), const((t, DK_C)),
                  pl.BlockSpec((n_c, 2 * gw), lambda i, s, pt: (i, 0)),
                  pl.BlockSpec((1, WINDOW, 2 * gw), lambda i, s, pt: (i, 0, 0)),
                  pl.BlockSpec((n_pg, n_c, PAGE_SIZE), lambda i, s, pt: (s, 0, 0))] + [page(p) for p in range(n_pg)],
        out_specs=[pl.BlockSpec((t, H_C * DV_C), lambda i, s, pt: (i, 0)),
                   pl.BlockSpec((t, 4 * gw), lambda i, s, pt: (i, 0)),
                   pl.BlockSpec((1, WINDOW, 2 * gw), lambda i, s, pt: (i, 0, 0))],
        scratch_shapes=[pltpu.VMEM((G_C, rows2, n_c), BF16), pltpu.VMEM((G_C, rows2, DK_C), BF16),
                        pltpu.VMEM((G_C, rows2, DV_C), F32), pltpu.VMEM((G_C, rows2, 1), F32),
                        pltpu.VMEM((G_C, rows2, 1), F32), pltpu.VMEM((G_C, rows2, DV_C), F32)])
    o, rows, wout = pl.pallas_call(
        functools.partial(_nsa_dec_kernel, tq=t, n_pg=n_pg, pos0=pos0),
        grid_spec=grid_spec,
        out_shape=[jax.ShapeDtypeStruct((b * t, H_C * DV_C), F32), jax.ShapeDtypeStruct((b * t, 4 * gw), F32),
                   jax.ShapeDtypeStruct((b, WINDOW, 2 * gw), F32)],
        compiler_params=_cparams("parallel", "arbitrary"),
        name="nsa_decode",
    )(page_table, z, z, z, cosf, sinf, cmp, win3, expand, *([pool4] * n_pg))
    return o, rows, wout


def _j_rmsnorm(x, g):
    xf = x.astype(jnp.float32)
    y = xf * lax.rsqrt(jnp.mean(xf * xf, axis=-1, keepdims=True) + EPS)
    return (y * g.astype(jnp.float32)).astype(x.dtype)


def _j_split(z, sizes):
    cuts = [int(c) for c in np.cumsum(sizes)[:-1]]
    return jnp.split(z, cuts, axis=-1)


def _j_rope(x, pos):
    half = x.shape[-1] // 2
    inv = ROPE_THETA ** (-jnp.arange(half, dtype=jnp.float32) / half)
    ang = pos.astype(jnp.float32)[:, None] * inv[None, :]
    cos = jnp.cos(ang)[None, :, None, :]
    sin = jnp.sin(ang)[None, :, None, :]
    xf = x.astype(jnp.float32)
    x1, x2 = xf[..., :half], xf[..., half:]
    return jnp.concatenate([x1 * cos - x2 * sin, x2 * cos + x1 * sin], axis=-1).astype(x.dtype)


def _j_qblock(T, pref):
    return pref if T % pref == 0 else T


def _j_blocked(fn, qb, *xs):
    B, T = xs[0].shape[:2]
    nb = T // qb
    def split(a):
        return jnp.moveaxis(a.reshape((B, nb, qb) + a.shape[2:]), 1, 0)
    out = lax.map(lambda args: fn(args[0], *args[1:]), (jnp.arange(nb),) + tuple(split(a) for a in xs))
    out = jnp.moveaxis(out, 0, 1)
    return out.reshape((B, T) + out.shape[3:])


def _j_gather_pages(pool, page_table):
    g = pool[page_table]
    return g.reshape((g.shape[0], g.shape[1] * g.shape[2]) + g.shape[3:])


def _j_chunked(a, c, fill):
    B, T = a.shape[:2]
    Tp = -(-T // c) * c
    a = jnp.pad(a, [(0, 0), (0, Tp - T)] + [(0, 0)] * (a.ndim - 2), constant_values=fill)
    a = a.reshape((B, Tp // c, c) + a.shape[2:])
    return jnp.transpose(a, (1, 0, 3, 2) + tuple(range(4, a.ndim)))


def _j_unchunk(o, T):
    nc, B, H, c, D = o.shape
    return jnp.transpose(o, (1, 0, 3, 2, 4)).reshape(B, nc * c, H, D)[:, :T]


def _j_hgrn_lower_bounds(lb_raw):
    sm = jax.nn.softmax(lb_raw.astype(jnp.float32), axis=0)
    return jnp.cumsum(sm, axis=0) - sm[0:1]


def _j_gla_scan(q, k, v, log_f, S0):
    B, T = q.shape[:2]
    c = HGRN_CHUNK if T >= HGRN_CHUNK else T
    tri = jnp.tril(jnp.ones((c, c), dtype=bool))[:, :, None]
    def step(S, xs):
        qc, kc, vc, gc = xs
        b = jnp.cumsum(gc, axis=2)
        o = jnp.einsum('bhtd,bhde->bhte', qc * jnp.exp(b), S)
        diff = b[:, :, :, None, :] - b[:, :, None, :, :]
        decay = jnp.where(tri, jnp.exp(jnp.where(tri, diff, 0.0)), 0.0)
        att = jnp.einsum('bhtd,bhtsd,bhsd->bhts', qc, decay, kc)
        o = o + jnp.einsum('bhts,bhse->bhte', att, vc)
        bl = b[:, :, -1:, :]
        S = jnp.exp(bl[:, :, 0, :])[..., None] * S + jnp.einsum('bhsd,bhse->bhde', kc * jnp.exp(bl - b), vc)
        return S, o
    xs = tuple(_j_chunked(a, c, 0.0) for a in (q, k, v, log_f))
    S, o = lax.scan(step, S0, xs)
    return _j_unchunk(o, T), S


def _j_hgrn2_mixer(zq, zf, zi, zg, lb, norm_g, S0):
    B, T, _ = zq.shape
    f32 = jnp.float32
    q = zq.reshape(B, T, H_A, DK_A).astype(f32)
    a = zf.reshape(B, T, H_A, DK_A).astype(f32)
    lbh = lb.reshape(H_A, DK_A)
    f = lbh + (1.0 - lbh) * jax.nn.sigmoid(a)
    log_f = jnp.log(jnp.maximum(f, F_FLOOR))
    k = (1.0 - lbh) * jax.nn.sigmoid(-a)
    v = zi.reshape(B, T, H_A, DV_A).astype(f32)
    o, S = _j_gla_scan(q, k, v, log_f, S0.astype(f32))
    o = _j_rmsnorm(o, norm_g) * jax.nn.silu(zg.reshape(B, T, H_A, DV_A).astype(f32))
    return o.reshape(B, T, H_A * DV_A).astype(zq.dtype), S


def _j_diff_mixer(zq, zk, zv, pos, past_kv, lam_vecs, norm_g, layer_idx):
    B, T, _ = zq.shape
    f32 = jnp.float32
    def rot(a):
        a = a.reshape(B, T, H_B, DV_B)
        return jnp.concatenate([_j_rope(a[..., :DH_B], pos), _j_rope(a[..., DH_B:], pos)], axis=-1)
    q, k = rot(zq), rot(zk)
    v = zv.reshape(B, T, H_B, DV_B)
    new_kv = jnp.stack([k, v], axis=2)
    if past_kv is None:
        k_all, v_all, n_prev = k, v, 0
    else:
        k_all = jnp.concatenate([past_kv[:, :, 0], k], axis=1)
        v_all = jnp.concatenate([past_kv[:, :, 1], v], axis=1)
        n_prev = past_kv.shape[1]
    lam_init = 0.8 - 0.6 * math.exp(-0.3 * layer_idx)
    lv = lam_vecs.astype(f32)
    lam = jnp.exp(jnp.sum(lv[0] * lv[1])) - jnp.exp(jnp.sum(lv[2] * lv[3])) + lam_init
    L = k_all.shape[1]
    k1, k2 = k_all[..., :DH_B], k_all[..., DH_B:]
    kidx = jnp.arange(L)
    qb = _j_qblock(T, QBLK)
    scale = DH_B ** -0.5
    def blk(j, q_b):
        qidx = n_prev + j * qb + jnp.arange(qb)
        mask = kidx[None, :] <= qidx[:, None]
        s1 = jnp.einsum('bqhd,bkhd->bhqk', q_b[..., :DH_B], k1).astype(f32) * scale
        s2 = jnp.einsum('bqhd,bkhd->bhqk', q_b[..., DH_B:], k2).astype(f32) * scale
        p = jax.nn.softmax(jnp.where(mask, s1, NEG), axis=-1) - lam * jax.nn.softmax(jnp.where(mask, s2, NEG), axis=-1)
        return jnp.einsum('bhqk,bkhe->bqhe', p.astype(v_all.dtype), v_all)
    o = _j_blocked(blk, qb, q)
    o = _j_rmsnorm(o, norm_g).astype(f32) * (1.0 - lam_init)
    return o.reshape(B, T, H_B * DV_B).astype(zq.dtype), new_kv


def _j_nsa_attend(q, rows, wrows, n_prev, n_prev_w, pos, gates, cmp_w, cmp_pe):
    B, T = q.shape[:2]
    L = rows.shape[1]
    f32 = jnp.float32
    scale = DK_C ** -0.5
    qg = q.reshape(B, T, G_C, HPG_C, DK_C)
    qr = _j_rope(q, pos).reshape(B, T, G_C, HPG_C, DK_C)
    kc, vc, ks, vs = rows[:, :, 0], rows[:, :, 1], rows[:, :, 2], rows[:, :, 3]
    n_c = L // CMP_BLK
    def compress(a, w, pe):
        blk = a[:, :n_c * CMP_BLK].reshape(B, n_c, CMP_BLK, G_C, DK_C) + pe[None, None, :, None, :]
        return jnp.einsum('bnlgd,lde->bnge', blk, w)
    k_cmp = compress(kc, cmp_w[0], cmp_pe[0])
    v_cmp = compress(vc, cmp_w[1], cmp_pe[1])
    s = jnp.einsum('btgpd,bngd->bgptn', qg, k_cmp).astype(f32) * scale
    cmask = ((jnp.arange(n_c) + 1) * CMP_BLK - 1)[None, :] <= pos[:, None]
    p_cmp = jax.nn.softmax(jnp.where(cmask, s, NEG), axis=-1) * jnp.any(cmask, axis=-1)[:, None]
    o_cmp = jnp.einsum('bgptn,bnge->btgpe', p_cmp.astype(v_cmp.dtype), v_cmp)
    R = SEL_BLK // CMP_BLK
    n_s = -(-L // SEL_BLK)
    imp = p_cmp.sum(axis=2)
    imp = jnp.pad(imp, ((0, 0), (0, 0), (0, 0), (0, n_s * R - n_c))).reshape(B, G_C, T, n_s, R).sum(-1)
    sb = jnp.arange(n_s)[None, :]
    cur = (pos // SEL_BLK)[:, None]
    valid = sb <= cur
    forced = (sb == 0) | (sb == cur) | (sb == cur - 1)
    score = jnp.where(valid, imp + FORCE_BONUS * forced.astype(f32), -1.0)
    n_top = min(N_SEL, n_s)
    top_v, top_i = lax.top_k(score, n_top)
    top_ok = jnp.transpose(top_v > -0.5, (0, 2, 1, 3))
    top_i = jnp.transpose(top_i, (0, 2, 1, 3))
    pad_s = n_s * SEL_BLK - L
    def blockify(a):
        a = jnp.pad(a, ((0, 0), (0, pad_s), (0, 0), (0, 0)))
        return jnp.transpose(a.reshape(B, n_s, SEL_BLK, G_C, DK_C), (0, 3, 1, 2, 4))
    ks_b, vs_b = blockify(ks), blockify(vs)
    gather = jax.vmap(jax.vmap(lambda a, i: a[i]))
    qs = _j_qblock(T, SEL_QBLK)
    def sel_fn(j, q_b, i_b, ok_b):
        nq = q_b.shape[1]
        idx = jnp.transpose(i_b, (0, 2, 1, 3))
        ok = jnp.transpose(ok_b, (0, 2, 1, 3))
        flat = idx.reshape(B, G_C, nq * n_top)
        kg = gather(ks_b, flat).reshape(B, G_C, nq, n_top * SEL_BLK, DK_C)
        vg = gather(vs_b, flat).reshape(B, G_C, nq, n_top * SEL_BLK, DV_C)
        kpos = (idx[..., None] * SEL_BLK + jnp.arange(SEL_BLK)).reshape(B, G_C, nq, n_top * SEL_BLK)
        tq = n_prev + j * qs + jnp.arange(nq)
        m = jnp.repeat(ok, SEL_BLK, axis=-1) & (kpos <= tq[None, None, :, None])
        sc = jnp.einsum('bqgpd,bgqkd->bgpqk', q_b, kg).astype(f32) * scale
        p = jax.nn.softmax(jnp.where(m[:, :, None], sc, NEG), axis=-1)
        return jnp.einsum('bgpqk,bgqke->bqgpe', p.astype(vg.dtype), vg)
    o_sel = _j_blocked(sel_fn, qs, qr, top_i, top_ok)
    padw = WINDOW - n_prev_w
    kwp = jnp.pad(wrows[:, :, 0], ((0, 0), (padw, 0), (0, 0), (0, 0)))
    vwp = jnp.pad(wrows[:, :, 1], ((0, 0), (padw, 0), (0, 0), (0, 0)))
    qw = _j_qblock(T, QBLK)
    def win_fn(j, q_b):
        start = j * qw
        kb = lax.dynamic_slice_in_dim(kwp, start, qw + WINDOW, axis=1)
        vb = lax.dynamic_slice_in_dim(vwp, start, qw + WINDOW, axis=1)
        pl_ = jnp.arange(qw + WINDOW)
        rel = jnp.arange(qw)[:, None] + WINDOW - pl_[None, :]
        m = (rel >= 0) & (rel <= WINDOW) & ((start + pl_) >= padw)[None, :]
        sc = jnp.einsum('bqgpd,bkgd->bgpqk', q_b, kb).astype(f32) * scale
        p = jax.nn.softmax(jnp.where(m, sc, NEG), axis=-1)
        return jnp.einsum('bgpqk,bkge->bqgpe', p.astype(vb.dtype), vb)
    o_win = _j_blocked(win_fn, qw, qr)
    g = gates.reshape(B, T, G_C, HPG_C, 3)
    o = g[..., 0:1] * o_cmp + g[..., 1:2] * o_sel + g[..., 2:3] * o_win
    return o.reshape(B, T, H_C * DV_C).astype(q.dtype)


def _j_nsa_mixer(zq, zkc, zvc, zks, zvs, zkw, zvw, zg, pos, past_rows, win_buf, cmp_w, cmp_pe):
    B, T, _ = zq.shape
    heads = lambda z: z.reshape(B, T, G_C, DK_C)
    q = zq.reshape(B, T, H_C, DK_C)
    rows = jnp.stack([heads(zkc), heads(zvc), _j_rope(heads(zks), pos), heads(zvs)], axis=2)
    wrows = jnp.stack([_j_rope(heads(zkw), pos), heads(zvw)], axis=2)
    if past_rows is None:
        all_rows, n_prev = rows, 0
    else:
        all_rows, n_prev = jnp.concatenate([past_rows, rows], axis=1), past_rows.shape[1]
    if win_buf is None:
        all_w, n_prev_w = wrows, 0
    else:
        all_w, n_prev_w = jnp.concatenate([win_buf.astype(wrows.dtype), wrows], axis=1), win_buf.shape[1]
    new_win = all_w[:, -min(WINDOW, n_prev_w + T):]
    gates = jax.nn.sigmoid(zg.reshape(B, T, H_C, 3).astype(jnp.float32))
    o = _j_nsa_attend(q, all_rows, all_w, n_prev, n_prev_w, pos, gates, cmp_w, cmp_pe)
    return o, rows, new_win


def _j_mlstm_scan(q, k, v, ig, lf, C0, n0, m0):
    B, T = q.shape[:2]
    c = MLSTM_CHUNK if T >= MLSTM_CHUNK else T
    tri = jnp.tril(jnp.ones((c, c), dtype=bool))
    def step(carry, xs):
        C, n, m = carry
        qc, kc, vc, ic, fc = xs
        b = jnp.cumsum(fc, axis=-1)
        a = ic - b
        mt = b + jnp.maximum(m[..., None], lax.cummax(a, axis=2))
        dprev = jnp.exp(b + m[..., None] - mt)
        Dm = jnp.exp(jnp.where(tri, a[:, :, None, :] + b[:, :, :, None] - mt[:, :, :, None], NEG))
        s = jnp.einsum('bhtd,bhsd->bhts', qc, kc) * Dm
        num = dprev[..., None] * jnp.einsum('bhtd,bhde->bhte', qc, C) + jnp.einsum('bhts,bhse->bhte', s, vc)
        den = dprev * jnp.einsum('bhtd,bhd->bht', qc, n) + s.sum(-1)
        h = num / jnp.maximum(jnp.abs(den), jnp.exp(-mt))[..., None]
        mL = mt[..., -1]
        w = jnp.exp(a + b[..., -1:] - mL[..., None])
        dl = jnp.exp(b[..., -1] + m - mL)
        C = dl[..., None, None] * C + jnp.einsum('bhs,bhsd,bhse->bhde', w, kc, vc)
        n = dl[..., None] * n + jnp.einsum('bhs,bhsd->bhd', w, kc)
        return (C, n, mL), h
    xs = (_j_chunked(q, c, 0.0), _j_chunked(k, c, 0.0), _j_chunked(v, c, 0.0), _j_chunked(ig, c, NEG), _j_chunked(lf, c, 0.0))
    (C, n, m), h = lax.scan(step, (C0, n0, m0), xs)
    return _j_unchunk(h, T), C, n, m


def _j_mlstm_mixer(zqk, zv, zi, zf, zo, conv_w, conv_b, gate_b, norm_g, conv_buf, C0, n0, m0):
    B, T, _ = zqk.shape
    f32 = jnp.float32
    xp = jnp.concatenate([conv_buf.astype(zqk.dtype), zqk], axis=1)
    y = conv_b
    for j in range(CONV_W):
        y = y + conv_w[j] * xp[:, j:j + T]
    qk = jax.nn.silu(y)
    q = qk[..., :H_D * DK_D].reshape(B, T, H_D, DK_D).astype(f32)
    k = qk[..., H_D * DK_D:].reshape(B, T, H_D, DK_D).astype(f32) * (DK_D ** -0.5)
    v = zv.reshape(B, T, H_D, DV_D).astype(f32)
    ig = zi.astype(f32) + gate_b[0].astype(f32)
    lf = jax.nn.log_sigmoid(zf.astype(f32) + gate_b[1].astype(f32))
    h, C, n, m = _j_mlstm_scan(q, k, v, ig, lf, C0.astype(f32), n0.astype(f32), m0.astype(f32))
    h = _j_rmsnorm(h, norm_g) * jax.nn.sigmoid(zo.reshape(B, T, H_D, DV_D).astype(f32))
    return h.reshape(B, T, H_D * DV_D).astype(zqk.dtype), xp[:, -(CONV_W - 1):], C, n, m


def _run_group(x, pos0, mem_kv, past, W):
    B, T, D = x.shape
    pos = pos0 + jnp.arange(T, dtype=jnp.int32)
    cos32, sin32 = _rope_tables(pos, DH_B // 2, DV_B)
    cos64, sin64 = _rope_tables(pos, DK_C // 2, DK_C)
    new = {name: [] for name in ('diff_kv', 'hgrn', 'nsa_kv', 'nsa_win', 'm_C', 'm_n', 'm_m', 'm_conv')}
    x = x.reshape(B * T, D)
    for l in range(DEPTH):
        g = W['norm_w'][l]
        x = _ffn(x, g[0], W['ffn_w_gate'][l, 0], W['ffn_w_up'][l, 0], W['ffn_w_down'][l, 0])
        i = l // 2
        if l % 2 == 0:
            z = _normmm(x, g[1], W['w_in_even'][i])
            S0 = jnp.zeros((B, H_A, DK_A, DV_A), F32) if past is None else past['hgrn'][i]
            oa, S = _hgrn(z, S0, W['hgrn_lb_raw'], W['hgrn_norm'][i], i, B, T)
            if past is None:
                kv_new = _diff_prep(z, cos32, sin32, T)
                ob = _diff_prompt(z, kv_new, cos32, sin32, W['diff_lam'][i], W['diff_norm'][i], l, B, T)
                kv_new = kv_new.reshape(B, T, 2, H_B, DV_B)
            else:
                ob, kv_new = _diff_decode(z, past['diff_pool'], i, past['page_table'], cos32, sin32,
                                          W['diff_lam'][i], W['diff_norm'][i], l, B, T)
                kv_new = kv_new.reshape(B, T, 2, H_B, DV_B)
            w_out = W['w_out_even'][i]
            new['hgrn'].append(S)
            new['diff_kv'].append(kv_new)
        else:
            z = _normmm(x, g[1], W['w_in_odd'][i])
            if past is None:
                conv_buf = jnp.zeros((B, CONV_W - 1, QK_CH_D), F32)
                C0 = jnp.zeros((B, H_D, DK_D, DV_D), F32)
                n0 = jnp.zeros((B, H_D, DK_D), F32)
                m0 = jnp.zeros((B, H_D), F32)
                rows, wrows, cmp = _nsa_prep(z, cos64, sin64, W['nsa_cmp_w'][i], W['nsa_cmp_pe'][i], T)
                oa = _nsa_prompt(z, rows, wrows, cmp, cos64, sin64, B, T)
                rows = rows.reshape(B, T, 4, G_C, DK_C)
                win_new = wrows.reshape(B, T, 2, G_C, DK_C)[:, -min(WINDOW, T):]
            else:
                conv_buf, C0, n0, m0 = past['m_conv'][i], past['m_C'][i], past['m_n'][i], past['m_m'][i]
                oa, rows, win_new = _nsa_decode(z, past['nsa_pool'], i, past['page_table'], past['nsa_win'][i],
                                                cos64, sin64, W['nsa_cmp_w'][i], W['nsa_cmp_pe'][i], pos0, B, T)
                rows = rows.reshape(B, T, 4, G_C, DK_C)
                win_new = win_new.reshape(B, WINDOW, 2, G_C, DK_C)
            ob, conv_new, C, n, m = _mlstm(z, conv_buf, C0, n0, m0, W['mlstm_conv_w'][i], W['mlstm_conv_b'][i],
                                           W['mlstm_gate_b'][i], W['mlstm_norm'][i], B, T)
            w_out = W['w_out_odd'][i]
            new['nsa_kv'].append(rows)
            new['nsa_win'].append(win_new)
            new['m_C'].append(C)
            new['m_n'].append(n)
            new['m_m'].append(m)
            new['m_conv'].append(conv_new)
        x = _mmres(x, [oa.reshape(B * T, GROUP_W), ob.reshape(B * T, GROUP_W)], [w_out[:GROUP_W], w_out[GROUP_W:]])
        q = _normmm(x, g[2], W['mem_wq'][l])
        a = _xattn(q, mem_kv[l], T)
        x = _mmres(x, [a], [W['mem_wo'][l]])
        x = _ffn(x, g[3], W['ffn_w_gate'][l, 1], W['ffn_w_up'][l, 1], W['ffn_w_down'][l, 1])
    return _rmsnorm_rows(x, W['norm_final']).reshape(B, T, D), new


def kernel(x_prompt, x_sample, mem_prompt, cache_diff_kv, cache_nsa_kv, state_nsa_win, state_hgrn, state_mlstm_C, state_mlstm_n, state_mlstm_m, state_mlstm_conv, cache_mem_kv, page_table, norm_w, norm_final, ffn_w_gate, ffn_w_up, ffn_w_down, w_in_even, w_out_even, w_in_odd, w_out_odd, hgrn_lb_raw, hgrn_norm, diff_lam, diff_norm, nsa_cmp_w, nsa_cmp_pe, mlstm_conv_w, mlstm_conv_b, mlstm_gate_b, mlstm_norm, mem_wq, mem_wkv, mem_wo):
    bf = lambda w: w.astype(BF16)
    o_q, o_kv, o_g, o_qk, o_v, o_i, o_f, o_o = (int(c) for c in np.cumsum((0,) + ODD_SPLITS)[[0, 1, 7, 8, 9, 10, 11, 12]])
    w_in_odd = jnp.concatenate(
        [w_in_odd[..., o_kv:o_g], w_in_odd[..., o_q:o_kv], w_in_odd[..., o_qk:o_v], w_in_odd[..., o_v:o_i],
         w_in_odd[..., o_o:], w_in_odd[..., o_g:o_qk], w_in_odd[..., o_i:o_o],
         jnp.zeros(w_in_odd.shape[:2] + (OZ_WIDTH - IN_ODD,), w_in_odd.dtype)], axis=-1)
    W = {'norm_w': norm_w, 'norm_final': norm_final, 'ffn_w_gate': bf(ffn_w_gate), 'ffn_w_up': bf(ffn_w_up),
         'ffn_w_down': bf(ffn_w_down), 'w_in_even': bf(w_in_even), 'w_out_even': bf(w_out_even),
         'w_in_odd': bf(w_in_odd), 'w_out_odd': bf(w_out_odd),
         'hgrn_lb_raw': hgrn_lb_raw, 'hgrn_norm': hgrn_norm, 'diff_lam': diff_lam,
         'diff_norm': diff_norm, 'nsa_cmp_w': nsa_cmp_w, 'nsa_cmp_pe': nsa_cmp_pe, 'mlstm_conv_w': mlstm_conv_w,
         'mlstm_conv_b': mlstm_conv_b, 'mlstm_gate_b': mlstm_gate_b, 'mlstm_norm': mlstm_norm,
         'mem_wq': bf(mem_wq), 'mem_wo': bf(mem_wo)}
    Bp, M = mem_prompt.shape[0], mem_prompt.shape[1]
    wkv = bf(mem_wkv)
    ones = jnp.ones((D_MODEL,), F32)
    mem_flat = mem_prompt.reshape(Bp * M, D_MODEL)
    mem_kv_prompt = [_normmm(mem_flat, ones, wkv[l], norm=False).reshape(Bp, M, 2 * MEM_W) for l in range(DEPTH)]
    y_prompt, newp = _run_group(x_prompt, 0, mem_kv_prompt, None, W)
    past = {'page_table': page_table, 'diff_pool': cache_diff_kv, 'nsa_pool': cache_nsa_kv,
            'nsa_win': state_nsa_win, 'hgrn': state_hgrn, 'm_C': state_mlstm_C, 'm_n': state_mlstm_n,
            'm_m': state_mlstm_m, 'm_conv': state_mlstm_conv}
    past_len = page_table.shape[1] * PAGE_SIZE
    mem_kv_sample = [cache_mem_kv[l].reshape(cache_mem_kv.shape[1], MEM_LEN, 2 * MEM_W) for l in range(DEPTH)]
    y_sample, news = _run_group(x_sample, past_len, mem_kv_sample, past, W)
    mem_out = jnp.stack(mem_kv_prompt).reshape(DEPTH, Bp, M, 2, MEM_HEADS, MEM_DH)
    return (y_prompt, y_sample,
            jnp.stack(newp['diff_kv']), jnp.stack(news['diff_kv']),
            jnp.stack(newp['nsa_kv']), jnp.stack(news['nsa_kv']),
            jnp.stack(newp['nsa_win']), jnp.stack(news['nsa_win']),
            jnp.stack(newp['hgrn']), jnp.stack(news['hgrn']),
            jnp.stack(newp['m_C']), jnp.stack(news['m_C']),
            jnp.stack(newp['m_n']), jnp.stack(news['m_n']),
            jnp.stack(newp['m_m']), jnp.stack(news['m_m']),
            jnp.stack(newp['m_conv']), jnp.stack(news['m_conv']),
            mem_out)
```

```python
import functools
import math

import numpy as np
import jax
import jax.numpy as jnp
from jax import lax
from jax.experimental import pallas as pl
from jax.experimental.pallas import tpu as pltpu

F32 = jnp.float32
BF16 = jnp.bfloat16

D_MODEL = 1024
DEPTH = 4
PAGE_SIZE = 128
N_EVEN = (DEPTH + 1) // 2
N_ODD = DEPTH // 2
GROUP_W = D_MODEL // 2
MIX_W = 2 * GROUP_W
H_A = 4
DK_A = 128
DV_A = GROUP_W // H_A
HGRN_CHUNK = 16
F_FLOOR = 1e-30
H_B = 4
DH_B = 64
DV_B = 2 * DH_B
H_C = 4
G_C = 2
HPG_C = H_C // G_C
DK_C = 128
DV_C = GROUP_W // H_C
CMP_BLK = 32
SEL_BLK = 64
N_SEL = 16
WINDOW = 512
SEL_QBLK = 32
H_D = 4
DK_D = 128
DV_D = GROUP_W // H_D
MLSTM_CHUNK = 64
CONV_W = 4
QK_CH_D = 2 * H_D * DK_D
MEM_HEADS = 4
MEM_DH = 128
MEM_W = MEM_HEADS * MEM_DH
MEM_LEN = 256
D_FF = 2816
ROPE_THETA = 10000.0
QBLK = 128
EPS = 1e-6
NEG = -1e30
FORCE_BONUS = 1e4

EVEN_SPLITS = (H_A * DK_A, H_A * DK_A, H_A * DV_A, H_A * DV_A, H_B * DV_B, H_B * DV_B, H_B * DV_B)
ODD_SPLITS = (H_C * DK_C,) + (G_C * DK_C,) * 6 + (3 * H_C, QK_CH_D, H_D * DV_D, H_D, H_D, H_D * DV_D)
IN_EVEN = sum(EVEN_SPLITS)
IN_ODD = sum(ODD_SPLITS)

VMEM_LIMIT_BYTES = 56 * 1024 * 1024
FFN_CHUNK = D_FF // 2


def _cparams(*sem):
    return pltpu.CompilerParams(dimension_semantics=sem, vmem_limit_bytes=VMEM_LIMIT_BYTES)


def _row_tile(m):
    for t in (512, 256, 128, 64, 32, 16, 8):
        if m % t == 0:
            return t
    raise ValueError(f"row count {m} is not a multiple of 8")


def _resident(shape):
    return pl.BlockSpec(shape, lambda *_: (0,) * len(shape), pipeline_mode=pl.Buffered(1))


def _rms(x, g):
    return x * lax.rsqrt(jnp.mean(x * x, axis=-1, keepdims=True) + EPS) * g


def _ffn_kernel(x_ref, g_ref, wg_ref, wu_ref, wd_ref, o_ref):
    x = x_ref[...]
    h = _rms(x, g_ref[...]).astype(BF16)
    acc = jnp.zeros_like(x)
    for c in range(D_FF // FFN_CHUNK):
        sl = slice(c * FFN_CHUNK, (c + 1) * FFN_CHUNK)
        gate = jnp.dot(h, wg_ref[:, sl], preferred_element_type=F32)
        up = jnp.dot(h, wu_ref[:, sl], preferred_element_type=F32)
        a = (jax.nn.silu(gate) * up).astype(BF16)
        acc = acc + jnp.dot(a, wd_ref[sl, :], preferred_element_type=F32)
    o_ref[...] = x + 0.5 * acc


def _ffn(x, g, wg, wu, wd):
    m, d = x.shape
    tm = _row_tile(m)
    return pl.pallas_call(
        _ffn_kernel,
        grid=(m // tm,),
        in_specs=[pl.BlockSpec((tm, d), lambda i: (i, 0)), _resident((1, d)),
                  _resident(wg.shape), _resident(wu.shape), _resident(wd.shape)],
        out_specs=pl.BlockSpec((tm, d), lambda i: (i, 0)),
        out_shape=jax.ShapeDtypeStruct((m, d), F32),
        compiler_params=_cparams("parallel"),
        name="ffn",
    )(x, g.reshape(1, d), wg, wu, wd)


def _normmm_kernel(x_ref, g_ref, w_ref, o_ref, *, norm):
    x = x_ref[...]
    if norm:
        x = _rms(x, g_ref[...])
    o_ref[...] = jnp.dot(x.astype(BF16), w_ref[...], preferred_element_type=F32)


def _normmm(x, g, w, *, norm=True):
    m, d = x.shape
    n = w.shape[1]
    tm = _row_tile(m)
    return pl.pallas_call(
        functools.partial(_normmm_kernel, norm=norm),
        grid=(m // tm,),
        in_specs=[pl.BlockSpec((tm, d), lambda i: (i, 0)), _resident((1, d)), _resident(w.shape)],
        out_specs=pl.BlockSpec((tm, n), lambda i: (i, 0)),
        out_shape=jax.ShapeDtypeStruct((m, n), F32),
        compiler_params=_cparams("parallel"),
        name="normmm",
    )(x, g.reshape(1, d), w)


def _mmres_kernel(*refs, n_in):
    x_ref, o_ref = refs[0], refs[-1]
    acc = x_ref[...]
    for i in range(n_in):
        acc = acc + jnp.dot(refs[1 + i][...].astype(BF16), refs[1 + n_in + i][...], preferred_element_type=F32)
    o_ref[...] = acc


def _mmres(x, acts, ws):
    m, d = x.shape
    tm = _row_tile(m)
    n_in = len(acts)
    return pl.pallas_call(
        functools.partial(_mmres_kernel, n_in=n_in),
        grid=(m // tm,),
        in_specs=([pl.BlockSpec((tm, d), lambda i: (i, 0))]
                  + [pl.BlockSpec((tm, a.shape[1]), lambda i: (i, 0)) for a in acts]
                  + [_resident(w.shape) for w in ws]),
        out_specs=pl.BlockSpec((tm, d), lambda i: (i, 0)),
        out_shape=jax.ShapeDtypeStruct((m, d), F32),
        compiler_params=_cparams("parallel"),
        name="mmres",
    )(x, *acts, *ws)


def _rmsnorm_kernel(x_ref, g_ref, o_ref):
    o_ref[...] = _rms(x_ref[...], g_ref[...])


def _rmsnorm_rows(x, g):
    m, d = x.shape
    tm = _row_tile(m)
    return pl.pallas_call(
        _rmsnorm_kernel,
        grid=(m // tm,),
        in_specs=[pl.BlockSpec((tm, d), lambda i: (i, 0)), _resident((1, d))],
        out_specs=pl.BlockSpec((tm, d), lambda i: (i, 0)),
        out_shape=jax.ShapeDtypeStruct((m, d), F32),
        compiler_params=_cparams("parallel"),
        name="final_norm",
    )(x, g.reshape(1, d))


def _xattn_kernel(q_ref, kv_ref, o_ref):
    scale = MEM_DH ** -0.5
    for h in range(MEM_HEADS):
        lo, hi = h * MEM_DH, (h + 1) * MEM_DH
        q = q_ref[:, lo:hi].astype(BF16)
        k = kv_ref[0, :, lo:hi].astype(BF16)
        v = kv_ref[0, :, MEM_W + lo:MEM_W + hi].astype(BF16)
        s = lax.dot_general(q, k, (((1,), (1,)), ((), ())), preferred_element_type=F32) * scale
        p = jnp.exp(s - jnp.max(s, axis=-1, keepdims=True))
        l = jnp.sum(p, axis=-1, keepdims=True)
        o = jnp.dot(p.astype(BF16), v, preferred_element_type=F32)
        o_ref[:, lo:hi] = o / l


def _xattn(q, kv, t):
    b = kv.shape[0]
    tq = min(t, 512)
    nq = t // tq
    return pl.pallas_call(
        _xattn_kernel,
        grid=(b, nq),
        in_specs=[pl.BlockSpec((tq, MEM_W), lambda i, j: (i * nq + j, 0)),
                  pl.BlockSpec((1, MEM_LEN, 2 * MEM_W), lambda i, j: (i, 0, 0))],
        out_specs=pl.BlockSpec((tq, MEM_W), lambda i, j: (i * nq + j, 0)),
        out_shape=jax.ShapeDtypeStruct((b * t, MEM_W), F32),
        compiler_params=_cparams("parallel", "parallel"),
        name="xattn",
    )(q, kv)


def _rope_tables(pos, half, width):
    inv = ROPE_THETA ** (-jnp.arange(half, dtype=F32) / half)
    ang = pos.astype(F32)[:, None] * inv[None, :]
    cos, sin = jnp.cos(ang), jnp.sin(ang)
    reps = width // (2 * half)
    return (jnp.tile(jnp.concatenate([cos, cos], axis=-1), (1, reps)),
            jnp.tile(jnp.concatenate([-sin, sin], axis=-1), (1, reps)))


def _rot128(x, cosf, sinf):
    return x * cosf + pltpu.roll(x, 64, axis=1) * sinf


def _rot64(x, cosf, sinf):
    lane = lax.broadcasted_iota(jnp.int32, x.shape, 1)
    swapped = jnp.where((lane & 63) < 32, pltpu.roll(x, 96, axis=1), pltpu.roll(x, 32, axis=1))
    return x * cosf + swapped * sinf


def _dot_nt(a, b):
    return lax.dot_general(a, b, (((1,), (1,)), ((), ())), preferred_element_type=F32)


OZ_KV, OZ_Q, OZ_QK, OZ_V, OZ_O, OZ_SM = 0, 1536, 2048, 3072, 3584, 4096
OZ_WIDTH = 4224
NSA_TQ = 512


def _compress_blocks(x_ref, pe_ref, w_ref, cmp_ref, nb):
    acc = [jnp.zeros((nb, DK_C), F32) for _ in range(2 * G_C)]
    for l in range(CMP_BLK):
        for i in range(2 * G_C):
            c = i // G_C
            x = x_ref[i, pl.ds(l, nb, stride=CMP_BLK), :] + pe_ref[c, l:l + 1, :]
            acc[i] += jnp.dot(x.astype(BF16), w_ref[c, l], preferred_element_type=F32)
    for i in range(2 * G_C):
        cmp_ref[:, i * DK_C:(i + 1) * DK_C] = acc[i]


def _nsa_prep_kernel(z_ref, cos_ref, sin_ref, pe_ref, w_ref, rows_ref, wrows_ref, cmp_ref, x_ref, *, tm):
    cosf, sinf = cos_ref[...], sin_ref[...]
    gw = G_C * DK_C
    rows_ref[:, 0:2 * gw] = z_ref[:, 0:2 * gw]
    rows_ref[:, 3 * gw:4 * gw] = z_ref[:, 3 * gw:4 * gw]
    wrows_ref[:, gw:2 * gw] = z_ref[:, 5 * gw:6 * gw]
    for g in range(G_C):
        lo = g * DK_C
        rows_ref[:, 2 * gw + lo:2 * gw + lo + DK_C] = _rot128(z_ref[:, 2 * gw + lo:2 * gw + lo + DK_C], cosf, sinf)
        wrows_ref[:, lo:lo + DK_C] = _rot128(z_ref[:, 4 * gw + lo:4 * gw + lo + DK_C], cosf, sinf)
    for i in range(2 * G_C):
        x_ref[i] = z_ref[:, i * DK_C:(i + 1) * DK_C]
    _compress_blocks(x_ref, pe_ref, w_ref, cmp_ref, tm // CMP_BLK)


def _nsa_prep(z, cosf, sinf, cmp_w, cmp_pe, t):
    m = z.shape[0]
    tm = min(512, t)
    nt = t // tm
    gw = G_C * DK_C
    return pl.pallas_call(
        functools.partial(_nsa_prep_kernel, tm=tm),
        grid=(m // tm,),
        in_specs=[pl.BlockSpec((tm, 6 * gw), lambda i: (i, OZ_KV // (6 * gw))),
                  pl.BlockSpec((tm, DK_C), lambda i: (i % nt, 0)),
                  pl.BlockSpec((tm, DK_C), lambda i: (i % nt, 0)),
                  _resident(cmp_pe.shape), _resident(cmp_w.shape)],
        out_specs=[pl.BlockSpec((tm, 4 * gw), lambda i: (i, 0)),
                   pl.BlockSpec((tm, 2 * gw), lambda i: (i, 0)),
                   pl.BlockSpec((tm // CMP_BLK, 2 * gw), lambda i: (i, 0))],
        out_shape=[jax.ShapeDtypeStruct((m, 4 * gw), F32), jax.ShapeDtypeStruct((m, 2 * gw), F32),
                   jax.ShapeDtypeStruct((m // CMP_BLK, 2 * gw), F32)],
        scratch_shapes=[pltpu.VMEM((2 * G_C, tm, DK_C), F32)],
        compiler_params=_cparams("parallel"),
        name="nsa_prep",
    )(z, cosf, sinf, cmp_pe, cmp_w.astype(BF16))


def _flash_step_t(k, v, q, valid, m_ref, l_ref, acc_ref):
    s = _dot_nt(k, q)
    if valid is not None:
        s = jnp.where(valid, s, NEG)
    m_old = m_ref[...]
    m_new = jnp.maximum(m_old, jnp.max(s, axis=0, keepdims=True))
    alpha = jnp.exp(m_old - m_new)
    p = jnp.exp(s - m_new)
    if valid is not None:
        p = jnp.where(valid, p, 0.0)
    l_ref[...] = alpha * l_ref[...] + jnp.sum(p, axis=0, keepdims=True)
    acc_ref[...] = alpha * acc_ref[...] + lax.dot_general(v, p.astype(BF16), (((0,), (0,)), ((), ())),
                                                           preferred_element_type=F32)
    m_ref[...] = m_new


def _flash_init(m_ref, l_ref, acc_ref):
    m_ref[...] = jnp.full(m_ref.shape, NEG, F32)
    l_ref[...] = jnp.zeros(l_ref.shape, F32)
    acc_ref[...] = jnp.zeros(acc_ref.shape, F32)


def _nsa_kernel(zq_ref, zsm_ref, cos_ref, sin_ref, kcmp_ref, vcmp_ref, ks_ref, vs_ref, kw_ref, vw_ref, e_ref,
                o_ref, m_ref, l_ref, acc_ref, *, tq):
    g = pl.program_id(1)
    t0 = pl.program_id(2) * tq
    scale = DK_C ** -0.5
    n_c = kcmp_ref.shape[0]
    tpos = t0 + lax.broadcasted_iota(jnp.int32, (tq, 1), 0)
    lane = lax.broadcasted_iota(jnp.int32, (tq, n_c), 1)
    cosf, sinf = cos_ref[...], sin_ref[...]
    kc = kcmp_ref[...].astype(BF16)
    vc = vcmp_ref[...].astype(BF16)
    cmask = ((lane + 1) * CMP_BLK - 1) <= tpos
    anyc = tpos >= CMP_BLK - 1
    imp = jnp.zeros((tq, n_c), F32)
    o_cmp = []
    for p in range(HPG_C):
        q = zq_ref[:, p * DK_C:(p + 1) * DK_C].astype(BF16)
        s = jnp.where(cmask, _dot_nt(q, kc) * scale, NEG)
        e = jnp.exp(s - jnp.max(s, axis=-1, keepdims=True))
        pc = jnp.where(anyc, e / jnp.sum(e, axis=-1, keepdims=True), 0.0)
        o_cmp.append(jnp.dot(pc.astype(BF16), vc, preferred_element_type=F32))
        imp = imp + pc
    imp2 = imp + pltpu.roll(imp, n_c - 1, axis=1)
    cur2 = (tpos // SEL_BLK) * 2
    valid = ((lane & 1) == 0) & (lane <= cur2)
    forced = (lane == 0) | (lane == cur2) | (lane == cur2 - 2)
    score = jnp.where(valid, imp2 + jnp.where(forced, FORCE_BONUS, 0.0), -1.0)
    sel = jnp.zeros((tq, n_c), F32)
    lanef = lane.astype(F32)
    for _ in range(N_SEL):
        top = jnp.max(score, axis=-1, keepdims=True)
        idx = jnp.min(jnp.where(score == top, lanef, float(n_c)), axis=-1, keepdims=True)
        pick = lanef == idx
        sel = jnp.where(pick, jnp.where(top > -0.5, 1.0, 0.0), sel)
        score = jnp.where(pick, -3.0, score)
    rq = HPG_C * tq
    sel_t = sel.T.astype(BF16)
    sel_t = jnp.concatenate([sel_t] * HPG_C, axis=1)
    qr = jnp.concatenate([_rot128(zq_ref[:, p * DK_C:(p + 1) * DK_C] * scale, cosf, sinf)
                          for p in range(HPG_C)], axis=0).astype(BF16)
    j = pl.program_id(2)
    krow = lax.broadcasted_iota(jnp.int32, (tq, rq), 0)
    qlane = lax.broadcasted_iota(jnp.int32, (tq, rq), 1) & (tq - 1)
    sm, sl, sacc = m_ref.at[0], l_ref.at[0], acc_ref.at[0]
    wm, wl, wacc = m_ref.at[1], l_ref.at[1], acc_ref.at[1]
    _flash_init(m_ref, l_ref, acc_ref)

    def tile(k_ref, v_ref, jk):
        off = pl.multiple_of(jk * tq, tq)
        return k_ref[pl.ds(off, tq), :].astype(BF16), v_ref[pl.ds(off, tq), :].astype(BF16)

    def sel_body(jk, carry):
        k, v = tile(ks_ref, vs_ref, jk)
        hit = jnp.dot(e_ref[jk], sel_t, preferred_element_type=F32)
        _flash_step_t(k, v, qr, hit > 0.5, sm, sl, sacc)
        return carry

    lax.fori_loop(0, j, sel_body, 0)
    k, v = tile(ks_ref, vs_ref, j)
    hit = jnp.dot(e_ref[j], sel_t, preferred_element_type=F32)
    _flash_step_t(k, v, qr, jnp.where(krow <= qlane, hit, 0.0) > 0.5, sm, sl, sacc)

    def win_body(jk, carry):
        k, v = tile(kw_ref, vw_ref, jk)
        rel = (j - jk) * tq + qlane - krow
        _flash_step_t(k, v, qr, (rel >= 0) & (rel <= WINDOW), wm, wl, wacc)
        return carry

    lax.fori_loop(jnp.maximum(j - (WINDOW + tq - 1) // tq, 0), j + 1, win_body, 0)
    o_sel = (sacc[...] / sl[...]).T
    o_win = (wacc[...] / wl[...]).T
    for p in range(HPG_C):
        def gate(c):
            a = zsm_ref[:, p * 3 + c:p * 3 + c + 1]
            b = zsm_ref[:, (HPG_C + p) * 3 + c:(HPG_C + p) * 3 + c + 1]
            return jax.nn.sigmoid(jnp.where(g == 0, a, b))
        r = slice(p * tq, (p + 1) * tq)
        o_ref[:, p * DV_C:(p + 1) * DV_C] = gate(0) * o_cmp[p] + gate(1) * o_sel[r] + gate(2) * o_win[r]


def _nsa_prompt(z, rows, wrows, cmp, cosf, sinf, b, t):
    tq = NSA_TQ
    nq = t // tq
    n_c = t // CMP_BLK
    assert G_C == 2 and HPG_C == 2 and n_c % 128 == 0 and SEL_BLK == 2 * CMP_BLK and tq % SEL_BLK == 0
    assert tq & (tq - 1) == 0 and t % tq == 0
    sb = np.arange(n_c)[None, None, :]
    key = np.arange(nq)[:, None, None] * tq + np.arange(tq)[None, :, None]
    expand = jnp.asarray((sb % 2 == 0) & (key // SEL_BLK == sb // 2), BF16)
    kv_spec = lambda c: pl.BlockSpec((t, DK_C), lambda i, g, j, c=c: (i, c + g))
    return pl.pallas_call(
        functools.partial(_nsa_kernel, tq=tq),
        grid=(b, G_C, nq),
        in_specs=[pl.BlockSpec((tq, HPG_C * DK_C), lambda i, g, j: (i * nq + j, OZ_Q // (HPG_C * DK_C) + g)),
                  pl.BlockSpec((tq, 128), lambda i, g, j: (i * nq + j, OZ_SM // 128)),
                  pl.BlockSpec((tq, DK_C), lambda i, g, j: (j, 0)),
                  pl.BlockSpec((tq, DK_C), lambda i, g, j: (j, 0)),
                  pl.BlockSpec((n_c, DK_C), lambda i, g, j: (i, g)),
                  pl.BlockSpec((n_c, DK_C), lambda i, g, j: (i, G_C + g)),
                  kv_spec(2 * G_C), kv_spec(3 * G_C), kv_spec(0), kv_spec(G_C),
                  _resident(expand.shape)],
        out_specs=pl.BlockSpec((tq, HPG_C * DV_C), lambda i, g, j: (i * nq + j, g)),
        out_shape=jax.ShapeDtypeStruct((b * t, H_C * DV_C), F32),
        scratch_shapes=[pltpu.VMEM((2, 1, HPG_C * tq), F32), pltpu.VMEM((2, 1, HPG_C * tq), F32),
                        pltpu.VMEM((2, DV_C, HPG_C * tq), F32)],
        compiler_params=_cparams("parallel", "parallel", "arbitrary"),
        name="nsa_prompt",
    )(z, z, cosf, sinf, cmp, cmp, rows, rows, wrows, wrows, expand)


def _split3_terms(x):
    x1 = x.astype(BF16)
    r1 = x - x1.astype(F32)
    x2 = r1.astype(BF16)
    x3 = (r1 - x2.astype(F32)).astype(BF16)
    return x1, x2, x3


def _split3(x):
    return jnp.concatenate(_split3_terms(x), axis=-1)


def _sum3(y, w):
    return y[:, 0:w] + y[:, w:2 * w] + y[:, 2 * w:3 * w]


def _gla_tables(c):
    n_lvl = int(math.log2(c))
    assert 1 << n_lvl == c
    t = np.arange(c)[:, None]
    u = np.arange(c)[None, :]
    blocks = [u <= t, u > t]
    lvl = np.full((c, c), -1, np.int32)
    lvl[np.arange(c), np.arange(c)] = n_lvl
    for l in range(n_lvl):
        m = c >> (l + 1)
        r = (t // (2 * m)) * 2 * m + m - 1
        upper = (t % (2 * m)) >= m
        blocks.append(upper & (u > r) & (u <= t))
        blocks.append(~upper & (u > t) & (u <= r))
        same = (t // (2 * m)) == (u // (2 * m))
        lvl[same & upper & ((u % (2 * m)) < m)] = l
    sel = np.concatenate(blocks, axis=0).astype(np.float32)
    return jnp.asarray(sel, BF16), jnp.asarray(lvl), n_lvl


def _hgrn_kernel(zq_ref, zf_ref, zi_ref, zg_ref, s0_ref, lbraw_ref, ng_ref, sel_ref, lvl_ref, o_ref, s_ref, st_ref,
                 *, layer, c, n_lvl):
    j = pl.program_id(1)

    @pl.when(j == 0)
    def _():
        for h in range(H_A):
            st_ref[h] = s0_ref[0, h].T

    raw = [lbraw_ref[i:i + 1, :] for i in range(N_EVEN)]
    mx = functools.reduce(jnp.maximum, raw)
    ex = [jnp.exp(r - mx) for r in raw]
    den = functools.reduce(lambda a, b: a + b, ex)
    sm = [e / den for e in ex]
    lb_all = functools.reduce(lambda a, b: a + b, sm[:layer + 1]) - sm[0]
    lvl = lvl_ref[...]
    for h in range(H_A):
        lo, hi = h * DK_A, (h + 1) * DK_A
        lb = lb_all[:, lo:hi]
        a = zf_ref[:, lo:hi]
        q = zq_ref[:, lo:hi]
        v = zi_ref[:, lo:hi].astype(BF16)
        g = jnp.log(jnp.maximum(lb + (1.0 - lb) * jax.nn.sigmoid(a), F_FLOOR))
        k = (1.0 - lb) * jax.nn.sigmoid(-a)
        ex = jnp.exp(_sum3(jnp.dot(sel_ref[...], _split3(g), preferred_element_type=F32), DK_A))
        blk = lambda i: ex[i * c:(i + 1) * c]
        st = st_ref[h]
        o = _dot_nt((q * blk(0)).astype(BF16), st.astype(BF16))
        att = jnp.where(lvl == n_lvl, _dot_nt(q.astype(BF16), k.astype(BF16)), 0.0)
        for l in range(n_lvl):
            a_l = _dot_nt((q * blk(2 + 2 * l)).astype(BF16), (k * blk(3 + 2 * l)).astype(BF16))
            att = jnp.where(lvl == l, a_l, att)
        o = o + jnp.dot(att.astype(BF16), v, preferred_element_type=F32)
        ks = (k * blk(1)).astype(BF16)
        st_ref[h] = st * ex[c - 1:c] + lax.dot_general(v, ks, (((0,), (0,)), ((), ())), preferred_element_type=F32)
        o = o * lax.rsqrt(jnp.mean(o * o, axis=-1, keepdims=True) + EPS) * ng_ref[...]
        o_ref[:, lo:hi] = o * jax.nn.silu(zg_ref[:, lo:hi])

    @pl.when(j == pl.num_programs(1) - 1)
    def _():
        for h in range(H_A):
            s_ref[0, h] = st_ref[h].T


def _hgrn(z, s0, lb_raw, norm_g, layer, b, t):
    c = min(t, 128)
    nc = t // c
    sel, lvl, n_lvl = _gla_tables(c)
    w = H_A * DK_A
    col = lambda k: pl.BlockSpec((c, w), lambda i, j, k=k: (i * nc + j, k))
    return pl.pallas_call(
        functools.partial(_hgrn_kernel, layer=layer, c=c, n_lvl=n_lvl),
        grid=(b, nc),
        in_specs=[col(0), col(1), col(2), col(3),
                  pl.BlockSpec((1, H_A, DK_A, DV_A), lambda i, j: (i, 0, 0, 0)),
                  _resident(lb_raw.shape), _resident((1, DV_A)), _resident(sel.shape), _resident(lvl.shape)],
        out_specs=[pl.BlockSpec((c, w), lambda i, j: (i * nc + j, 0)),
                   pl.BlockSpec((1, H_A, DK_A, DV_A), lambda i, j: (i, 0, 0, 0))],
        out_shape=[jax.ShapeDtypeStruct((b * t, w), F32), jax.ShapeDtypeStruct((b, H_A, DK_A, DV_A), F32)],
        scratch_shapes=[pltpu.VMEM((H_A, DV_A, DK_A), F32)],
        compiler_params=_cparams("parallel", "arbitrary"),
        name="hgrn",
    )(z, z, z, z, s0, lb_raw, norm_g.reshape(1, DV_A), sel, lvl)


SM_GATE, SM_I, SM_F = 0, 3 * H_C, 3 * H_C + H_D
HIST = 8


def _log_sigmoid(x):
    return jnp.minimum(x, 0.0) - jnp.log1p(jnp.exp(-jnp.abs(x)))


def _mlstm_kernel(zqk_ref, zv_ref, zo_ref, sm_ref, cbuf_ref, c0_ref, n0_ref, m0_ref, cw_ref, cb_ref, gb_ref, ng_ref,
                  tri_ref, o_ref, conv_ref, c_ref, n_ref, m_ref, xh_ref, st_ref, mm_ref, *, c):
    j = pl.program_id(1)
    last = pl.num_programs(1) - 1
    lane128 = lax.broadcasted_iota(jnp.int32, (DK_D, DV_D), 1)

    @pl.when(j == 0)
    def _():
        xh_ref[HIST - (CONV_W - 1):HIST, :] = cbuf_ref[0]
        mm_ref[...] = m0_ref[0]
        for h in range(H_D):
            st_ref[h, :, 0:DV_D] = c0_ref[0, h]
            ncol = jnp.broadcast_to(n0_ref[0, h:h + 1, :], (DK_D, DK_D)).T
            st_ref[h, :, DV_D:2 * DV_D] = jnp.where(lane128 == 0, ncol, 0.0)

    xh_ref[HIST:HIST + c, :] = zqk_ref[...]
    y = cb_ref[...]
    for jj in range(CONV_W):
        y = y + cw_ref[jj:jj + 1, :] * xh_ref[HIST - (CONV_W - 1) + jj:HIST - (CONV_W - 1) + jj + c, :]
    qk = jax.nn.silu(y)
    tail = xh_ref[HIST + c - (CONV_W - 1):HIST + c, :]
    xh_ref[HIST - (CONV_W - 1):HIST, :] = tail

    pre = sm_ref[...] + gb_ref[...]
    pre_t = pre.T
    lf_c = _log_sigmoid(pre)
    lf_r = _log_sigmoid(pre_t[SM_F:SM_F + 8, :])
    tri = tri_ref[...]
    b_c = _sum3(jnp.dot(tri, _split3(lf_c), preferred_element_type=F32), 128)
    b_r = functools.reduce(lambda x, y: x + y, [_dot_nt(term, tri) for term in _split3_terms(lf_r)])
    row_t = lax.broadcasted_iota(jnp.int32, (c, c), 0)
    col_s = lax.broadcasted_iota(jnp.int32, (c, c), 1)
    causal = col_s <= row_t
    lane = lax.broadcasted_iota(jnp.int32, (1, 128), 1)
    ones_col = jnp.where(lax.broadcasted_iota(jnp.int32, (c, DV_D), 1) == 0, 1.0, 0.0).astype(BF16)
    m_new_row = mm_ref[...]
    for h in range(H_D):
        lo, hi = h * DK_D, (h + 1) * DK_D
        q = qk[:, lo:hi].astype(BF16)
        kf = qk[:, H_D * DK_D + lo:H_D * DK_D + hi] * (DK_D ** -0.5)
        v_aug = jnp.concatenate([zv_ref[:, lo:hi].astype(BF16), ones_col], axis=-1)
        m_prev = mm_ref[:, h:h + 1]
        bc = b_c[:, SM_F + h:SM_F + h + 1]
        a_c = pre[:, SM_I + h:SM_I + h + 1] - bc
        a_r = pre_t[SM_I + h:SM_I + h + 1, :] - b_r[h:h + 1, :]
        cmax = jnp.max(jnp.where(causal, a_r, -jnp.inf), axis=-1, keepdims=True)
        mt = bc + jnp.maximum(m_prev, cmax)
        dprev = jnp.exp(bc + m_prev - mt)
        dm = jnp.exp(jnp.where(causal, a_r + (bc - mt), NEG))
        s = _dot_nt(q, kf.astype(BF16)) * dm
        st = st_ref[h]
        nd = dprev * jnp.dot(q, st.astype(BF16), preferred_element_type=F32) \
            + jnp.dot(s.astype(BF16), v_aug, preferred_element_type=F32)
        den = nd[:, DV_D:DV_D + 1]
        hh = nd[:, 0:DV_D] / jnp.maximum(jnp.abs(den), jnp.exp(-mt))
        b_l = bc[c - 1:c, :]
        m_l = mt[c - 1:c, :]
        w = jnp.exp(a_c + b_l - m_l)
        dl = jnp.exp(b_l + m_prev - m_l)
        st_ref[h] = dl * st + lax.dot_general((w * kf).astype(BF16), v_aug, (((0,), (0,)), ((), ())),
                                              preferred_element_type=F32)
        m_new_row = jnp.where(lane == h, m_l, m_new_row)
        hh = hh * lax.rsqrt(jnp.mean(hh * hh, axis=-1, keepdims=True) + EPS) * ng_ref[...]
        o_ref[:, lo:hi] = hh * jax.nn.sigmoid(zo_ref[:, lo:hi])
    mm_ref[...] = m_new_row

    @pl.when(j == last)
    def _():
        conv_ref[0] = tail
        m_ref[0] = m_new_row
        for h in range(H_D):
            c_ref[0, h] = st_ref[h, :, 0:DV_D]
            n_ref[0, h:h + 1, :] = st_ref[h, :, DV_D:2 * DV_D].T[0:1, :]


def _mlstm(z, conv_buf, c0, n0, m0, conv_w, conv_b, gate_b, norm_g, b, t):
    c = min(t, 128)
    nc = t // c
    tri = jnp.asarray(np.tril(np.ones((c, c), np.float32)), BF16)
    gb = jnp.zeros((1, 128), F32).at[0, SM_I:SM_I + H_D].set(gate_b[0]).at[0, SM_F:SM_F + H_D].set(gate_b[1])
    m0p = jnp.zeros((b, 1, 128), F32).at[:, 0, :H_D].set(m0)
    w = H_D * DV_D
    blk = lambda width, k: pl.BlockSpec((c, width), lambda i, j, k=k: (i * nc + j, k))
    per_b = lambda shape: pl.BlockSpec((1,) + shape, lambda i, j: (i,) + (0,) * len(shape))
    o, conv, cc, nn, mm = pl.pallas_call(
        functools.partial(_mlstm_kernel, c=c),
        grid=(b, nc),
        in_specs=[blk(QK_CH_D, OZ_QK // QK_CH_D), blk(w, OZ_V // w), blk(w, OZ_O // w), blk(128, OZ_SM // 128),
                  per_b((CONV_W - 1, QK_CH_D)), per_b((H_D, DK_D, DV_D)), per_b((H_D, DK_D)), per_b((1, 128)),
                  _resident((CONV_W, QK_CH_D)), _resident((1, QK_CH_D)), _resident((1, 128)), _resident((1, DV_D)),
                  _resident((c, c))],
        out_specs=[pl.BlockSpec((c, w), lambda i, j: (i * nc + j, 0)),
                   per_b((CONV_W - 1, QK_CH_D)), per_b((H_D, DK_D, DV_D)), per_b((H_D, DK_D)), per_b((1, 128))],
        out_shape=[jax.ShapeDtypeStruct((b * t, w), F32), jax.ShapeDtypeStruct((b, CONV_W - 1, QK_CH_D), F32),
                   jax.ShapeDtypeStruct((b, H_D, DK_D, DV_D), F32), jax.ShapeDtypeStruct((b, H_D, DK_D), F32),
                   jax.ShapeDtypeStruct((b, 1, 128), F32)],
        scratch_shapes=[pltpu.VMEM((HIST + c, QK_CH_D), F32), pltpu.VMEM((H_D, DK_D, 2 * DV_D), F32),
                        pltpu.VMEM((1, 128), F32)],
        compiler_params=_cparams("parallel", "arbitrary"),
        name="mlstm",
    )(z, z, z, z, conv_buf, c0, n0, m0p, conv_w, conv_b.reshape(1, QK_CH_D), gb, norm_g.reshape(1, DV_D), tri)
    return o, conv, cc, nn, mm[:, 0, :H_D]


EZ_DQ, EZ_DK, EZ_DV = 4 * H_A * DK_A, 4 * H_A * DK_A + H_B * DV_B, 4 * H_A * DK_A + 2 * H_B * DV_B
DIFF_TQ = 512


def _diff_prep_kernel(zk_ref, zv_ref, cos_ref, sin_ref, kv_ref):
    cosf, sinf = cos_ref[...], sin_ref[...]
    w = H_B * DV_B
    for h in range(H_B):
        kv_ref[:, h * DV_B:(h + 1) * DV_B] = _rot64(zk_ref[:, h * DV_B:(h + 1) * DV_B], cosf, sinf)
    kv_ref[:, w:2 * w] = zv_ref[...]


def _diff_prep(z, cosf, sinf, t):
    m = z.shape[0]
    tm = min(512, t)
    nt = t // tm
    w = H_B * DV_B
    return pl.pallas_call(
        _diff_prep_kernel,
        grid=(m // tm,),
        in_specs=[pl.BlockSpec((tm, w), lambda i: (i, EZ_DK // w)), pl.BlockSpec((tm, w), lambda i: (i, EZ_DV // w)),
                  pl.BlockSpec((tm, DV_B), lambda i: (i % nt, 0)), pl.BlockSpec((tm, DV_B), lambda i: (i % nt, 0))],
        out_specs=pl.BlockSpec((tm, 2 * w), lambda i: (i, 0)),
        out_shape=jax.ShapeDtypeStruct((m, 2 * w), F32),
        compiler_params=_cparams("parallel"),
        name="diff_prep",
    )(z, z, cosf, sinf)


def _diff_lambda(lv_ref, lam_init):
    lv = lv_ref[...]
    return (jnp.exp(jnp.sum(lv[0:1] * lv[1:2], axis=-1, keepdims=True))
            - jnp.exp(jnp.sum(lv[2:3] * lv[3:4], axis=-1, keepdims=True)) + lam_init)


def _diff_queries(zq, cosf, sinf):
    q = _rot64(zq, cosf, sinf)
    lane = lax.broadcasted_iota(jnp.int32, q.shape, 1)
    return jnp.concatenate([jnp.where(lane < DH_B, q, 0.0), jnp.where(lane >= DH_B, q, 0.0)], axis=0).astype(BF16)


def _diff_finish(o1, o2, lam, lam_init, ng):
    o = o1 - lam * o2
    o = o * lax.rsqrt(jnp.mean(o * o, axis=-1, keepdims=True) + EPS) * ng
    return o * (1.0 - lam_init)


def _diff_kernel(zq_ref, cos_ref, sin_ref, k_ref, v_ref, lv_ref, ng_ref, o_ref, m_ref, l_ref, acc_ref,
                 *, tq, lam_init):
    j = pl.program_id(2)
    q2 = _diff_queries(zq_ref[...] * (DH_B ** -0.5), cos_ref[...], sin_ref[...])
    _flash_init(m_ref, l_ref, acc_ref)

    def tile(jk):
        off = pl.multiple_of(jk * tq, tq)
        return k_ref[pl.ds(off, tq), :].astype(BF16), v_ref[pl.ds(off, tq), :].astype(BF16)

    def body(jk, carry):
        k, v = tile(jk)
        _flash_step_t(k, v, q2, None, m_ref, l_ref, acc_ref)
        return carry

    lax.fori_loop(0, j, body, 0)
    krow = lax.broadcasted_iota(jnp.int32, (tq, 2 * tq), 0)
    qlane = lax.broadcasted_iota(jnp.int32, (tq, 2 * tq), 1) & (tq - 1)
    k, v = tile(j)
    _flash_step_t(k, v, q2, krow <= qlane, m_ref, l_ref, acc_ref)
    o_t = acc_ref[...] / l_ref[...]
    lam = _diff_lambda(lv_ref, lam_init)
    o_ref[...] = _diff_finish(o_t[:, 0:tq].T, o_t[:, tq:2 * tq].T, lam, lam_init, ng_ref[...])


def _diff_prompt(z, kv, cosf, sinf, lam_vecs, norm_g, layer_idx, b, t):
    tq = DIFF_TQ
    nq = t // tq
    assert tq & (tq - 1) == 0 and t % tq == 0
    lam_init = 0.8 - 0.6 * math.exp(-0.3 * layer_idx)
    return pl.pallas_call(
        functools.partial(_diff_kernel, tq=tq, lam_init=lam_init),
        grid=(b, H_B, nq),
        in_specs=[pl.BlockSpec((tq, DV_B), lambda i, h, j: (i * nq + j, EZ_DQ // DV_B + h)),
                  pl.BlockSpec((tq, DV_B), lambda i, h, j: (j, 0)),
                  pl.BlockSpec((tq, DV_B), lambda i, h, j: (j, 0)),
                  pl.BlockSpec((t, DV_B), lambda i, h, j: (i, h)),
                  pl.BlockSpec((t, DV_B), lambda i, h, j: (i, H_B + h)),
                  _resident(lam_vecs.shape), _resident((1, DV_B))],
        out_specs=pl.BlockSpec((tq, DV_B), lambda i, h, j: (i * nq + j, h)),
        out_shape=jax.ShapeDtypeStruct((b * t, H_B * DV_B), F32),
        scratch_shapes=[pltpu.VMEM((1, 2 * tq), F32), pltpu.VMEM((1, 2 * tq), F32), pltpu.VMEM((DV_B, 2 * tq), F32)],
        compiler_params=_cparams("parallel", "parallel", "arbitrary"),
        name="diff_prompt",
    )(z, cosf, sinf, kv, kv, lam_vecs, norm_g.reshape(1, DV_B))


DEC_PAGES = 16


def _pool_rows(pool):
    return pool.reshape(pool.shape[0], pool.shape[1], -1, pool.shape[-1])


def _page_spec(pool4, layer, page_of):
    return pl.BlockSpec((1, 1) + pool4.shape[2:], lambda i, s, pt: (layer, page_of(i, s, pt), 0, 0))


def _page_rows(pg, slot, n_slots):
    return pg[0, 0, pl.ds(slot, PAGE_SIZE, stride=n_slots), :]


def _diff_dec_kernel(pt_ref, zq_ref, zk_ref, zv_ref, cos_ref, sin_ref, lv_ref, ng_ref, *rest, tq, n_pg, lam_init):
    pages = rest[:n_pg]
    o_ref, kv_ref, m_ref, l_ref, acc_ref, q2_ref = rest[n_pg:]
    step = pl.program_id(1)
    scale = DH_B ** -0.5
    w = H_B * DV_B
    cosf, sinf = cos_ref[...], sin_ref[...]

    @pl.when(step == 0)
    def _():
        m_ref[...] = jnp.full(m_ref.shape, NEG, F32)
        l_ref[...] = jnp.zeros(l_ref.shape, F32)
        acc_ref[...] = jnp.zeros(acc_ref.shape, F32)
        for h in range(H_B):
            q2_ref[h] = _diff_queries(zq_ref[:, h * DV_B:(h + 1) * DV_B], cosf, sinf)

    rows = 2 * tq
    sc = jnp.concatenate(
        [jnp.concatenate([_dot_nt(q2_ref[h], _page_rows(pg, h, 2 * H_B).astype(BF16)) for pg in pages], axis=1)
         for h in range(H_B)], axis=0) * scale
    m_old = jnp.concatenate([m_ref[h] for h in range(H_B)], axis=0)
    l_old = jnp.concatenate([l_ref[h] for h in range(H_B)], axis=0)
    m_new = jnp.maximum(m_old, jnp.max(sc, axis=-1, keepdims=True))
    alpha = jnp.exp(m_old - m_new)
    p = jnp.exp(sc - m_new)
    l_new = alpha * l_old + jnp.sum(p, axis=-1, keepdims=True)
    for h in range(H_B):
        r = slice(h * rows, (h + 1) * rows)
        pv = jnp.zeros((rows, DV_B), F32)
        for i, pg in enumerate(pages):
            pv = pv + jnp.dot(p[r, i * PAGE_SIZE:(i + 1) * PAGE_SIZE].astype(BF16),
                              _page_rows(pg, H_B + h, 2 * H_B).astype(BF16),
                              preferred_element_type=F32)
        acc_ref[h] = alpha[r] * acc_ref[h] + pv
        m_ref[h] = m_new[r]
        l_ref[h] = l_new[r]

    @pl.when(step == pl.num_programs(1) - 1)
    def _():
        lam = _diff_lambda(lv_ref, lam_init)
        qi = lax.broadcasted_iota(jnp.int32, (2 * tq, tq), 0) & (tq - 1)
        ki = lax.broadcasted_iota(jnp.int32, (2 * tq, tq), 1)
        kv_ref[:, w:2 * w] = zv_ref[...]
        for h in range(H_B):
            lo, hi = h * DV_B, (h + 1) * DV_B
            k_new = _rot64(zk_ref[:, lo:hi], cosf, sinf)
            kv_ref[:, lo:hi] = k_new
            s_new = _dot_nt(q2_ref[h].astype(F32), k_new) * scale
            valid = ki <= qi
            m_old = m_ref[h]
            m_new = jnp.maximum(m_old, jnp.max(jnp.where(valid, s_new, NEG), axis=-1, keepdims=True))
            alpha = jnp.exp(m_old - m_new)
            p = jnp.where(valid, jnp.exp(s_new - m_new), 0.0)
            l = alpha * l_ref[h] + jnp.sum(p, axis=-1, keepdims=True)
            acc = alpha * acc_ref[h] + jnp.dot(p, zv_ref[:, lo:hi], preferred_element_type=F32)
            o_ref[:, lo:hi] = _diff_finish(acc[0:tq] / l[0:tq], acc[tq:2 * tq] / l[tq:2 * tq], lam, lam_init,
                                           ng_ref[...])


def _diff_decode(z, pool, layer, page_table, cosf, sinf, lam_vecs, norm_g, layer_idx, b, t):
    n_pages = page_table.shape[1]
    n_pg = DEC_PAGES
    assert n_pages % n_pg == 0 and t & (t - 1) == 0 and t % 8 == 0
    w = H_B * DV_B
    lam_init = 0.8 - 0.6 * math.exp(-0.3 * layer_idx)
    pool4 = _pool_rows(pool)
    zcol = lambda k: pl.BlockSpec((t, w), lambda i, s, pt, k=k: (i, k))
    const = lambda shape: pl.BlockSpec(shape, lambda i, s, pt: (0,) * len(shape))
    page = lambda p: _page_spec(pool4, layer, lambda i, s, pt, p=p: pt[i, s * n_pg + p])
    grid_spec = pltpu.PrefetchScalarGridSpec(
        num_scalar_prefetch=1,
        grid=(b, n_pages // n_pg),
        in_specs=[zcol(EZ_DQ // w), zcol(EZ_DK // w), zcol(EZ_DV // w), const((t, DV_B)), const((t, DV_B)),
                  const(lam_vecs.shape), const((1, DV_B))] + [page(p) for p in range(n_pg)],
        out_specs=[pl.BlockSpec((t, w), lambda i, s, pt: (i, 0)), pl.BlockSpec((t, 2 * w), lambda i, s, pt: (i, 0))],
        scratch_shapes=[pltpu.VMEM((H_B, 2 * t, 1), F32), pltpu.VMEM((H_B, 2 * t, 1), F32),
                        pltpu.VMEM((H_B, 2 * t, DV_B), F32), pltpu.VMEM((H_B, 2 * t, DV_B), BF16)])
    return pl.pallas_call(
        functools.partial(_diff_dec_kernel, tq=t, n_pg=n_pg, lam_init=lam_init),
        grid_spec=grid_spec,
        out_shape=[jax.ShapeDtypeStruct((b * t, w), F32), jax.ShapeDtypeStruct((b * t, 2 * w), F32)],
        compiler_params=_cparams("parallel", "arbitrary"),
        name="diff_decode",
    )(page_table, z, z, z, cosf, sinf, lam_vecs, norm_g.reshape(1, DV_B), *([pool4] * n_pg))


CMP_PAGES = 16


def _nsa_dec_cmp_kernel(pt_ref, pe_ref, w_ref, *rest, n_pg):
    pages = rest[:n_pg]
    cmp_ref, x_ref = rest[n_pg:]
    for p, pg in enumerate(pages):
        for i in range(2 * G_C):
            x_ref[i, p * PAGE_SIZE:(p + 1) * PAGE_SIZE, :] = _page_rows(pg, i, 4 * G_C)
    _compress_blocks(x_ref, pe_ref, w_ref, cmp_ref, n_pg * PAGE_SIZE // CMP_BLK)


def _nsa_dec_cmp(pool4, layer, page_table, cmp_w, cmp_pe):
    b, n_pages = page_table.shape
    n_pg = CMP_PAGES
    assert n_pages % n_pg == 0 and PAGE_SIZE % CMP_BLK == 0
    gw = G_C * DK_C
    nb = n_pg * PAGE_SIZE // CMP_BLK
    steps = n_pages // n_pg
    const = lambda shape: pl.BlockSpec(shape, lambda i, s, pt: (0,) * len(shape))
    page = lambda p: _page_spec(pool4, layer, lambda i, s, pt, p=p: pt[i, s * n_pg + p])
    grid_spec = pltpu.PrefetchScalarGridSpec(
        num_scalar_prefetch=1,
        grid=(b, steps),
        in_specs=[const(cmp_pe.shape), const(cmp_w.shape)] + [page(p) for p in range(n_pg)],
        out_specs=pl.BlockSpec((nb, 2 * gw), lambda i, s, pt: (i * steps + s, 0)),
        scratch_shapes=[pltpu.VMEM((2 * G_C, n_pg * PAGE_SIZE, DK_C), F32)])
    return pl.pallas_call(
        functools.partial(_nsa_dec_cmp_kernel, n_pg=n_pg),
        grid_spec=grid_spec,
        out_shape=jax.ShapeDtypeStruct((b * steps * nb, 2 * gw), F32),
        compiler_params=_cparams("parallel", "arbitrary"),
        name="nsa_dec_cmp",
    )(page_table, cmp_pe, cmp_w.astype(BF16), *([pool4] * n_pg))


def _softmax_rows(s):
    e = jnp.exp(s - jnp.max(s, axis=-1, keepdims=True))
    return e / jnp.sum(e, axis=-1, keepdims=True)


def _nsa_dec_kernel(pt_ref, zq_ref, zkv_ref, zsm_ref, cos_ref, sin_ref, cmp_ref, win_ref, e_ref, *rest,
                    tq, n_pg, pos0):
    pages = rest[:n_pg]
    o_ref, rows_ref, wout_ref, sel_ref, q_ref, ocmp_ref, m_ref, l_ref, acc_ref = rest[n_pg:]
    step = pl.program_id(1)
    scale = DK_C ** -0.5
    gw = G_C * DK_C
    n_c = cmp_ref.shape[0]
    cosf, sinf = cos_ref[...], sin_ref[...]
    rows2 = HPG_C * tq
    tpos = pos0 + (lax.broadcasted_iota(jnp.int32, (rows2, 1), 0) & (tq - 1))

    def stacked(fn):
        return lambda g: jnp.concatenate([fn(zq_ref[:, (g * HPG_C + p) * DK_C:(g * HPG_C + p + 1) * DK_C])
                                          for p in range(HPG_C)], axis=0)

    @pl.when(step == 0)
    def _():
        m_ref[...] = jnp.full(m_ref.shape, NEG, F32)
        l_ref[...] = jnp.zeros(l_ref.shape, F32)
        acc_ref[...] = jnp.zeros(acc_ref.shape, F32)
        lane = lax.broadcasted_iota(jnp.int32, (rows2, n_c), 1)
        lane_t = lane[0:tq]
        lanef = lane_t.astype(F32)
        for g in range(G_C):
            kc = cmp_ref[:, g * DK_C:(g + 1) * DK_C].astype(BF16)
            vc = cmp_ref[:, gw + g * DK_C:gw + (g + 1) * DK_C].astype(BF16)
            qg = stacked(lambda x: x)(g).astype(BF16)
            cmask = ((lane + 1) * CMP_BLK - 1) <= tpos
            pc = _softmax_rows(jnp.where(cmask, _dot_nt(qg, kc) * scale, NEG))
            pc = jnp.where(tpos >= CMP_BLK - 1, pc, 0.0)
            ocmp_ref[g] = jnp.dot(pc.astype(BF16), vc, preferred_element_type=F32)
            imp = functools.reduce(lambda a, b: a + b, [pc[p * tq:(p + 1) * tq] for p in range(HPG_C)])
            imp2 = jnp.concatenate([imp[:, c0:c0 + 128] + pltpu.roll(imp[:, c0:c0 + 128], 127, axis=1)
                                    for c0 in range(0, n_c, 128)], axis=1)
            cur2 = (tpos[0:tq] // SEL_BLK) * 2
            valid = ((lane_t & 1) == 0) & (lane_t <= cur2)
            forced = (lane_t == 0) | (lane_t == cur2) | (lane_t == cur2 - 2)
            score = jnp.where(valid, imp2 + jnp.where(forced, FORCE_BONUS, 0.0), -1.0)
            sel = jnp.zeros((tq, n_c), F32)
            for _ in range(N_SEL - 1):
                top = jnp.max(score, axis=-1, keepdims=True)
                idx = jnp.min(jnp.where(score == top, lanef, float(n_c)), axis=-1, keepdims=True)
                pick = lanef == idx
                sel = jnp.where(pick, jnp.where(top > -0.5, 1.0, 0.0), sel)
                score = jnp.where(pick, -3.0, score)
            sel_ref[g] = jnp.concatenate([sel] * HPG_C, axis=0).astype(BF16)
            q_ref[g] = stacked(lambda x: _rot128(x, cosf, sinf))(g).astype(BF16)

    sc = jnp.concatenate(
        [jnp.concatenate([_dot_nt(q_ref[g], _page_rows(pg, 2 * G_C + g, 4 * G_C).astype(BF16)) for pg in pages], axis=1)
         for g in range(G_C)], axis=0) * scale
    ok = jnp.concatenate(
        [jnp.concatenate([jnp.dot(sel_ref[g], e_ref[p], preferred_element_type=F32) for p in range(n_pg)], axis=1)
         for g in range(G_C)], axis=0) > 0.5
    m_old = jnp.concatenate([m_ref[g] for g in range(G_C)], axis=0)
    l_old = jnp.concatenate([l_ref[g] for g in range(G_C)], axis=0)
    m_new = jnp.maximum(m_old, jnp.max(jnp.where(ok, sc, NEG), axis=-1, keepdims=True))
    alpha = jnp.exp(m_old - m_new)
    p_ = jnp.where(ok, jnp.exp(sc - m_new), 0.0)
    l_new = alpha * l_old + jnp.sum(p_, axis=-1, keepdims=True)
    for g in range(G_C):
        r = slice(g * rows2, (g + 1) * rows2)
        pv = jnp.zeros((rows2, DV_C), F32)
        for i, pg in enumerate(pages):
            pv = pv + jnp.dot(p_[r, i * PAGE_SIZE:(i + 1) * PAGE_SIZE].astype(BF16),
                              _page_rows(pg, 3 * G_C + g, 4 * G_C).astype(BF16), preferred_element_type=F32)
        acc_ref[g] = alpha[r] * acc_ref[g] + pv
        m_ref[g] = m_new[r]
        l_ref[g] = l_new[r]

    @pl.when(step == pl.num_programs(1) - 1)
    def _():
        qi = lax.broadcasted_iota(jnp.int32, (rows2, tq), 0) & (tq - 1)
        ki = lax.broadcasted_iota(jnp.int32, (rows2, tq), 1)
        causal = ki <= qi
        wj = lax.broadcasted_iota(jnp.int32, (rows2, WINDOW), 1)
        wq = lax.broadcasted_iota(jnp.int32, (rows2, WINDOW), 0) & (tq - 1)
        in_win = wj >= wq
        rows_ref[:, 0:2 * gw] = zkv_ref[:, 0:2 * gw]
        rows_ref[:, 3 * gw:4 * gw] = zkv_ref[:, 3 * gw:4 * gw]
        wout_ref[0, 0:WINDOW - tq, :] = win_ref[0, tq:WINDOW, :]
        wout_ref[0, WINDOW - tq:WINDOW, gw:2 * gw] = zkv_ref[:, 5 * gw:6 * gw]
        for g in range(G_C):
            lo = g * DK_C
            q = q_ref[g].astype(F32)
            ks_new = _rot128(zkv_ref[:, 2 * gw + lo:2 * gw + lo + DK_C], cosf, sinf)
            kw_new = _rot128(zkv_ref[:, 4 * gw + lo:4 * gw + lo + DK_C], cosf, sinf)
            vs_new = zkv_ref[:, 3 * gw + lo:3 * gw + lo + DK_C]
            vw_new = zkv_ref[:, 5 * gw + lo:5 * gw + lo + DK_C]
            rows_ref[:, 2 * gw + lo:2 * gw + lo + DK_C] = ks_new
            wout_ref[0, WINDOW - tq:WINDOW, lo:lo + DK_C] = kw_new
            s_new = _dot_nt(q, ks_new) * scale
            m_old = m_ref[g]
            m_new = jnp.maximum(m_old, jnp.max(jnp.where(causal, s_new, NEG), axis=-1, keepdims=True))
            alpha = jnp.exp(m_old - m_new)
            p_ = jnp.where(causal, jnp.exp(s_new - m_new), 0.0)
            l_sel = alpha * l_ref[g] + jnp.sum(p_, axis=-1, keepdims=True)
            o_sel = (alpha * acc_ref[g] + jnp.dot(p_, vs_new, preferred_element_type=F32)) / l_sel
            kw = win_ref[0, :, lo:lo + DK_C].astype(BF16)
            vw = win_ref[0, :, gw + lo:gw + lo + DK_C].astype(BF16)
            s_buf = jnp.where(in_win, _dot_nt(q_ref[g], kw) * scale, NEG)
            s_own = jnp.where(causal, _dot_nt(q, kw_new) * scale, NEG)
            m_w = jnp.maximum(jnp.max(s_buf, axis=-1, keepdims=True), jnp.max(s_own, axis=-1, keepdims=True))
            p_buf = jnp.where(in_win, jnp.exp(s_buf - m_w), 0.0)
            p_own = jnp.where(causal, jnp.exp(s_own - m_w), 0.0)
            l_w = jnp.sum(p_buf, axis=-1, keepdims=True) + jnp.sum(p_own, axis=-1, keepdims=True)
            o_win = (jnp.dot(p_buf.astype(BF16), vw, preferred_element_type=F32)
                     + jnp.dot(p_own, vw_new, preferred_element_type=F32)) / l_w
            o_cmp = ocmp_ref[g]
            for p in range(HPG_C):
                h = g * HPG_C + p
                gate = lambda c: jax.nn.sigmoid(zsm_ref[:, SM_GATE + h * 3 + c:SM_GATE + h * 3 + c + 1])
                r = slice(p * tq, (p + 1) * tq)
                o_ref[:, h * DV_C:(h + 1) * DV_C] = gate(0) * o_cmp[r] + gate(1) * o_sel[r] + gate(2) * o_win[r]


def _nsa_decode(z, pool, layer, page_table, win_buf, cosf, sinf, cmp_w, cmp_pe, pos0, b, t):
    n_pages = page_table.shape[1]
    n_pg = DEC_PAGES
    gw = G_C * DK_C
    n_c = n_pages * PAGE_SIZE // CMP_BLK
    assert pos0 == n_pages * PAGE_SIZE and pos0 % SEL_BLK == 0 and t <= SEL_BLK and t % 8 == 0 and t & (t - 1) == 0
    assert n_pages % n_pg == 0 and n_c % 128 == 0 and N_SEL >= 3 and win_buf.shape[1] == WINDOW and G_C == 2
    pool4 = _pool_rows(pool)
    cmp = _nsa_dec_cmp(pool4, layer, page_table, cmp_w, cmp_pe)
    win3 = win_buf.reshape(b, WINDOW, 2 * gw)
    sb = np.arange(n_c)[None, :, None]
    key = np.arange(n_pages)[:, None, None] * PAGE_SIZE + np.arange(PAGE_SIZE)[None, None, :]
    expand = jnp.asarray((sb % 2 == 0) & (key // SEL_BLK == sb // 2), BF16)
    zblk = lambda width, k: pl.BlockSpec((t, width), lambda i, s, pt, k=k: (i, k))
    const = lambda shape: pl.BlockSpec(shape, lambda i, s, pt: (0,) * len(shape))
    page = lambda p: _page_spec(pool4, layer, lambda i, s, pt, p=p: pt[i, s * n_pg + p])
    rows2 = HPG_C * t
    grid_spec = pltpu.PrefetchScalarGridSpec(
        num_scalar_prefetch=1,
        grid=(b, n_pages // n_pg),
        in_specs=[zblk(H_C * DK_C, OZ_Q // (H_C * DK_C)), zblk(6 * gw, OZ_KV // (6 * gw)), zblk(128, OZ_SM // 128),
                  const((t, DK_C)), const((t, DK_C)),
                  pl.BlockSpec((n_c, 2 * gw), lambda i, s, pt: (i, 0)),
                  pl.BlockSpec((1, WINDOW, 2 * gw), lambda i, s, pt: (i, 0, 0)),
                  pl.BlockSpec((n_pg, n_c, PAGE_SIZE), lambda i, s, pt: (s, 0, 0))] + [page(p) for p in range(n_pg)],
        out_specs=[pl.BlockSpec((t, H_C * DV_C), lambda i, s, pt: (i, 0)),
                   pl.BlockSpec((t, 4 * gw), lambda i, s, pt: (i, 0)),
                   pl.BlockSpec((1, WINDOW, 2 * gw), lambda i, s, pt: (i, 0, 0))],
        scratch_shapes=[pltpu.VMEM((G_C, rows2, n_c), BF16), pltpu.VMEM((G_C, rows2, DK_C), BF16),
                        pltpu.VMEM((G_C, rows2, DV_C), F32), pltpu.VMEM((G_C, rows2, 1), F32),
                        pltpu.VMEM((G_C, rows2, 1), F32), pltpu.VMEM((G_C, rows2, DV_C), F32)])
    o, rows, wout = pl.pallas_call(
        functools.partial(_nsa_dec_kernel, tq=t, n_pg=n_pg, pos0=pos0),
        grid_spec=grid_spec,
        out_shape=[jax.ShapeDtypeStruct((b * t, H_C * DV_C), F32), jax.ShapeDtypeStruct((b * t, 4 * gw), F32),
                   jax.ShapeDtypeStruct((b, WINDOW, 2 * gw), F32)],
        compiler_params=_cparams("parallel", "arbitrary"),
        name="nsa_decode",
    )(page_table, z, z, z, cosf, sinf, cmp, win3, expand, *([pool4] * n_pg))
    return o, rows, wout


def _run_group(x, pos0, mem_kv, past, W):
    B, T, D = x.shape
    pos = pos0 + jnp.arange(T, dtype=jnp.int32)
    cos32, sin32 = _rope_tables(pos, DH_B // 2, DV_B)
    cos64, sin64 = _rope_tables(pos, DK_C // 2, DK_C)
    new = {name: [] for name in ('diff_kv', 'hgrn', 'nsa_kv', 'nsa_win', 'm_C', 'm_n', 'm_m', 'm_conv')}
    x = x.reshape(B * T, D)
    for l in range(DEPTH):
        g = W['norm_w'][l]
        x = _ffn(x, g[0], W['ffn_w_gate'][l, 0], W['ffn_w_up'][l, 0], W['ffn_w_down'][l, 0])
        i = l // 2
        if l % 2 == 0:
            z = _normmm(x, g[1], W['w_in_even'][i])
            S0 = jnp.zeros((B, H_A, DK_A, DV_A), F32) if past is None else past['hgrn'][i]
            oa, S = _hgrn(z, S0, W['hgrn_lb_raw'], W['hgrn_norm'][i], i, B, T)
            if past is None:
                kv_new = _diff_prep(z, cos32, sin32, T)
                ob = _diff_prompt(z, kv_new, cos32, sin32, W['diff_lam'][i], W['diff_norm'][i], l, B, T)
                kv_new = kv_new.reshape(B, T, 2, H_B, DV_B)
            else:
                ob, kv_new = _diff_decode(z, past['diff_pool'], i, past['page_table'], cos32, sin32,
                                          W['diff_lam'][i], W['diff_norm'][i], l, B, T)
                kv_new = kv_new.reshape(B, T, 2, H_B, DV_B)
            w_out = W['w_out_even'][i]
            new['hgrn'].append(S)
            new['diff_kv'].append(kv_new)
        else:
            z = _normmm(x, g[1], W['w_in_odd'][i])
            if past is None:
                conv_buf = jnp.zeros((B, CONV_W - 1, QK_CH_D), F32)
                C0 = jnp.zeros((B, H_D, DK_D, DV_D), F32)
                n0 = jnp.zeros((B, H_D, DK_D), F32)
                m0 = jnp.zeros((B, H_D), F32)
                rows, wrows, cmp = _nsa_prep(z, cos64, sin64, W['nsa_cmp_w'][i], W['nsa_cmp_pe'][i], T)
                oa = _nsa_prompt(z, rows, wrows, cmp, cos64, sin64, B, T)
                rows = rows.reshape(B, T, 4, G_C, DK_C)
                win_new = wrows.reshape(B, T, 2, G_C, DK_C)[:, -min(WINDOW, T):]
            else:
                conv_buf, C0, n0, m0 = past['m_conv'][i], past['m_C'][i], past['m_n'][i], past['m_m'][i]
                oa, rows, win_new = _nsa_decode(z, past['nsa_pool'], i, past['page_table'], past['nsa_win'][i],
                                                cos64, sin64, W['nsa_cmp_w'][i], W['nsa_cmp_pe'][i], pos0, B, T)
                rows = rows.reshape(B, T, 4, G_C, DK_C)
                win_new = win_new.reshape(B, WINDOW, 2, G_C, DK_C)
            ob, conv_new, C, n, m = _mlstm(z, conv_buf, C0, n0, m0, W['mlstm_conv_w'][i], W['mlstm_conv_b'][i],
                                           W['mlstm_gate_b'][i], W['mlstm_norm'][i], B, T)
            w_out = W['w_out_odd'][i]
            new['nsa_kv'].append(rows)
            new['nsa_win'].append(win_new)
            new['m_C'].append(C)
            new['m_n'].append(n)
            new['m_m'].append(m)
            new['m_conv'].append(conv_new)
        x = _mmres(x, [oa.reshape(B * T, GROUP_W), ob.reshape(B * T, GROUP_W)], [w_out[:GROUP_W], w_out[GROUP_W:]])
        q = _normmm(x, g[2], W['mem_wq'][l])
        a = _xattn(q, mem_kv[l], T)
        x = _mmres(x, [a], [W['mem_wo'][l]])
        x = _ffn(x, g[3], W['ffn_w_gate'][l, 1], W['ffn_w_up'][l, 1], W['ffn_w_down'][l, 1])
    return _rmsnorm_rows(x, W['norm_final']).reshape(B, T, D), new


def kernel(x_prompt, x_sample, mem_prompt, cache_diff_kv, cache_nsa_kv, state_nsa_win, state_hgrn, state_mlstm_C, state_mlstm_n, state_mlstm_m, state_mlstm_conv, cache_mem_kv, page_table, norm_w, norm_final, ffn_w_gate, ffn_w_up, ffn_w_down, w_in_even, w_out_even, w_in_odd, w_out_odd, hgrn_lb_raw, hgrn_norm, diff_lam, diff_norm, nsa_cmp_w, nsa_cmp_pe, mlstm_conv_w, mlstm_conv_b, mlstm_gate_b, mlstm_norm, mem_wq, mem_wkv, mem_wo):
    bf = lambda w: w.astype(BF16)
    o_q, o_kv, o_g, o_qk, o_v, o_i, o_f, o_o = (int(c) for c in np.cumsum((0,) + ODD_SPLITS)[[0, 1, 7, 8, 9, 10, 11, 12]])
    w_in_odd = jnp.concatenate(
        [w_in_odd[..., o_kv:o_g], w_in_odd[..., o_q:o_kv], w_in_odd[..., o_qk:o_v], w_in_odd[..., o_v:o_i],
         w_in_odd[..., o_o:], w_in_odd[..., o_g:o_qk], w_in_odd[..., o_i:o_o],
         jnp.zeros(w_in_odd.shape[:2] + (OZ_WIDTH - IN_ODD,), w_in_odd.dtype)], axis=-1)
    W = {'norm_w': norm_w, 'norm_final': norm_final, 'ffn_w_gate': bf(ffn_w_gate), 'ffn_w_up': bf(ffn_w_up),
         'ffn_w_down': bf(ffn_w_down), 'w_in_even': bf(w_in_even), 'w_out_even': bf(w_out_even),
         'w_in_odd': bf(w_in_odd), 'w_out_odd': bf(w_out_odd),
         'hgrn_lb_raw': hgrn_lb_raw, 'hgrn_norm': hgrn_norm, 'diff_lam': diff_lam,
         'diff_norm': diff_norm, 'nsa_cmp_w': nsa_cmp_w, 'nsa_cmp_pe': nsa_cmp_pe, 'mlstm_conv_w': mlstm_conv_w,
         'mlstm_conv_b': mlstm_conv_b, 'mlstm_gate_b': mlstm_gate_b, 'mlstm_norm': mlstm_norm,
         'mem_wq': bf(mem_wq), 'mem_wo': bf(mem_wo)}
    Bp, M = mem_prompt.shape[0], mem_prompt.shape[1]
    wkv = bf(mem_wkv)
    ones = jnp.ones((D_MODEL,), F32)
    mem_flat = mem_prompt.reshape(Bp * M, D_MODEL)
    mem_kv_prompt = [_normmm(mem_flat, ones, wkv[l], norm=False).reshape(Bp, M, 2 * MEM_W) for l in range(DEPTH)]
    y_prompt, newp = _run_group(x_prompt, 0, mem_kv_prompt, None, W)
    past = {'page_table': page_table, 'diff_pool': cache_diff_kv, 'nsa_pool': cache_nsa_kv,
            'nsa_win': state_nsa_win, 'hgrn': state_hgrn, 'm_C': state_mlstm_C, 'm_n': state_mlstm_n,
            'm_m': state_mlstm_m, 'm_conv': state_mlstm_conv}
    past_len = page_table.shape[1] * PAGE_SIZE
    mem_kv_sample = [cache_mem_kv[l].reshape(cache_mem_kv.shape[1], MEM_LEN, 2 * MEM_W) for l in range(DEPTH)]
    y_sample, news = _run_group(x_sample, past_len, mem_kv_sample, past, W)
    mem_out = jnp.stack(mem_kv_prompt).reshape(DEPTH, Bp, M, 2, MEM_HEADS, MEM_DH)
    return (y_prompt, y_sample,
            jnp.stack(newp['diff_kv']), jnp.stack(news['diff_kv']),
            jnp.stack(newp['nsa_kv']), jnp.stack(news['nsa_kv']),
            jnp.stack(newp['nsa_win']), jnp.stack(news['nsa_win']),
            jnp.stack(newp['hgrn']), jnp.stack(news['hgrn']),
            jnp.stack(newp['m_C']), jnp.stack(news['m_C']),
            jnp.stack(newp['m_n']), jnp.stack(news['m_n']),
            jnp.stack(newp['m_m']), jnp.stack(news['m_m']),
            jnp.stack(newp['m_conv']), jnp.stack(news['m_conv']),
            mem_out)
```

```python
import functools
import math

import numpy as np
import jax
import jax.numpy as jnp
from jax import lax
from jax.experimental import pallas as pl
from jax.experimental.pallas import tpu as pltpu

F32 = jnp.float32
BF16 = jnp.bfloat16

D_MODEL = 1024
DEPTH = 4
PAGE_SIZE = 128
N_EVEN = (DEPTH + 1) // 2
N_ODD = DEPTH // 2
GROUP_W = D_MODEL // 2
MIX_W = 2 * GROUP_W
H_A = 4
DK_A = 128
DV_A = GROUP_W // H_A
HGRN_CHUNK = 16
F_FLOOR = 1e-30
H_B = 4
DH_B = 64
DV_B = 2 * DH_B
H_C = 4
G_C = 2
HPG_C = H_C // G_C
DK_C = 128
DV_C = GROUP_W // H_C
CMP_BLK = 32
SEL_BLK = 64
N_SEL = 16
WINDOW = 512
SEL_QBLK = 32
H_D = 4
DK_D = 128
DV_D = GROUP_W // H_D
MLSTM_CHUNK = 64
CONV_W = 4
QK_CH_D = 2 * H_D * DK_D
MEM_HEADS = 4
MEM_DH = 128
MEM_W = MEM_HEADS * MEM_DH
MEM_LEN = 256
D_FF = 2816
ROPE_THETA = 10000.0
QBLK = 128
EPS = 1e-6
NEG = -1e30
FORCE_BONUS = 1e4

EVEN_SPLITS = (H_A * DK_A, H_A * DK_A, H_A * DV_A, H_A * DV_A, H_B * DV_B, H_B * DV_B, H_B * DV_B)
ODD_SPLITS = (H_C * DK_C,) + (G_C * DK_C,) * 6 + (3 * H_C, QK_CH_D, H_D * DV_D, H_D, H_D, H_D * DV_D)
IN_EVEN = sum(EVEN_SPLITS)
IN_ODD = sum(ODD_SPLITS)

VMEM_LIMIT_BYTES = 56 * 1024 * 1024
FFN_CHUNK = D_FF // 2


def _cparams(*sem):
    return pltpu.CompilerParams(dimension_semantics=sem, vmem_limit_bytes=VMEM_LIMIT_BYTES)


def _row_tile(m):
    for t in (512, 256, 128, 64, 32, 16, 8):
        if m % t == 0:
            return t
    raise ValueError(f"row count {m} is not a multiple of 8")


def _resident(shape):
    return pl.BlockSpec(shape, lambda *_: (0,) * len(shape), pipeline_mode=pl.Buffered(1))


def _rms(x, g):
    return x * lax.rsqrt(jnp.mean(x * x, axis=-1, keepdims=True) + EPS) * g


def _ffn_kernel(x_ref, g_ref, wg_ref, wu_ref, wd_ref, o_ref):
    x = x_ref[...]
    h = _rms(x, g_ref[...]).astype(BF16)
    acc = jnp.zeros_like(x)
    for c in range(D_FF // FFN_CHUNK):
        sl = slice(c * FFN_CHUNK, (c + 1) * FFN_CHUNK)
        gate = jnp.dot(h, wg_ref[:, sl], preferred_element_type=F32)
        up = jnp.dot(h, wu_ref[:, sl], preferred_element_type=F32)
        a = (jax.nn.silu(gate) * up).astype(BF16)
        acc = acc + jnp.dot(a, wd_ref[sl, :], preferred_element_type=F32)
    o_ref[...] = x + 0.5 * acc


def _ffn(x, g, wg, wu, wd):
    m, d = x.shape
    tm = _row_tile(m)
    return pl.pallas_call(
        _ffn_kernel,
        grid=(m // tm,),
        in_specs=[pl.BlockSpec((tm, d), lambda i: (i, 0)), _resident((1, d)),
                  _resident(wg.shape), _resident(wu.shape), _resident(wd.shape)],
        out_specs=pl.BlockSpec((tm, d), lambda i: (i, 0)),
        out_shape=jax.ShapeDtypeStruct((m, d), F32),
        compiler_params=_cparams("parallel"),
        name="ffn",
    )(x, g.reshape(1, d), wg, wu, wd)


def _normmm_kernel(x_ref, g_ref, w_ref, o_ref, *, norm):
    x = x_ref[...]
    if norm:
        x = _rms(x, g_ref[...])
    o_ref[...] = jnp.dot(x.astype(BF16), w_ref[...], preferred_element_type=F32)


def _normmm(x, g, w, *, norm=True):
    m, d = x.shape
    n = w.shape[1]
    tm = _row_tile(m)
    return pl.pallas_call(
        functools.partial(_normmm_kernel, norm=norm),
        grid=(m // tm,),
        in_specs=[pl.BlockSpec((tm, d), lambda i: (i, 0)), _resident((1, d)), _resident(w.shape)],
        out_specs=pl.BlockSpec((tm, n), lambda i: (i, 0)),
        out_shape=jax.ShapeDtypeStruct((m, n), F32),
        compiler_params=_cparams("parallel"),
        name="normmm",
    )(x, g.reshape(1, d), w)


def _mmres_kernel(*refs, n_in):
    x_ref, o_ref = refs[0], refs[-1]
    acc = x_ref[...]
    for i in range(n_in):
        acc = acc + jnp.dot(refs[1 + i][...].astype(BF16), refs[1 + n_in + i][...], preferred_element_type=F32)
    o_ref[...] = acc


def _mmres(x, acts, ws):
    m, d = x.shape
    tm = _row_tile(m)
    n_in = len(acts)
    return pl.pallas_call(
        functools.partial(_mmres_kernel, n_in=n_in),
        grid=(m // tm,),
        in_specs=([pl.BlockSpec((tm, d), lambda i: (i, 0))]
                  + [pl.BlockSpec((tm, a.shape[1]), lambda i: (i, 0)) for a in acts]
                  + [_resident(w.shape) for w in ws]),
        out_specs=pl.BlockSpec((tm, d), lambda i: (i, 0)),
        out_shape=jax.ShapeDtypeStruct((m, d), F32),
        compiler_params=_cparams("parallel"),
        name="mmres",
    )(x, *acts, *ws)


def _rmsnorm_kernel(x_ref, g_ref, o_ref):
    o_ref[...] = _rms(x_ref[...], g_ref[...])


def _rmsnorm_rows(x, g):
    m, d = x.shape
    tm = _row_tile(m)
    return pl.pallas_call(
        _rmsnorm_kernel,
        grid=(m // tm,),
        in_specs=[pl.BlockSpec((tm, d), lambda i: (i, 0)), _resident((1, d))],
        out_specs=pl.BlockSpec((tm, d), lambda i: (i, 0)),
        out_shape=jax.ShapeDtypeStruct((m, d), F32),
        compiler_params=_cparams("parallel"),
        name="final_norm",
    )(x, g.reshape(1, d))


def _xattn_kernel(q_ref, kv_ref, o_ref, *, cache_rows):
    scale = MEM_DH ** -0.5
    for h in range(MEM_HEADS):
        lo, hi = h * MEM_DH, (h + 1) * MEM_DH
        q = q_ref[:, lo:hi].astype(BF16)
        if cache_rows:
            k = kv_ref[0, 0, pl.ds(h, MEM_LEN, stride=2 * MEM_HEADS), :].astype(BF16)
            v = kv_ref[0, 0, pl.ds(MEM_HEADS + h, MEM_LEN, stride=2 * MEM_HEADS), :].astype(BF16)
        else:
            k = kv_ref[0, :, lo:hi].astype(BF16)
            v = kv_ref[0, :, MEM_W + lo:MEM_W + hi].astype(BF16)
        s = lax.dot_general(q, k, (((1,), (1,)), ((), ())), preferred_element_type=F32) * scale
        p = jnp.exp(s - jnp.max(s, axis=-1, keepdims=True))
        l = jnp.sum(p, axis=-1, keepdims=True)
        o = jnp.dot(p.astype(BF16), v, preferred_element_type=F32)
        o_ref[:, lo:hi] = o / l


def _xattn(q, kv, t, layer=None):
    if layer is None:
        b = kv.shape[0]
        kv_spec = pl.BlockSpec((1, MEM_LEN, 2 * MEM_W), lambda i, j: (i, 0, 0))
    else:
        b = kv.shape[1]
        kv = kv.reshape(kv.shape[0], b, MEM_LEN * 2 * MEM_HEADS, MEM_DH)
        kv_spec = pl.BlockSpec((1, 1) + kv.shape[2:], lambda i, j: (layer, i, 0, 0))
    tq = min(t, 512)
    nq = t // tq
    return pl.pallas_call(
        functools.partial(_xattn_kernel, cache_rows=layer is not None),
        grid=(b, nq),
        in_specs=[pl.BlockSpec((tq, MEM_W), lambda i, j: (i * nq + j, 0)), kv_spec],
        out_specs=pl.BlockSpec((tq, MEM_W), lambda i, j: (i * nq + j, 0)),
        out_shape=jax.ShapeDtypeStruct((b * t, MEM_W), F32),
        compiler_params=_cparams("parallel", "parallel"),
        name="xattn",
    )(q, kv)


def _rope_tables(pos, half, width):
    inv = ROPE_THETA ** (-jnp.arange(half, dtype=F32) / half)
    ang = pos.astype(F32)[:, None] * inv[None, :]
    cos, sin = jnp.cos(ang), jnp.sin(ang)
    reps = width // (2 * half)
    return (jnp.tile(jnp.concatenate([cos, cos], axis=-1), (1, reps)),
            jnp.tile(jnp.concatenate([-sin, sin], axis=-1), (1, reps)))


def _rot128(x, cosf, sinf):
    return x * cosf + pltpu.roll(x, 64, axis=1) * sinf


def _rot64(x, cosf, sinf):
    lane = lax.broadcasted_iota(jnp.int32, x.shape, 1)
    swapped = jnp.where((lane & 63) < 32, pltpu.roll(x, 96, axis=1), pltpu.roll(x, 32, axis=1))
    return x * cosf + swapped * sinf


def _dot_nt(a, b):
    return lax.dot_general(a, b, (((1,), (1,)), ((), ())), preferred_element_type=F32)


OZ_KV, OZ_Q, OZ_QK, OZ_V, OZ_O, OZ_SM = 0, 1536, 2048, 3072, 3584, 4096
OZ_WIDTH = 4224
NSA_TQ = 512


def _compress_blocks(x_ref, pe_ref, w_ref, cmp_ref, nb, tiled=False):
    acc = [jnp.zeros((nb, DK_C), F32) for _ in range(2 * G_C)]
    for l in range(CMP_BLK):
        start, stride = ((l // 8) * nb * 8 + l % 8, 8) if tiled else (l, CMP_BLK)
        for i in range(2 * G_C):
            c = i // G_C
            x = x_ref[i, pl.ds(start, nb, stride=stride), :] + pe_ref[c, l:l + 1, :]
            acc[i] += jnp.dot(x.astype(BF16), w_ref[c, l], preferred_element_type=F32)
    for i in range(2 * G_C):
        cmp_ref[:, i * DK_C:(i + 1) * DK_C] = acc[i]


def _nsa_prep_kernel(z_ref, cos_ref, sin_ref, pe_ref, w_ref, rows_ref, wrows_ref, cmp_ref, x_ref, *, tm):
    cosf, sinf = cos_ref[...], sin_ref[...]
    gw = G_C * DK_C
    rows_ref[:, 0:2 * gw] = z_ref[:, 0:2 * gw]
    rows_ref[:, 3 * gw:4 * gw] = z_ref[:, 3 * gw:4 * gw]
    wrows_ref[:, gw:2 * gw] = z_ref[:, 5 * gw:6 * gw]
    for g in range(G_C):
        lo = g * DK_C
        rows_ref[:, 2 * gw + lo:2 * gw + lo + DK_C] = _rot128(z_ref[:, 2 * gw + lo:2 * gw + lo + DK_C], cosf, sinf)
        wrows_ref[:, lo:lo + DK_C] = _rot128(z_ref[:, 4 * gw + lo:4 * gw + lo + DK_C], cosf, sinf)
    for i in range(2 * G_C):
        x_ref[i] = z_ref[:, i * DK_C:(i + 1) * DK_C]
    _compress_blocks(x_ref, pe_ref, w_ref, cmp_ref, tm // CMP_BLK)


def _nsa_prep(z, cosf, sinf, cmp_w, cmp_pe, t):
    m = z.shape[0]
    tm = min(512, t)
    nt = t // tm
    gw = G_C * DK_C
    return pl.pallas_call(
        functools.partial(_nsa_prep_kernel, tm=tm),
        grid=(m // tm,),
        in_specs=[pl.BlockSpec((tm, 6 * gw), lambda i: (i, OZ_KV // (6 * gw))),
                  pl.BlockSpec((tm, DK_C), lambda i: (i % nt, 0)),
                  pl.BlockSpec((tm, DK_C), lambda i: (i % nt, 0)),
                  _resident(cmp_pe.shape), _resident(cmp_w.shape)],
        out_specs=[pl.BlockSpec((tm, 4 * gw), lambda i: (i, 0)),
                   pl.BlockSpec((tm, 2 * gw), lambda i: (i, 0)),
                   pl.BlockSpec((tm // CMP_BLK, 2 * gw), lambda i: (i, 0))],
        out_shape=[jax.ShapeDtypeStruct((m, 4 * gw), F32), jax.ShapeDtypeStruct((m, 2 * gw), F32),
                   jax.ShapeDtypeStruct((m // CMP_BLK, 2 * gw), F32)],
        scratch_shapes=[pltpu.VMEM((2 * G_C, tm, DK_C), F32)],
        compiler_params=_cparams("parallel"),
        name="nsa_prep",
    )(z, cosf, sinf, cmp_pe, cmp_w.astype(BF16))


def _flash_step_t(k, v, q, valid, m_ref, l_ref, acc_ref):
    s = _dot_nt(k, q)
    if valid is not None:
        s = jnp.where(valid, s, NEG)
    m_old = m_ref[...]
    m_new = jnp.maximum(m_old, jnp.max(s, axis=0, keepdims=True))
    alpha = jnp.exp(m_old - m_new)
    p = jnp.exp(s - m_new)
    if valid is not None:
        p = jnp.where(valid, p, 0.0)
    l_ref[...] = alpha * l_ref[...] + jnp.sum(p, axis=0, keepdims=True)
    acc_ref[...] = alpha * acc_ref[...] + lax.dot_general(v, p.astype(BF16), (((0,), (0,)), ((), ())),
                                                           preferred_element_type=F32)
    m_ref[...] = m_new


def _flash_init(m_ref, l_ref, acc_ref):
    m_ref[...] = jnp.full(m_ref.shape, NEG, F32)
    l_ref[...] = jnp.zeros(l_ref.shape, F32)
    acc_ref[...] = jnp.zeros(acc_ref.shape, F32)


def _nsa_kernel(zq_ref, zsm_ref, cos_ref, sin_ref, kcmp_ref, vcmp_ref, ks_ref, vs_ref, kw_ref, vw_ref, e_ref,
                o_ref, m_ref, l_ref, acc_ref, *, tq):
    g = pl.program_id(1)
    t0 = pl.program_id(2) * tq
    scale = DK_C ** -0.5
    n_c = kcmp_ref.shape[0]
    tpos = t0 + lax.broadcasted_iota(jnp.int32, (tq, 1), 0)
    lane = lax.broadcasted_iota(jnp.int32, (tq, n_c), 1)
    cosf, sinf = cos_ref[...], sin_ref[...]
    kc = kcmp_ref[...].astype(BF16)
    vc = vcmp_ref[...].astype(BF16)
    cmask = ((lane + 1) * CMP_BLK - 1) <= tpos
    anyc = tpos >= CMP_BLK - 1
    imp = jnp.zeros((tq, n_c), F32)
    o_cmp = []
    for p in range(HPG_C):
        q = zq_ref[:, p * DK_C:(p + 1) * DK_C].astype(BF16)
        s = jnp.where(cmask, _dot_nt(q, kc) * scale, NEG)
        e = jnp.exp(s - jnp.max(s, axis=-1, keepdims=True))
        pc = jnp.where(anyc, e / jnp.sum(e, axis=-1, keepdims=True), 0.0)
        o_cmp.append(jnp.dot(pc.astype(BF16), vc, preferred_element_type=F32))
        imp = imp + pc
    imp2 = imp + pltpu.roll(imp, n_c - 1, axis=1)
    cur2 = (tpos // SEL_BLK) * 2
    valid = ((lane & 1) == 0) & (lane <= cur2)
    forced = (lane == 0) | (lane == cur2) | (lane == cur2 - 2)
    score = jnp.where(valid, imp2 + jnp.where(forced, FORCE_BONUS, 0.0), -1.0)
    sel = jnp.zeros((tq, n_c), F32)
    lanef = lane.astype(F32)
    for _ in range(N_SEL):
        top = jnp.max(score, axis=-1, keepdims=True)
        idx = jnp.min(jnp.where(score == top, lanef, float(n_c)), axis=-1, keepdims=True)
        pick = lanef == idx
        sel = jnp.where(pick, jnp.where(top > -0.5, 1.0, 0.0), sel)
        score = jnp.where(pick, -3.0, score)
    rq = HPG_C * tq
    sel_t = sel.T.astype(BF16)
    sel_t = jnp.concatenate([sel_t] * HPG_C, axis=1)
    qr = jnp.concatenate([_rot128(zq_ref[:, p * DK_C:(p + 1) * DK_C] * scale, cosf, sinf)
                          for p in range(HPG_C)], axis=0).astype(BF16)
    j = pl.program_id(2)
    krow = lax.broadcasted_iota(jnp.int32, (tq, rq), 0)
    qlane = lax.broadcasted_iota(jnp.int32, (tq, rq), 1) & (tq - 1)
    sm, sl, sacc = m_ref.at[0], l_ref.at[0], acc_ref.at[0]
    wm, wl, wacc = m_ref.at[1], l_ref.at[1], acc_ref.at[1]
    _flash_init(m_ref, l_ref, acc_ref)

    def tile(k_ref, v_ref, jk):
        off = pl.multiple_of(jk * tq, tq)
        return k_ref[pl.ds(off, tq), :].astype(BF16), v_ref[pl.ds(off, tq), :].astype(BF16)

    def sel_body(jk, carry):
        k, v = tile(ks_ref, vs_ref, jk)
        hit = jnp.dot(e_ref[jk], sel_t, preferred_element_type=F32)
        _flash_step_t(k, v, qr, hit > 0.5, sm, sl, sacc)
        return carry

    lax.fori_loop(0, j, sel_body, 0)
    k, v = tile(ks_ref, vs_ref, j)
    hit = jnp.dot(e_ref[j], sel_t, preferred_element_type=F32)
    _flash_step_t(k, v, qr, jnp.where(krow <= qlane, hit, 0.0) > 0.5, sm, sl, sacc)

    def win_body(jk, carry):
        k, v = tile(kw_ref, vw_ref, jk)
        rel = (j - jk) * tq + qlane - krow
        _flash_step_t(k, v, qr, (rel >= 0) & (rel <= WINDOW), wm, wl, wacc)
        return carry

    lax.fori_loop(jnp.maximum(j - (WINDOW + tq - 1) // tq, 0), j + 1, win_body, 0)
    o_sel = (sacc[...] / sl[...]).T
    o_win = (wacc[...] / wl[...]).T
    for p in range(HPG_C):
        def gate(c):
            a = zsm_ref[:, p * 3 + c:p * 3 + c + 1]
            b = zsm_ref[:, (HPG_C + p) * 3 + c:(HPG_C + p) * 3 + c + 1]
            return jax.nn.sigmoid(jnp.where(g == 0, a, b))
        r = slice(p * tq, (p + 1) * tq)
        o_ref[:, p * DV_C:(p + 1) * DV_C] = gate(0) * o_cmp[p] + gate(1) * o_sel[r] + gate(2) * o_win[r]


def _nsa_prompt(z, rows, wrows, cmp, cosf, sinf, b, t):
    tq = NSA_TQ
    nq = t // tq
    n_c = t // CMP_BLK
    assert G_C == 2 and HPG_C == 2 and n_c % 128 == 0 and SEL_BLK == 2 * CMP_BLK and tq % SEL_BLK == 0
    assert tq & (tq - 1) == 0 and t % tq == 0
    sb = np.arange(n_c)[None, None, :]
    key = np.arange(nq)[:, None, None] * tq + np.arange(tq)[None, :, None]
    expand = jnp.asarray((sb % 2 == 0) & (key // SEL_BLK == sb // 2), BF16)
    kv_spec = lambda c: pl.BlockSpec((t, DK_C), lambda i, g, j, c=c: (i, c + g))
    return pl.pallas_call(
        functools.partial(_nsa_kernel, tq=tq),
        grid=(b, G_C, nq),
        in_specs=[pl.BlockSpec((tq, HPG_C * DK_C), lambda i, g, j: (i * nq + j, OZ_Q // (HPG_C * DK_C) + g)),
                  pl.BlockSpec((tq, 128), lambda i, g, j: (i * nq + j, OZ_SM // 128)),
                  pl.BlockSpec((tq, DK_C), lambda i, g, j: (j, 0)),
                  pl.BlockSpec((tq, DK_C), lambda i, g, j: (j, 0)),
                  pl.BlockSpec((n_c, DK_C), lambda i, g, j: (i, g)),
                  pl.BlockSpec((n_c, DK_C), lambda i, g, j: (i, G_C + g)),
                  kv_spec(2 * G_C), kv_spec(3 * G_C), kv_spec(0), kv_spec(G_C),
                  _resident(expand.shape)],
        out_specs=pl.BlockSpec((tq, HPG_C * DV_C), lambda i, g, j: (i * nq + j, g)),
        out_shape=jax.ShapeDtypeStruct((b * t, H_C * DV_C), F32),
        scratch_shapes=[pltpu.VMEM((2, 1, HPG_C * tq), F32), pltpu.VMEM((2, 1, HPG_C * tq), F32),
                        pltpu.VMEM((2, DV_C, HPG_C * tq), F32)],
        compiler_params=_cparams("parallel", "parallel", "arbitrary"),
        name="nsa_prompt",
    )(z, z, cosf, sinf, cmp, cmp, rows, rows, wrows, wrows, expand)


def _split3_terms(x):
    x1 = x.astype(BF16)
    r1 = x - x1.astype(F32)
    x2 = r1.astype(BF16)
    x3 = (r1 - x2.astype(F32)).astype(BF16)
    return x1, x2, x3


def _split3(x):
    return jnp.concatenate(_split3_terms(x), axis=-1)


def _sum3(y, w):
    return y[:, 0:w] + y[:, w:2 * w] + y[:, 2 * w:3 * w]


def _gla_tables(c):
    n_lvl = int(math.log2(c))
    assert 1 << n_lvl == c
    t = np.arange(c)[:, None]
    u = np.arange(c)[None, :]
    blocks = [u <= t, u > t]
    lvl = np.full((c, c), -1, np.int32)
    lvl[np.arange(c), np.arange(c)] = n_lvl
    for l in range(n_lvl):
        m = c >> (l + 1)
        r = (t // (2 * m)) * 2 * m + m - 1
        upper = (t % (2 * m)) >= m
        blocks.append(upper & (u > r) & (u <= t))
        blocks.append(~upper & (u > t) & (u <= r))
        same = (t // (2 * m)) == (u // (2 * m))
        lvl[same & upper & ((u % (2 * m)) < m)] = l
    sel = np.concatenate(blocks, axis=0).astype(np.float32)
    return jnp.asarray(sel, BF16), jnp.asarray(lvl), n_lvl


def _hgrn_kernel(zq_ref, zf_ref, zi_ref, zg_ref, s0_ref, lbraw_ref, ng_ref, sel_ref, lvl_ref, o_ref, s_ref, st_ref,
                 *, layer, c, n_lvl):
    j = pl.program_id(1)

    @pl.when(j == 0)
    def _():
        for h in range(H_A):
            st_ref[h] = s0_ref[0, h].T

    raw = [lbraw_ref[i:i + 1, :] for i in range(N_EVEN)]
    mx = functools.reduce(jnp.maximum, raw)
    ex = [jnp.exp(r - mx) for r in raw]
    den = functools.reduce(lambda a, b: a + b, ex)
    sm = [e / den for e in ex]
    lb_all = functools.reduce(lambda a, b: a + b, sm[:layer + 1]) - sm[0]
    lvl = lvl_ref[...]
    for h in range(H_A):
        lo, hi = h * DK_A, (h + 1) * DK_A
        lb = lb_all[:, lo:hi]
        a = zf_ref[:, lo:hi]
        q = zq_ref[:, lo:hi]
        v = zi_ref[:, lo:hi].astype(BF16)
        g = jnp.log(jnp.maximum(lb + (1.0 - lb) * jax.nn.sigmoid(a), F_FLOOR))
        k = (1.0 - lb) * jax.nn.sigmoid(-a)
        ex = jnp.exp(_sum3(jnp.dot(sel_ref[...], _split3(g), preferred_element_type=F32), DK_A))
        blk = lambda i: ex[i * c:(i + 1) * c]
        st = st_ref[h]
        o = _dot_nt((q * blk(0)).astype(BF16), st.astype(BF16))
        att = jnp.where(lvl == n_lvl, _dot_nt(q.astype(BF16), k.astype(BF16)), 0.0)
        for l in range(n_lvl):
            a_l = _dot_nt((q * blk(2 + 2 * l)).astype(BF16), (k * blk(3 + 2 * l)).astype(BF16))
            att = jnp.where(lvl == l, a_l, att)
        o = o + jnp.dot(att.astype(BF16), v, preferred_element_type=F32)
        ks = (k * blk(1)).astype(BF16)
        st_ref[h] = st * ex[c - 1:c] + lax.dot_general(v, ks, (((0,), (0,)), ((), ())), preferred_element_type=F32)
        o = o * lax.rsqrt(jnp.mean(o * o, axis=-1, keepdims=True) + EPS) * ng_ref[...]
        o_ref[:, lo:hi] = o * jax.nn.silu(zg_ref[:, lo:hi])

    @pl.when(j == pl.num_programs(1) - 1)
    def _():
        for h in range(H_A):
            s_ref[0, h] = st_ref[h].T


def _hgrn(z, s0, lb_raw, norm_g, layer, b, t):
    c = min(t, 128)
    nc = t // c
    sel, lvl, n_lvl = _gla_tables(c)
    w = H_A * DK_A
    col = lambda k: pl.BlockSpec((c, w), lambda i, j, k=k: (i * nc + j, k))
    return pl.pallas_call(
        functools.partial(_hgrn_kernel, layer=layer, c=c, n_lvl=n_lvl),
        grid=(b, nc),
        in_specs=[col(0), col(1), col(2), col(3),
                  pl.BlockSpec((1, H_A, DK_A, DV_A), lambda i, j: (i, 0, 0, 0)),
                  _resident(lb_raw.shape), _resident((1, DV_A)), _resident(sel.shape), _resident(lvl.shape)],
        out_specs=[pl.BlockSpec((c, w), lambda i, j: (i * nc + j, 0)),
                   pl.BlockSpec((1, H_A, DK_A, DV_A), lambda i, j: (i, 0, 0, 0))],
        out_shape=[jax.ShapeDtypeStruct((b * t, w), F32), jax.ShapeDtypeStruct((b, H_A, DK_A, DV_A), F32)],
        scratch_shapes=[pltpu.VMEM((H_A, DV_A, DK_A), F32)],
        compiler_params=_cparams("parallel", "arbitrary"),
        name="hgrn",
    )(z, z, z, z, s0, lb_raw, norm_g.reshape(1, DV_A), sel, lvl)


SM_GATE, SM_I, SM_F = 0, 3 * H_C, 3 * H_C + H_D
HIST = 8


def _log_sigmoid(x):
    return jnp.minimum(x, 0.0) - jnp.log1p(jnp.exp(-jnp.abs(x)))


def _mlstm_kernel(zqk_ref, zv_ref, zo_ref, sm_ref, cbuf_ref, c0_ref, n0_ref, m0_ref, cw_ref, cb_ref, gb_ref, ng_ref,
                  tri_ref, o_ref, conv_ref, c_ref, n_ref, m_ref, xh_ref, st_ref, mm_ref, *, c):
    j = pl.program_id(1)
    last = pl.num_programs(1) - 1
    lane128 = lax.broadcasted_iota(jnp.int32, (DK_D, DV_D), 1)

    @pl.when(j == 0)
    def _():
        xh_ref[HIST - (CONV_W - 1):HIST, :] = cbuf_ref[0]
        mm_ref[...] = m0_ref[0]
        for h in range(H_D):
            st_ref[h, :, 0:DV_D] = c0_ref[0, h]
            ncol = jnp.broadcast_to(n0_ref[0, h:h + 1, :], (DK_D, DK_D)).T
            st_ref[h, :, DV_D:2 * DV_D] = jnp.where(lane128 == 0, ncol, 0.0)

    xh_ref[HIST:HIST + c, :] = zqk_ref[...]
    y = cb_ref[...]
    for jj in range(CONV_W):
        y = y + cw_ref[jj:jj + 1, :] * xh_ref[HIST - (CONV_W - 1) + jj:HIST - (CONV_W - 1) + jj + c, :]
    qk = jax.nn.silu(y)
    tail = xh_ref[HIST + c - (CONV_W - 1):HIST + c, :]
    xh_ref[HIST - (CONV_W - 1):HIST, :] = tail

    pre = sm_ref[...] + gb_ref[...]
    pre_t = pre.T
    lf_c = _log_sigmoid(pre)
    lf_r = _log_sigmoid(pre_t[SM_F:SM_F + 8, :])
    tri = tri_ref[...]
    b_c = _sum3(jnp.dot(tri, _split3(lf_c), preferred_element_type=F32), 128)
    b_r = functools.reduce(lambda x, y: x + y, [_dot_nt(term, tri) for term in _split3_terms(lf_r)])
    row_t = lax.broadcasted_iota(jnp.int32, (c, c), 0)
    col_s = lax.broadcasted_iota(jnp.int32, (c, c), 1)
    causal = col_s <= row_t
    lane = lax.broadcasted_iota(jnp.int32, (1, 128), 1)
    ones_col = jnp.where(lax.broadcasted_iota(jnp.int32, (c, DV_D), 1) == 0, 1.0, 0.0).astype(BF16)
    m_new_row = mm_ref[...]
    for h in range(H_D):
        lo, hi = h * DK_D, (h + 1) * DK_D
        q = qk[:, lo:hi].astype(BF16)
        kf = qk[:, H_D * DK_D + lo:H_D * DK_D + hi] * (DK_D ** -0.5)
        v_aug = jnp.concatenate([zv_ref[:, lo:hi].astype(BF16), ones_col], axis=-1)
        m_prev = mm_ref[:, h:h + 1]
        bc = b_c[:, SM_F + h:SM_F + h + 1]
        a_c = pre[:, SM_I + h:SM_I + h + 1] - bc
        a_r = pre_t[SM_I + h:SM_I + h + 1, :] - b_r[h:h + 1, :]
        cmax = jnp.max(jnp.where(causal, a_r, -jnp.inf), axis=-1, keepdims=True)
        mt = bc + jnp.maximum(m_prev, cmax)
        dprev = jnp.exp(bc + m_prev - mt)
        dm = jnp.exp(jnp.where(causal, a_r + (bc - mt), NEG))
        s = _dot_nt(q, kf.astype(BF16)) * dm
        st = st_ref[h]
        nd = dprev * jnp.dot(q, st.astype(BF16), preferred_element_type=F32) \
            + jnp.dot(s.astype(BF16), v_aug, preferred_element_type=F32)
        den = nd[:, DV_D:DV_D + 1]
        hh = nd[:, 0:DV_D] / jnp.maximum(jnp.abs(den), jnp.exp(-mt))
        b_l = bc[c - 1:c, :]
        m_l = mt[c - 1:c, :]
        w = jnp.exp(a_c + b_l - m_l)
        dl = jnp.exp(b_l + m_prev - m_l)
        st_ref[h] = dl * st + lax.dot_general((w * kf).astype(BF16), v_aug, (((0,), (0,)), ((), ())),
                                              preferred_element_type=F32)
        m_new_row = jnp.where(lane == h, m_l, m_new_row)
        hh = hh * lax.rsqrt(jnp.mean(hh * hh, axis=-1, keepdims=True) + EPS) * ng_ref[...]
        o_ref[:, lo:hi] = hh * jax.nn.sigmoid(zo_ref[:, lo:hi])
    mm_ref[...] = m_new_row

    @pl.when(j == last)
    def _():
        conv_ref[0] = tail
        m_ref[0] = m_new_row
        for h in range(H_D):
            c_ref[0, h] = st_ref[h, :, 0:DV_D]
            n_ref[0, h:h + 1, :] = st_ref[h, :, DV_D:2 * DV_D].T[0:1, :]


def _mlstm(z, conv_buf, c0, n0, m0, conv_w, conv_b, gate_b, norm_g, b, t):
    c = min(t, 128)
    nc = t // c
    tri = jnp.asarray(np.tril(np.ones((c, c), np.float32)), BF16)
    gb = jnp.zeros((1, 128), F32).at[0, SM_I:SM_I + H_D].set(gate_b[0]).at[0, SM_F:SM_F + H_D].set(gate_b[1])
    m0p = jnp.zeros((b, 1, 128), F32).at[:, 0, :H_D].set(m0)
    w = H_D * DV_D
    blk = lambda width, k: pl.BlockSpec((c, width), lambda i, j, k=k: (i * nc + j, k))
    per_b = lambda shape: pl.BlockSpec((1,) + shape, lambda i, j: (i,) + (0,) * len(shape))
    o, conv, cc, nn, mm = pl.pallas_call(
        functools.partial(_mlstm_kernel, c=c),
        grid=(b, nc),
        in_specs=[blk(QK_CH_D, OZ_QK // QK_CH_D), blk(w, OZ_V // w), blk(w, OZ_O // w), blk(128, OZ_SM // 128),
                  per_b((CONV_W - 1, QK_CH_D)), per_b((H_D, DK_D, DV_D)), per_b((H_D, DK_D)), per_b((1, 128)),
                  _resident((CONV_W, QK_CH_D)), _resident((1, QK_CH_D)), _resident((1, 128)), _resident((1, DV_D)),
                  _resident((c, c))],
        out_specs=[pl.BlockSpec((c, w), lambda i, j: (i * nc + j, 0)),
                   per_b((CONV_W - 1, QK_CH_D)), per_b((H_D, DK_D, DV_D)), per_b((H_D, DK_D)), per_b((1, 128))],
        out_shape=[jax.ShapeDtypeStruct((b * t, w), F32), jax.ShapeDtypeStruct((b, CONV_W - 1, QK_CH_D), F32),
                   jax.ShapeDtypeStruct((b, H_D, DK_D, DV_D), F32), jax.ShapeDtypeStruct((b, H_D, DK_D), F32),
                   jax.ShapeDtypeStruct((b, 1, 128), F32)],
        scratch_shapes=[pltpu.VMEM((HIST + c, QK_CH_D), F32), pltpu.VMEM((H_D, DK_D, 2 * DV_D), F32),
                        pltpu.VMEM((1, 128), F32)],
        compiler_params=_cparams("parallel", "arbitrary"),
        name="mlstm",
    )(z, z, z, z, conv_buf, c0, n0, m0p, conv_w, conv_b.reshape(1, QK_CH_D), gb, norm_g.reshape(1, DV_D), tri)
    return o, conv, cc, nn, mm[:, 0, :H_D]


EZ_DQ, EZ_DK, EZ_DV = 4 * H_A * DK_A, 4 * H_A * DK_A + H_B * DV_B, 4 * H_A * DK_A + 2 * H_B * DV_B
DIFF_TQ = 512


def _diff_prep_kernel(zk_ref, zv_ref, cos_ref, sin_ref, kv_ref):
    cosf, sinf = cos_ref[...], sin_ref[...]
    w = H_B * DV_B
    for h in range(H_B):
        kv_ref[:, h * DV_B:(h + 1) * DV_B] = _rot64(zk_ref[:, h * DV_B:(h + 1) * DV_B], cosf, sinf)
    kv_ref[:, w:2 * w] = zv_ref[...]


def _diff_prep(z, cosf, sinf, t):
    m = z.shape[0]
    tm = min(512, t)
    nt = t // tm
    w = H_B * DV_B
    return pl.pallas_call(
        _diff_prep_kernel,
        grid=(m // tm,),
        in_specs=[pl.BlockSpec((tm, w), lambda i: (i, EZ_DK // w)), pl.BlockSpec((tm, w), lambda i: (i, EZ_DV // w)),
                  pl.BlockSpec((tm, DV_B), lambda i: (i % nt, 0)), pl.BlockSpec((tm, DV_B), lambda i: (i % nt, 0))],
        out_specs=pl.BlockSpec((tm, 2 * w), lambda i: (i, 0)),
        out_shape=jax.ShapeDtypeStruct((m, 2 * w), F32),
        compiler_params=_cparams("parallel"),
        name="diff_prep",
    )(z, z, cosf, sinf)


def _diff_lambda(lv_ref, lam_init):
    lv = lv_ref[...]
    return (jnp.exp(jnp.sum(lv[0:1] * lv[1:2], axis=-1, keepdims=True))
            - jnp.exp(jnp.sum(lv[2:3] * lv[3:4], axis=-1, keepdims=True)) + lam_init)


def _diff_queries(zq, cosf, sinf):
    q = _rot64(zq, cosf, sinf)
    lane = lax.broadcasted_iota(jnp.int32, q.shape, 1)
    return jnp.concatenate([jnp.where(lane < DH_B, q, 0.0), jnp.where(lane >= DH_B, q, 0.0)], axis=0).astype(BF16)


def _diff_finish(o1, o2, lam, lam_init, ng):
    o = o1 - lam * o2
    o = o * lax.rsqrt(jnp.mean(o * o, axis=-1, keepdims=True) + EPS) * ng
    return o * (1.0 - lam_init)


def _diff_kernel(zq_ref, cos_ref, sin_ref, k_ref, v_ref, lv_ref, ng_ref, o_ref, m_ref, l_ref, acc_ref,
                 *, tq, lam_init):
    j = pl.program_id(2)
    q2 = _diff_queries(zq_ref[...] * (DH_B ** -0.5), cos_ref[...], sin_ref[...])
    _flash_init(m_ref, l_ref, acc_ref)

    def tile(jk):
        off = pl.multiple_of(jk * tq, tq)
        return k_ref[pl.ds(off, tq), :].astype(BF16), v_ref[pl.ds(off, tq), :].astype(BF16)

    def body(jk, carry):
        k, v = tile(jk)
        _flash_step_t(k, v, q2, None, m_ref, l_ref, acc_ref)
        return carry

    lax.fori_loop(0, j, body, 0)
    krow = lax.broadcasted_iota(jnp.int32, (tq, 2 * tq), 0)
    qlane = lax.broadcasted_iota(jnp.int32, (tq, 2 * tq), 1) & (tq - 1)
    k, v = tile(j)
    _flash_step_t(k, v, q2, krow <= qlane, m_ref, l_ref, acc_ref)
    o_t = acc_ref[...] / l_ref[...]
    lam = _diff_lambda(lv_ref, lam_init)
    o_ref[...] = _diff_finish(o_t[:, 0:tq].T, o_t[:, tq:2 * tq].T, lam, lam_init, ng_ref[...])


def _diff_prompt(z, kv, cosf, sinf, lam_vecs, norm_g, layer_idx, b, t):
    tq = DIFF_TQ
    nq = t // tq
    assert tq & (tq - 1) == 0 and t % tq == 0
    lam_init = 0.8 - 0.6 * math.exp(-0.3 * layer_idx)
    return pl.pallas_call(
        functools.partial(_diff_kernel, tq=tq, lam_init=lam_init),
        grid=(b, H_B, nq),
        in_specs=[pl.BlockSpec((tq, DV_B), lambda i, h, j: (i * nq + j, EZ_DQ // DV_B + h)),
                  pl.BlockSpec((tq, DV_B), lambda i, h, j: (j, 0)),
                  pl.BlockSpec((tq, DV_B), lambda i, h, j: (j, 0)),
                  pl.BlockSpec((t, DV_B), lambda i, h, j: (i, h)),
                  pl.BlockSpec((t, DV_B), lambda i, h, j: (i, H_B + h)),
                  _resident(lam_vecs.shape), _resident((1, DV_B))],
        out_specs=pl.BlockSpec((tq, DV_B), lambda i, h, j: (i * nq + j, h)),
        out_shape=jax.ShapeDtypeStruct((b * t, H_B * DV_B), F32),
        scratch_shapes=[pltpu.VMEM((1, 2 * tq), F32), pltpu.VMEM((1, 2 * tq), F32), pltpu.VMEM((DV_B, 2 * tq), F32)],
        compiler_params=_cparams("parallel", "parallel", "arbitrary"),
        name="diff_prompt",
    )(z, cosf, sinf, kv, kv, lam_vecs, norm_g.reshape(1, DV_B))


DEC_PAGES = 16


def _pool_rows(pool):
    return pool.reshape(pool.shape[0], pool.shape[1], -1, pool.shape[-1])


def _page_spec(pool4, layer, page_of):
    return pl.BlockSpec((1, 1) + pool4.shape[2:], lambda i, s, pt: (layer, page_of(i, s, pt), 0, 0))


def _page_rows(pg, slot, n_slots):
    return pg[0, 0, pl.ds(slot, PAGE_SIZE, stride=n_slots), :]


def _diff_dec_kernel(pt_ref, zq_ref, zk_ref, zv_ref, cos_ref, sin_ref, lv_ref, ng_ref, *rest, tq, n_pg, lam_init):
    pages = rest[:n_pg]
    o_ref, kv_ref, m_ref, l_ref, acc_ref, q2_ref = rest[n_pg:]
    step = pl.program_id(1)
    scale = DH_B ** -0.5
    w = H_B * DV_B
    cosf, sinf = cos_ref[...], sin_ref[...]

    @pl.when(step == 0)
    def _():
        m_ref[...] = jnp.full(m_ref.shape, NEG, F32)
        l_ref[...] = jnp.zeros(l_ref.shape, F32)
        acc_ref[...] = jnp.zeros(acc_ref.shape, F32)
        for h in range(H_B):
            q2_ref[h] = _diff_queries(zq_ref[:, h * DV_B:(h + 1) * DV_B], cosf, sinf)

    rows = 2 * tq
    sc = jnp.concatenate(
        [jnp.concatenate([_dot_nt(q2_ref[h], _page_rows(pg, h, 2 * H_B).astype(BF16)) for pg in pages], axis=1)
         for h in range(H_B)], axis=0) * scale
    m_old = jnp.concatenate([m_ref[h] for h in range(H_B)], axis=0)
    l_old = jnp.concatenate([l_ref[h] for h in range(H_B)], axis=0)
    m_new = jnp.maximum(m_old, jnp.max(sc, axis=-1, keepdims=True))
    alpha = jnp.exp(m_old - m_new)
    p = jnp.exp(sc - m_new)
    l_new = alpha * l_old + jnp.sum(p, axis=-1, keepdims=True)
    for h in range(H_B):
        r = slice(h * rows, (h + 1) * rows)
        pv = jnp.zeros((rows, DV_B), F32)
        for i, pg in enumerate(pages):
            pv = pv + jnp.dot(p[r, i * PAGE_SIZE:(i + 1) * PAGE_SIZE].astype(BF16),
                              _page_rows(pg, H_B + h, 2 * H_B).astype(BF16),
                              preferred_element_type=F32)
        acc_ref[h] = alpha[r] * acc_ref[h] + pv
        m_ref[h] = m_new[r]
        l_ref[h] = l_new[r]

    @pl.when(step == pl.num_programs(1) - 1)
    def _():
        lam = _diff_lambda(lv_ref, lam_init)
        qi = lax.broadcasted_iota(jnp.int32, (2 * tq, tq), 0) & (tq - 1)
        ki = lax.broadcasted_iota(jnp.int32, (2 * tq, tq), 1)
        kv_ref[:, w:2 * w] = zv_ref[...]
        for h in range(H_B):
            lo, hi = h * DV_B, (h + 1) * DV_B
            k_new = _rot64(zk_ref[:, lo:hi], cosf, sinf)
            kv_ref[:, lo:hi] = k_new
            s_new = _dot_nt(q2_ref[h].astype(F32), k_new) * scale
            valid = ki <= qi
            m_old = m_ref[h]
            m_new = jnp.maximum(m_old, jnp.max(jnp.where(valid, s_new, NEG), axis=-1, keepdims=True))
            alpha = jnp.exp(m_old - m_new)
            p = jnp.where(valid, jnp.exp(s_new - m_new), 0.0)
            l = alpha * l_ref[h] + jnp.sum(p, axis=-1, keepdims=True)
            acc = alpha * acc_ref[h] + jnp.dot(p, zv_ref[:, lo:hi], preferred_element_type=F32)
            o_ref[:, lo:hi] = _diff_finish(acc[0:tq] / l[0:tq], acc[tq:2 * tq] / l[tq:2 * tq], lam, lam_init,
                                           ng_ref[...])


def _diff_decode(z, pool, layer, page_table, cosf, sinf, lam_vecs, norm_g, layer_idx, b, t):
    n_pages = page_table.shape[1]
    n_pg = DEC_PAGES
    assert n_pages % n_pg == 0 and t & (t - 1) == 0 and t % 8 == 0
    w = H_B * DV_B
    lam_init = 0.8 - 0.6 * math.exp(-0.3 * layer_idx)
    pool4 = _pool_rows(pool)
    zcol = lambda k: pl.BlockSpec((t, w), lambda i, s, pt, k=k: (i, k))
    const = lambda shape: pl.BlockSpec(shape, lambda i, s, pt: (0,) * len(shape))
    page = lambda p: _page_spec(pool4, layer, lambda i, s, pt, p=p: pt[i, s * n_pg + p])
    grid_spec = pltpu.PrefetchScalarGridSpec(
        num_scalar_prefetch=1,
        grid=(b, n_pages // n_pg),
        in_specs=[zcol(EZ_DQ // w), zcol(EZ_DK // w), zcol(EZ_DV // w), const((t, DV_B)), const((t, DV_B)),
                  const(lam_vecs.shape), const((1, DV_B))] + [page(p) for p in range(n_pg)],
        out_specs=[pl.BlockSpec((t, w), lambda i, s, pt: (i, 0)), pl.BlockSpec((t, 2 * w), lambda i, s, pt: (i, 0))],
        scratch_shapes=[pltpu.VMEM((H_B, 2 * t, 1), F32), pltpu.VMEM((H_B, 2 * t, 1), F32),
                        pltpu.VMEM((H_B, 2 * t, DV_B), F32), pltpu.VMEM((H_B, 2 * t, DV_B), BF16)])
    return pl.pallas_call(
        functools.partial(_diff_dec_kernel, tq=t, n_pg=n_pg, lam_init=lam_init),
        grid_spec=grid_spec,
        out_shape=[jax.ShapeDtypeStruct((b * t, w), F32), jax.ShapeDtypeStruct((b * t, 2 * w), F32)],
        compiler_params=_cparams("parallel", "arbitrary"),
        name="diff_decode",
    )(page_table, z, z, z, cosf, sinf, lam_vecs, norm_g.reshape(1, DV_B), *([pool4] * n_pg))


CMP_PAGES = 16


def _nsa_dec_cmp_kernel(pt_ref, pe_ref, w_ref, *rest, n_pg):
    pages = rest[:n_pg]
    cmp_ref, x_ref = rest[n_pg:]
    nb = n_pg * PAGE_SIZE // CMP_BLK
    per_page = PAGE_SIZE // CMP_BLK
    for p, pg in enumerate(pages):
        for i in range(2 * G_C):
            v = _page_rows(pg, i, 4 * G_C)
            for n in range(per_page):
                for t in range(CMP_BLK // 8):
                    src = n * CMP_BLK + t * 8
                    dst = (t * nb + p * per_page + n) * 8
                    x_ref[i, dst:dst + 8, :] = v[src:src + 8]
    _compress_blocks(x_ref, pe_ref, w_ref, cmp_ref, nb, tiled=True)


def _nsa_dec_cmp(pool4, layer, page_table, cmp_w, cmp_pe):
    b, n_pages = page_table.shape
    n_pg = CMP_PAGES
    assert n_pages % n_pg == 0 and PAGE_SIZE % CMP_BLK == 0
    gw = G_C * DK_C
    nb = n_pg * PAGE_SIZE // CMP_BLK
    steps = n_pages // n_pg
    const = lambda shape: pl.BlockSpec(shape, lambda i, s, pt: (0,) * len(shape))
    page = lambda p: _page_spec(pool4, layer, lambda i, s, pt, p=p: pt[i, s * n_pg + p])
    grid_spec = pltpu.PrefetchScalarGridSpec(
        num_scalar_prefetch=1,
        grid=(b, steps),
        in_specs=[const(cmp_pe.shape), const(cmp_w.shape)] + [page(p) for p in range(n_pg)],
        out_specs=pl.BlockSpec((nb, 2 * gw), lambda i, s, pt: (i * steps + s, 0)),
        scratch_shapes=[pltpu.VMEM((2 * G_C, n_pg * PAGE_SIZE, DK_C), F32)])
    return pl.pallas_call(
        functools.partial(_nsa_dec_cmp_kernel, n_pg=n_pg),
        grid_spec=grid_spec,
        out_shape=jax.ShapeDtypeStruct((b * steps * nb, 2 * gw), F32),
        compiler_params=_cparams("parallel", "arbitrary"),
        name="nsa_dec_cmp",
    )(page_table, cmp_pe, cmp_w.astype(BF16), *([pool4] * n_pg))


def _softmax_rows(s):
    e = jnp.exp(s - jnp.max(s, axis=-1, keepdims=True))
    return e / jnp.sum(e, axis=-1, keepdims=True)


def _nsa_dec_kernel(pt_ref, zq_ref, zkv_ref, zsm_ref, cos_ref, sin_ref, cmp_ref, win_ref, e_ref, *rest,
                    tq, n_pg, pos0):
    pages = rest[:n_pg]
    o_ref, rows_ref, wout_ref, sel_ref, q_ref, ocmp_ref, m_ref, l_ref, acc_ref = rest[n_pg:]
    step = pl.program_id(1)
    scale = DK_C ** -0.5
    gw = G_C * DK_C
    n_c = cmp_ref.shape[0]
    cosf, sinf = cos_ref[...], sin_ref[...]
    rows2 = HPG_C * tq
    tpos = pos0 + (lax.broadcasted_iota(jnp.int32, (rows2, 1), 0) & (tq - 1))

    def stacked(fn):
        return lambda g: jnp.concatenate([fn(zq_ref[:, (g * HPG_C + p) * DK_C:(g * HPG_C + p + 1) * DK_C])
                                          for p in range(HPG_C)], axis=0)

    @pl.when(step == 0)
    def _():
        m_ref[...] = jnp.full(m_ref.shape, NEG, F32)
        l_ref[...] = jnp.zeros(l_ref.shape, F32)
        acc_ref[...] = jnp.zeros(acc_ref.shape, F32)
        lane = lax.broadcasted_iota(jnp.int32, (rows2, n_c), 1)
        lane_t = lane[0:tq]
        lanef = lane_t.astype(F32)
        for g in range(G_C):
            kc = cmp_ref[:, g * DK_C:(g + 1) * DK_C].astype(BF16)
            vc = cmp_ref[:, gw + g * DK_C:gw + (g + 1) * DK_C].astype(BF16)
            qg = stacked(lambda x: x)(g).astype(BF16)
            cmask = ((lane + 1) * CMP_BLK - 1) <= tpos
            pc = _softmax_rows(jnp.where(cmask, _dot_nt(qg, kc) * scale, NEG))
            pc = jnp.where(tpos >= CMP_BLK - 1, pc, 0.0)
            ocmp_ref[g] = jnp.dot(pc.astype(BF16), vc, preferred_element_type=F32)
            imp = functools.reduce(lambda a, b: a + b, [pc[p * tq:(p + 1) * tq] for p in range(HPG_C)])
            imp2 = jnp.concatenate([imp[:, c0:c0 + 128] + pltpu.roll(imp[:, c0:c0 + 128], 127, axis=1)
                                    for c0 in range(0, n_c, 128)], axis=1)
            cur2 = (tpos[0:tq] // SEL_BLK) * 2
            valid = ((lane_t & 1) == 0) & (lane_t <= cur2)
            forced = (lane_t == 0) | (lane_t == cur2) | (lane_t == cur2 - 2)
            score = jnp.where(valid, imp2 + jnp.where(forced, FORCE_BONUS, 0.0), -1.0)
            sel = jnp.zeros((tq, n_c), F32)
            for _ in range(N_SEL - 1):
                top = jnp.max(score, axis=-1, keepdims=True)
                idx = jnp.min(jnp.where(score == top, lanef, float(n_c)), axis=-1, keepdims=True)
                pick = lanef == idx
                sel = jnp.where(pick, jnp.where(top > -0.5, 1.0, 0.0), sel)
                score = jnp.where(pick, -3.0, score)
            sel_ref[g] = jnp.concatenate([sel] * HPG_C, axis=0).astype(BF16)
            q_ref[g] = stacked(lambda x: _rot128(x, cosf, sinf))(g).astype(BF16)

    sc = jnp.concatenate(
        [jnp.concatenate([_dot_nt(q_ref[g], _page_rows(pg, 2 * G_C + g, 4 * G_C).astype(BF16)) for pg in pages], axis=1)
         for g in range(G_C)], axis=0) * scale
    ok = jnp.concatenate(
        [jnp.concatenate([jnp.dot(sel_ref[g], e_ref[p], preferred_element_type=F32) for p in range(n_pg)], axis=1)
         for g in range(G_C)], axis=0) > 0.5
    m_old = jnp.concatenate([m_ref[g] for g in range(G_C)], axis=0)
    l_old = jnp.concatenate([l_ref[g] for g in range(G_C)], axis=0)
    m_new = jnp.maximum(m_old, jnp.max(jnp.where(ok, sc, NEG), axis=-1, keepdims=True))
    alpha = jnp.exp(m_old - m_new)
    p_ = jnp.where(ok, jnp.exp(sc - m_new), 0.0)
    l_new = alpha * l_old + jnp.sum(p_, axis=-1, keepdims=True)
    for g in range(G_C):
        r = slice(g * rows2, (g + 1) * rows2)
        pv = jnp.zeros((rows2, DV_C), F32)
        for i, pg in enumerate(pages):
            pv = pv + jnp.dot(p_[r, i * PAGE_SIZE:(i + 1) * PAGE_SIZE].astype(BF16),
                              _page_rows(pg, 3 * G_C + g, 4 * G_C).astype(BF16), preferred_element_type=F32)
        acc_ref[g] = alpha[r] * acc_ref[g] + pv
        m_ref[g] = m_new[r]
        l_ref[g] = l_new[r]

    @pl.when(step == pl.num_programs(1) - 1)
    def _():
        qi = lax.broadcasted_iota(jnp.int32, (rows2, tq), 0) & (tq - 1)
        ki = lax.broadcasted_iota(jnp.int32, (rows2, tq), 1)
        causal = ki <= qi
        wj = lax.broadcasted_iota(jnp.int32, (rows2, WINDOW), 1)
        wq = lax.broadcasted_iota(jnp.int32, (rows2, WINDOW), 0) & (tq - 1)
        in_win = wj >= wq
        rows_ref[:, 0:2 * gw] = zkv_ref[:, 0:2 * gw]
        rows_ref[:, 3 * gw:4 * gw] = zkv_ref[:, 3 * gw:4 * gw]
        wout_ref[0, 0:WINDOW - tq, :] = win_ref[0, tq:WINDOW, :]
        wout_ref[0, WINDOW - tq:WINDOW, gw:2 * gw] = zkv_ref[:, 5 * gw:6 * gw]
        for g in range(G_C):
            lo = g * DK_C
            q = q_ref[g].astype(F32)
            ks_new = _rot128(zkv_ref[:, 2 * gw + lo:2 * gw + lo + DK_C], cosf, sinf)
            kw_new = _rot128(zkv_ref[:, 4 * gw + lo:4 * gw + lo + DK_C], cosf, sinf)
            vs_new = zkv_ref[:, 3 * gw + lo:3 * gw + lo + DK_C]
            vw_new = zkv_ref[:, 5 * gw + lo:5 * gw + lo + DK_C]
            rows_ref[:, 2 * gw + lo:2 * gw + lo + DK_C] = ks_new
            wout_ref[0, WINDOW - tq:WINDOW, lo:lo + DK_C] = kw_new
            s_new = _dot_nt(q, ks_new) * scale
            m_old = m_ref[g]
            m_new = jnp.maximum(m_old, jnp.max(jnp.where(causal, s_new, NEG), axis=-1, keepdims=True))
            alpha = jnp.exp(m_old - m_new)
            p_ = jnp.where(causal, jnp.exp(s_new - m_new), 0.0)
            l_sel = alpha * l_ref[g] + jnp.sum(p_, axis=-1, keepdims=True)
            o_sel = (alpha * acc_ref[g] + jnp.dot(p_, vs_new, preferred_element_type=F32)) / l_sel
            kw = win_ref[0, :, lo:lo + DK_C].astype(BF16)
            vw = win_ref[0, :, gw + lo:gw + lo + DK_C].astype(BF16)
            s_buf = jnp.where(in_win, _dot_nt(q_ref[g], kw) * scale, NEG)
            s_own = jnp.where(causal, _dot_nt(q, kw_new) * scale, NEG)
            m_w = jnp.maximum(jnp.max(s_buf, axis=-1, keepdims=True), jnp.max(s_own, axis=-1, keepdims=True))
            p_buf = jnp.where(in_win, jnp.exp(s_buf - m_w), 0.0)
            p_own = jnp.where(causal, jnp.exp(s_own - m_w), 0.0)
            l_w = jnp.sum(p_buf, axis=-1, keepdims=True) + jnp.sum(p_own, axis=-1, keepdims=True)
            o_win = (jnp.dot(p_buf.astype(BF16), vw, preferred_element_type=F32)
                     + jnp.dot(p_own, vw_new, preferred_element_type=F32)) / l_w
            o_cmp = ocmp_ref[g]
            for p in range(HPG_C):
                h = g * HPG_C + p
                gate = lambda c: jax.nn.sigmoid(zsm_ref[:, SM_GATE + h * 3 + c:SM_GATE + h * 3 + c + 1])
                r = slice(p * tq, (p + 1) * tq)
                o_ref[:, h * DV_C:(h + 1) * DV_C] = gate(0) * o_cmp[r] + gate(1) * o_sel[r] + gate(2) * o_win[r]


def _nsa_decode(z, pool, layer, page_table, win_buf, cosf, sinf, cmp_w, cmp_pe, pos0, b, t):
    n_pages = page_table.shape[1]
    n_pg = DEC_PAGES
    gw = G_C * DK_C
    n_c = n_pages * PAGE_SIZE // CMP_BLK
    assert pos0 == n_pages * PAGE_SIZE and pos0 % SEL_BLK == 0 and t <= SEL_BLK and t % 8 == 0 and t & (t - 1) == 0
    assert n_pages % n_pg == 0 and n_c % 128 == 0 and N_SEL >= 3 and win_buf.shape[1] == WINDOW and G_C == 2
    pool4 = _pool_rows(pool)
    cmp = _nsa_dec_cmp(pool4, layer, page_table, cmp_w, cmp_pe)
    win3 = win_buf.reshape(b, WINDOW, 2 * gw)
    sb = np.arange(n_c)[None, :, None]
    key = np.arange(n_pages)[:, None, None] * PAGE_SIZE + np.arange(PAGE_SIZE)[None, None, :]
    expand = jnp.asarray((sb % 2 == 0) & (key // SEL_BLK == sb // 2), BF16)
    zblk = lambda width, k: pl.BlockSpec((t, width), lambda i, s, pt, k=k: (i, k))
    const = lambda shape: pl.BlockSpec(shape, lambda i, s, pt: (0,) * len(shape))
    page = lambda p: _page_spec(pool4, layer, lambda i, s, pt, p=p: pt[i, s * n_pg + p])
    rows2 = HPG_C * t
    grid_spec = pltpu.PrefetchScalarGridSpec(
        num_scalar_prefetch=1,
        grid=(b, n_pages // n_pg),
        in_specs=[zblk(H_C * DK_C, OZ_Q // (H_C * DK_C)), zblk(6 * gw, OZ_KV // (6 * gw)), zblk(128, OZ_SM // 128),
                  const((t, DK_C)), const((t, DK_C)),
                  pl.BlockSpec((n_c, 2 * gw), lambda i, s, pt: (i, 0)),
                  pl.BlockSpec((1, WINDOW, 2 * gw), lambda i, s, pt: (i, 0, 0)),
                  pl.BlockSpec((n_pg, n_c, PAGE_SIZE), lambda i, s, pt: (s, 0, 0))] + [page(p) for p in range(n_pg)],
        out_specs=[pl.BlockSpec((t, H_C * DV_C), lambda i, s, pt: (i, 0)),
                   pl.BlockSpec((t, 4 * gw), lambda i, s, pt: (i, 0)),
                   pl.BlockSpec((1, WINDOW, 2 * gw), lambda i, s, pt: (i, 0, 0))],
        scratch_shapes=[pltpu.VMEM((G_C, rows2, n_c), BF16), pltpu.VMEM((G_C, rows2, DK_C), BF16),
                        pltpu.VMEM((G_C, rows2, DV_C), F32), pltpu.VMEM((G_C, rows2, 1), F32),
                        pltpu.VMEM((G_C, rows2, 1), F32), pltpu.VMEM((G_C, rows2, DV_C), F32)])
    o, rows, wout = pl.pallas_call(
        functools.partial(_nsa_dec_kernel, tq=t, n_pg=n_pg, pos0=pos0),
        grid_spec=grid_spec,
        out_shape=[jax.ShapeDtypeStruct((b * t, H_C * DV_C), F32), jax.ShapeDtypeStruct((b * t, 4 * gw), F32),
                   jax.ShapeDtypeStruct((b, WINDOW, 2 * gw), F32)],
        compiler_params=_cparams("parallel", "arbitrary"),
        name="nsa_decode",
    )(page_table, z, z, z, cosf, sinf, cmp, win3, expand, *([pool4] * n_pg))
    return o, rows, wout


def _run_group(x, pos0, mem_kv, past, W):
    B, T, D = x.shape
    pos = pos0 + jnp.arange(T, dtype=jnp.int32)
    cos32, sin32 = _rope_tables(pos, DH_B // 2, DV_B)
    cos64, sin64 = _rope_tables(pos, DK_C // 2, DK_C)
    new = {name: [] for name in ('diff_kv', 'hgrn', 'nsa_kv', 'nsa_win', 'm_C', 'm_n', 'm_m', 'm_conv')}
    x = x.reshape(B * T, D)
    for l in range(DEPTH):
        g = W['norm_w'][l]
        x = _ffn(x, g[0], W['ffn_w_gate'][l, 0], W['ffn_w_up'][l, 0], W['ffn_w_down'][l, 0])
        i = l // 2
        if l % 2 == 0:
            z = _normmm(x, g[1], W['w_in_even'][i])
            S0 = jnp.zeros((B, H_A, DK_A, DV_A), F32) if past is None else past['hgrn'][i]
            oa, S = _hgrn(z, S0, W['hgrn_lb_raw'], W['hgrn_norm'][i], i, B, T)
            if past is None:
                kv_new = _diff_prep(z, cos32, sin32, T)
                ob = _diff_prompt(z, kv_new, cos32, sin32, W['diff_lam'][i], W['diff_norm'][i], l, B, T)
                kv_new = kv_new.reshape(B, T, 2, H_B, DV_B)
            else:
                ob, kv_new = _diff_decode(z, past['diff_pool'], i, past['page_table'], cos32, sin32,
                                          W['diff_lam'][i], W['diff_norm'][i], l, B, T)
                kv_new = kv_new.reshape(B, T, 2, H_B, DV_B)
            w_out = W['w_out_even'][i]
            new['hgrn'].append(S)
            new['diff_kv'].append(kv_new)
        else:
            z = _normmm(x, g[1], W['w_in_odd'][i])
            if past is None:
                conv_buf = jnp.zeros((B, CONV_W - 1, QK_CH_D), F32)
                C0 = jnp.zeros((B, H_D, DK_D, DV_D), F32)
                n0 = jnp.zeros((B, H_D, DK_D), F32)
                m0 = jnp.zeros((B, H_D), F32)
                rows, wrows, cmp = _nsa_prep(z, cos64, sin64, W['nsa_cmp_w'][i], W['nsa_cmp_pe'][i], T)
                oa = _nsa_prompt(z, rows, wrows, cmp, cos64, sin64, B, T)
                rows = rows.reshape(B, T, 4, G_C, DK_C)
                win_new = wrows.reshape(B, T, 2, G_C, DK_C)[:, -min(WINDOW, T):]
            else:
                conv_buf, C0, n0, m0 = past['m_conv'][i], past['m_C'][i], past['m_n'][i], past['m_m'][i]
                oa, rows, win_new = _nsa_decode(z, past['nsa_pool'], i, past['page_table'], past['nsa_win'][i],
                                                cos64, sin64, W['nsa_cmp_w'][i], W['nsa_cmp_pe'][i], pos0, B, T)
                rows = rows.reshape(B, T, 4, G_C, DK_C)
                win_new = win_new.reshape(B, WINDOW, 2, G_C, DK_C)
            ob, conv_new, C, n, m = _mlstm(z, conv_buf, C0, n0, m0, W['mlstm_conv_w'][i], W['mlstm_conv_b'][i],
                                           W['mlstm_gate_b'][i], W['mlstm_norm'][i], B, T)
            w_out = W['w_out_odd'][i]
            new['nsa_kv'].append(rows)
            new['nsa_win'].append(win_new)
            new['m_C'].append(C)
            new['m_n'].append(n)
            new['m_m'].append(m)
            new['m_conv'].append(conv_new)
        x = _mmres(x, [oa.reshape(B * T, GROUP_W), ob.reshape(B * T, GROUP_W)], [w_out[:GROUP_W], w_out[GROUP_W:]])
        q = _normmm(x, g[2], W['mem_wq'][l])
        a = _xattn(q, mem_kv[l], T) if past is None else _xattn(q, mem_kv, T, layer=l)
        x = _mmres(x, [a], [W['mem_wo'][l]])
        x = _ffn(x, g[3], W['ffn_w_gate'][l, 1], W['ffn_w_up'][l, 1], W['ffn_w_down'][l, 1])
    return _rmsnorm_rows(x, W['norm_final']).reshape(B, T, D), new


def kernel(x_prompt, x_sample, mem_prompt, cache_diff_kv, cache_nsa_kv, state_nsa_win, state_hgrn, state_mlstm_C, state_mlstm_n, state_mlstm_m, state_mlstm_conv, cache_mem_kv, page_table, norm_w, norm_final, ffn_w_gate, ffn_w_up, ffn_w_down, w_in_even, w_out_even, w_in_odd, w_out_odd, hgrn_lb_raw, hgrn_norm, diff_lam, diff_norm, nsa_cmp_w, nsa_cmp_pe, mlstm_conv_w, mlstm_conv_b, mlstm_gate_b, mlstm_norm, mem_wq, mem_wkv, mem_wo):
    bf = lambda w: w.astype(BF16)
    o_q, o_kv, o_g, o_qk, o_v, o_i, o_f, o_o = (int(c) for c in np.cumsum((0,) + ODD_SPLITS)[[0, 1, 7, 8, 9, 10, 11, 12]])
    w_in_odd = jnp.concatenate(
        [w_in_odd[..., o_kv:o_g], w_in_odd[..., o_q:o_kv], w_in_odd[..., o_qk:o_v], w_in_odd[..., o_v:o_i],
         w_in_odd[..., o_o:], w_in_odd[..., o_g:o_qk], w_in_odd[..., o_i:o_o],
         jnp.zeros(w_in_odd.shape[:2] + (OZ_WIDTH - IN_ODD,), w_in_odd.dtype)], axis=-1)
    W = {'norm_w': norm_w, 'norm_final': norm_final, 'ffn_w_gate': bf(ffn_w_gate), 'ffn_w_up': bf(ffn_w_up),
         'ffn_w_down': bf(ffn_w_down), 'w_in_even': bf(w_in_even), 'w_out_even': bf(w_out_even),
         'w_in_odd': bf(w_in_odd), 'w_out_odd': bf(w_out_odd),
         'hgrn_lb_raw': hgrn_lb_raw, 'hgrn_norm': hgrn_norm, 'diff_lam': diff_lam,
         'diff_norm': diff_norm, 'nsa_cmp_w': nsa_cmp_w, 'nsa_cmp_pe': nsa_cmp_pe, 'mlstm_conv_w': mlstm_conv_w,
         'mlstm_conv_b': mlstm_conv_b, 'mlstm_gate_b': mlstm_gate_b, 'mlstm_norm': mlstm_norm,
         'mem_wq': bf(mem_wq), 'mem_wo': bf(mem_wo)}
    Bp, M = mem_prompt.shape[0], mem_prompt.shape[1]
    wkv = bf(mem_wkv)
    ones = jnp.ones((D_MODEL,), F32)
    mem_flat = mem_prompt.reshape(Bp * M, D_MODEL)
    mem_kv_prompt = [_normmm(mem_flat, ones, wkv[l], norm=False).reshape(Bp, M, 2 * MEM_W) for l in range(DEPTH)]
    y_prompt, newp = _run_group(x_prompt, 0, mem_kv_prompt, None, W)
    past = {'page_table': page_table, 'diff_pool': cache_diff_kv, 'nsa_pool': cache_nsa_kv,
            'nsa_win': state_nsa_win, 'hgrn': state_hgrn, 'm_C': state_mlstm_C, 'm_n': state_mlstm_n,
            'm_m': state_mlstm_m, 'm_conv': state_mlstm_conv}
    past_len = page_table.shape[1] * PAGE_SIZE
    y_sample, news = _run_group(x_sample, past_len, cache_mem_kv, past, W)
    mem_out = jnp.stack(mem_kv_prompt).reshape(DEPTH, Bp, M, 2, MEM_HEADS, MEM_DH)
    return (y_prompt, y_sample,
            jnp.stack(newp['diff_kv']), jnp.stack(news['diff_kv']),
            jnp.stack(newp['nsa_kv']), jnp.stack(news['nsa_kv']),
            jnp.stack(newp['nsa_win']), jnp.stack(news['nsa_win']),
            jnp.stack(newp['hgrn']), jnp.stack(news['hgrn']),
            jnp.stack(newp['m_C']), jnp.stack(news['m_C']),
            jnp.stack(newp['m_n']), jnp.stack(news['m_n']),
            jnp.stack(newp['m_m']), jnp.stack(news['m_m']),
            jnp.stack(newp['m_conv']), jnp.stack(news['m_conv']),
            mem_out)
```
